```python
import math
import jax, jax.numpy as jnp
from jax import lax
import numpy as np

D_MODEL = 2048
BATCH = 4
SEQ = 4096
DEPTH = 1
DEC_BATCH = 32
DEC_SEQ = 1
PAST_LEN = 16384
PAGE_SIZE = 128

N_ATT_HEADS = 8
ATT_HEAD_DIM = 128
ATT_W = N_ATT_HEADS * ATT_HEAD_DIM
ROT_DIM = ATT_HEAD_DIM // 4
ROPE_THETA = 500000.0
MOBA_BLOCK = 256
MOBA_TOP_K = 3
Q_CHUNK = 32
N_RWKV_HEADS = 16
RWKV_HEAD_DIM = 64
RWKV_W = N_RWKV_HEADS * RWKV_HEAD_DIM
DECAY_RANK = 64
ICLR_RANK = 64
GATE_RANK = 160
SHIFT_W = 3 * RWKV_W + DECAY_RANK + ICLR_RANK + GATE_RANK
IN_W = 3 * ATT_W + SHIFT_W + 2 * D_MODEL
D_FF = -(-8 * D_MODEL // (3 * 256)) * 256
RMS_EPS = 1e-6
GN_EPS = 64e-5
L2_EPS = 1e-12

kernel_name = 'moba_rwkv7_gated_hybrid_step'

F32 = jnp.float32


def rms_norm(x, g):
    xf = x.astype(F32)
    y = xf * lax.rsqrt(jnp.mean(xf * xf, axis=-1, keepdims=True) + RMS_EPS) * g.astype(F32)
    return y.astype(x.dtype)


def partial_rotary(x, pos):
    half = ROT_DIM // 2
    inv = jnp.exp(jnp.arange(half, dtype=F32) * (-2.0 * math.log(ROPE_THETA) / ROT_DIM))
    ang = pos.astype(F32)[:, None] * inv[None, :]
    cos = jnp.cos(ang)[:, None, :]
    sin = jnp.sin(ang)[:, None, :]
    xf = x.astype(F32)
    x1 = xf[..., :half]
    x2 = xf[..., half:ROT_DIM]
    out = jnp.concatenate([x1 * cos - x2 * sin, x2 * cos + x1 * sin, xf[..., ROT_DIM:]], axis=-1)
    return out.astype(x.dtype)


def moba_prompt(q, k, v):
    n, s, h, dh = q.shape
    nb = -(-s // MOBA_BLOCK)
    sp = nb * MOBA_BLOCK
    padw = ((0, 0), (0, sp - s), (0, 0), (0, 0))
    qp, kp, vp = jnp.pad(q, padw), jnp.pad(k, padw), jnp.pad(v, padw)
    kbt = kp.reshape(n, nb, MOBA_BLOCK, h, dh).transpose(0, 3, 1, 2, 4)
    vbt = vp.reshape(n, nb, MOBA_BLOCK, h, dh).transpose(0, 3, 1, 2, 4)
    k_mean = jnp.mean(kbt.astype(F32), axis=3)
    gate = jnp.einsum('bshd,bhnd->bshn', qp.astype(F32), k_mean)
    q_blk = jnp.arange(sp) // MOBA_BLOCK
    is_past = jnp.arange(nb)[None, :] < q_blk[:, None]
    gate = jnp.where(is_past[None, :, None, :], gate, -jnp.inf)
    n_sel = min(MOBA_TOP_K, nb)
    _, sel = lax.top_k(gate, n_sel)
    sel_ok = sel < q_blk[None, :, None, None]
    b_i = jnp.arange(n)[:, None, None]
    h_i = jnp.arange(h)[None, :, None]
    offs = jnp.arange(MOBA_BLOCK)

    def chunk(c):
        q0 = c * Q_CHUNK
        blk = q0 // MOBA_BLOCK
        qc = lax.dynamic_slice_in_dim(qp, q0, Q_CHUNK, axis=1)
        selc = lax.dynamic_slice_in_dim(sel, q0, Q_CHUNK, axis=1).transpose(0, 2, 1, 3)
        okc = lax.dynamic_slice_in_dim(sel_ok, q0, Q_CHUNK, axis=1).transpose(0, 2, 1, 3)
        idx = selc.reshape(n, h, Q_CHUNK * n_sel)
        k_sel = kbt[b_i, h_i, idx].reshape(n, h, Q_CHUNK, n_sel, MOBA_BLOCK, dh)
        v_sel = vbt[b_i, h_i, idx].reshape(n, h, Q_CHUNK, n_sel, MOBA_BLOCK, dh)
        k_own = lax.dynamic_index_in_dim(kbt, blk, axis=2, keepdims=False)
        v_own = lax.dynamic_index_in_dim(vbt, blk, axis=2, keepdims=False)
        s_sel = jnp.einsum('bqhd,bhqnrd->bhqnr', qc, k_sel, preferred_element_type=F32)
        s_sel = jnp.where(okc[..., None], s_sel, -jnp.inf)
        s_own = jnp.einsum('bqhd,bhrd->bhqr', qc, k_own, preferred_element_type=F32)
        causal = (blk * MOBA_BLOCK + offs)[None, :] <= (q0 + jnp.arange(Q_CHUNK))[:, None]
        s_own = jnp.where(causal[None, None], s_own, -jnp.inf)
        scores = jnp.concatenate([s_sel.reshape(n, h, Q_CHUNK, n_sel * MOBA_BLOCK), s_own], axis=-1)
        prob = jax.nn.softmax(scores, axis=-1).astype(v.dtype)
        p_sel = prob[..., :n_sel * MOBA_BLOCK].reshape(n, h, Q_CHUNK, n_sel, MOBA_BLOCK)
        p_own = prob[..., n_sel * MOBA_BLOCK:]
        return (jnp.einsum('bhqnr,bhqnrd->bqhd', p_sel, v_sel)
                + jnp.einsum('bhqr,bhrd->bqhd', p_own, v_own))

    out = lax.map(chunk, jnp.arange(sp // Q_CHUNK))
    return out.transpose(1, 0, 2, 3, 4).reshape(n, sp, h, dh)[:, :s]


def moba_sample(q, k_new, v_new, cache_k, cache_v, page_ksum, page_table, layer):
    db, ds, h, dh = q.shape
    n_pages = page_table.shape[1]
    past = n_pages * PAGE_SIZE
    total = past + ds
    nbt = -(-total // MOBA_BLOCK)
    seq_ksum = page_ksum[layer][page_table]
    eye = np.eye(nbt, dtype=np.float32)
    page_to_blk = eye[(np.arange(n_pages) * PAGE_SIZE) // MOBA_BLOCK]
    new_to_blk = eye[(past + np.arange(ds)) // MOBA_BLOCK]
    blk_sum = (jnp.einsum('bphd,pn->bnhd', seq_ksum, page_to_blk)
               + jnp.einsum('bshd,sn->bnhd', k_new.astype(F32), new_to_blk))
    k_mean = blk_sum / MOBA_BLOCK
    q_pos = past + jnp.arange(ds)
    q_blk = q_pos // MOBA_BLOCK
    gate = jnp.einsum('bshd,bnhd->bshn', q.astype(F32), k_mean)
    is_past = jnp.arange(nbt)[None, :] < q_blk[:, None]
    gate = jnp.where(is_past[None, :, None, :], gate, -jnp.inf)
    n_sel = min(MOBA_TOP_K, nbt)
    _, sel = lax.top_k(gate, n_sel)
    own = jnp.broadcast_to(q_blk[None, :, None, None], (db, ds, h, 1))
    blocks = jnp.concatenate([sel, own], axis=-1)
    blk_ok = jnp.concatenate([sel < q_blk[None, :, None, None], jnp.ones((db, ds, h, 1), bool)], axis=-1)
    pos = blocks[..., None] * MOBA_BLOCK + jnp.arange(MOBA_BLOCK)
    in_cache = pos < past
    b_idx = jnp.arange(db)[:, None, None, None, None]
    h_idx = jnp.arange(h)[None, None, :, None, None]
    phys = page_table[b_idx, jnp.clip(pos // PAGE_SIZE, 0, n_pages - 1)]
    off = pos % PAGE_SIZE
    new_i = jnp.clip(pos - past, 0, ds - 1)
    kg = jnp.where(in_cache[..., None], cache_k[layer, phys, off, h_idx], k_new[b_idx, new_i, h_idx])
    vg = jnp.where(in_cache[..., None], cache_v[layer, phys, off, h_idx], v_new[b_idx, new_i, h_idx])
    scores = jnp.einsum('bshd,bshnrd->bshnr', q, kg, preferred_element_type=F32)
    ok = blk_ok[..., None] & (pos <= q_pos[None, :, None, None, None])
    scores = jnp.where(ok, scores, -jnp.inf).reshape(db, ds, h, (n_sel + 1) * MOBA_BLOCK)
    prob = jax.nn.softmax(scores, axis=-1).astype(vg.dtype)
    return jnp.einsum('bshm,bshmd->bshd', prob, vg.reshape(db, ds, h, (n_sel + 1) * MOBA_BLOCK, dh))


def rwkv_recurrence(state0, r, w, k, v, kk, a):
    def step(st, inp):
        r_t, w_t, k_t, v_t, kk_t, a_t = inp
        sa = jnp.einsum('nhvk,nhk->nhv', st, -kk_t)
        st = (st * w_t[:, :, None, :] + sa[..., None] * (kk_t * a_t)[:, :, None, :]
              + v_t[..., None] * k_t[:, :, None, :])
        return st, jnp.einsum('nhvk,nhk->nhv', st, r_t)
    xs = tuple(jnp.moveaxis(t, 1, 0) for t in (r, w, k, v, kk, a))
    st, y = lax.scan(step, state0, xs)
    return jnp.moveaxis(y, 0, 1), st


def rwkv_branch(mixed, state0, p):
    n, s, _ = mixed.shape
    m = mixed.astype(F32)
    r = m[..., :RWKV_W]
    k = m[..., RWKV_W:2 * RWKV_W]
    v = m[..., 2 * RWKV_W:3 * RWKV_W]
    o = 3 * RWKV_W
    wd = m[..., o:o + DECAY_RANK]
    ad = m[..., o + DECAY_RANK:o + DECAY_RANK + ICLR_RANK]
    gd = m[..., o + DECAY_RANK + ICLR_RANK:]
    w_log = -jax.nn.softplus(-(p['w0'] + jnp.tanh(wd) @ p['w_decay_up'])) - 0.5
    decay = jnp.exp(-jnp.exp(w_log.astype(F32)))
    a = jax.nn.sigmoid(p['a0'] + ad @ p['w_iclr_up']).astype(F32)
    g = (jax.nn.sigmoid(gd) @ p['w_gate_up']).astype(F32)

    def heads(t):
        return t.reshape(n, s, N_RWKV_HEADS, RWKV_HEAD_DIM)

    kk = heads((k * p['k_k']).astype(F32))
    kk = kk / jnp.maximum(jnp.sqrt(jnp.sum(kk * kk, axis=-1, keepdims=True)), L2_EPS)
    k = (k * (1.0 + (a - 1.0) * p['k_a'])).astype(F32)
    rh, kh, vh = heads(r), heads(k), heads(v)
    y, st = rwkv_recurrence(state0.astype(F32), rh, heads(decay), kh, vh, kk, heads(a))
    mu = jnp.mean(y, axis=-1, keepdims=True)
    var = jnp.mean(jnp.square(y - mu), axis=-1, keepdims=True)
    yn = ((y - mu) * lax.rsqrt(var + GN_EPS)).reshape(n, s, RWKV_W) * p['ln_x_w'] + p['ln_x_b']
    bonus = (jnp.sum(rh * kh * p['r_k'].astype(F32), axis=-1, keepdims=True) * vh).reshape(n, s, RWKV_W)
    return (yn + bonus) * g, st


def hybrid_layer(x, pos, attend, shift_prev, wkv0, p):
    n, s, _ = x.shape
    xn = rms_norm(x, p['g_mix'])
    proj = xn @ p['w_in']
    q = proj[..., :ATT_W].reshape(n, s, N_ATT_HEADS, ATT_HEAD_DIM)
    k = proj[..., ATT_W:2 * ATT_W].reshape(n, s, N_ATT_HEADS, ATT_HEAD_DIM)
    v = proj[..., 2 * ATT_W:3 * ATT_W].reshape(n, s, N_ATT_HEADS, ATT_HEAD_DIM)
    q = partial_rotary(q, pos) * (ATT_HEAD_DIM ** -0.5)
    k = partial_rotary(k, pos)
    attn = attend(q, k, v).reshape(n, s, ATT_W)
    o = 3 * ATT_W
    sh = proj[..., o:o + SHIFT_W]
    prev = jnp.concatenate([shift_prev[:, None, :].astype(sh.dtype), sh[:, :-1]], axis=1)
    mixed = sh + p['mu_shift'] * (prev - sh)
    rw, wkv_new = rwkv_branch(mixed, wkv0, p)
    gates = jax.nn.sigmoid((proj[..., o + SHIFT_W:] + p['b_gate']).astype(F32))
    ya = attn @ p['w_proj_attn']
    yb = rw.astype(x.dtype) @ p['w_proj_rwkv']
    merged = (gates[..., :D_MODEL] * ya + gates[..., D_MODEL:] * yb).astype(x.dtype)
    hid = x + merged @ p['w_out']
    hn = rms_norm(hid, p['g_ffn'])
    hid = hid + (jax.nn.silu(hn @ p['w_ffn_gate']) * (hn @ p['w_ffn_up'])) @ p['w_ffn_down']
    return hid, k, v, wkv_new, sh[:, -1]


def setup_inputs(seed: int = 0) -> dict:
    key = jax.random.key(seed)
    ks = jax.random.split(key, 32)
    L = DEPTH
    n_pages = PAST_LEN // PAGE_SIZE
    n_used = DEC_BATCH * n_pages
    n_pool = n_used + max(1, n_used // 4)

    def nrm(k, shape, scale):
        return jax.random.normal(k, shape, F32) * scale

    page_table = jax.random.permutation(ks[6], n_pool)[:n_used].reshape(DEC_BATCH, n_pages).astype(jnp.int32)
    return {
        'x_prompt': nrm(ks[0], (BATCH, SEQ, D_MODEL), 1.0),
        'x_sample': nrm(ks[1], (DEC_BATCH, DEC_SEQ, D_MODEL), 1.0),
        'cache_k': nrm(ks[2], (L, n_pool, PAGE_SIZE, N_ATT_HEADS, ATT_HEAD_DIM), 1.0),
        'cache_v': nrm(ks[3], (L, n_pool, PAGE_SIZE, N_ATT_HEADS, ATT_HEAD_DIM), 1.0),
        'state_wkv': nrm(ks[4], (L, DEC_BATCH, N_RWKV_HEADS, RWKV_HEAD_DIM, RWKV_HEAD_DIM), 0.5),
        'state_shift': nrm(ks[5], (L, DEC_BATCH, SHIFT_W), 1.0),
        'page_table': page_table,
        'g_mix': 1.0 + nrm(ks[7], (L, D_MODEL), 0.02),
        'w_in': nrm(ks[8], (L, D_MODEL, IN_W), D_MODEL ** -0.5),
        'b_gate': nrm(ks[9], (L, 2 * D_MODEL), 0.01),
        'mu_shift': jax.random.uniform(ks[10], (L, SHIFT_W), F32),
        'w0': nrm(ks[11], (L, RWKV_W), 0.5) - 0.5,
        'w_decay_up': nrm(ks[12], (L, DECAY_RANK, RWKV_W), 0.1),
        'a0': nrm(ks[13], (L, RWKV_W), 0.1),
        'w_iclr_up': nrm(ks[14], (L, ICLR_RANK, RWKV_W), 0.1),
        'w_gate_up': nrm(ks[15], (L, GATE_RANK, RWKV_W), GATE_RANK ** -0.5),
        'k_k': 0.85 + nrm(ks[16], (L, RWKV_W), 0.05),
        'k_a': 1.0 + nrm(ks[17], (L, RWKV_W), 0.05),
        'r_k': nrm(ks[18], (L, N_RWKV_HEADS, RWKV_HEAD_DIM), 0.1),
        'ln_x_w': 1.0 + nrm(ks[19], (L, RWKV_W), 0.02),
        'ln_x_b': nrm(ks[20], (L, RWKV_W), 0.01),
        'w_proj_attn': nrm(ks[21], (L, ATT_W, D_MODEL), ATT_W ** -0.5),
        'w_proj_rwkv': nrm(ks[22], (L, RWKV_W, D_MODEL), RWKV_W ** -0.5),
        'w_out': nrm(ks[23], (L, D_MODEL, D_MODEL), D_MODEL ** -0.5),
        'g_ffn': 1.0 + nrm(ks[24], (L, D_MODEL), 0.02),
        'w_ffn_gate': nrm(ks[25], (L, D_MODEL, D_FF), D_MODEL ** -0.5),
        'w_ffn_up': nrm(ks[26], (L, D_MODEL, D_FF), D_MODEL ** -0.5),
        'w_ffn_down': nrm(ks[27], (L, D_FF, D_MODEL), D_FF ** -0.5),
        'g_final': 1.0 + nrm(ks[28], (D_MODEL,), 0.02),
    }


def reference(x_prompt, x_sample, cache_k, cache_v, state_wkv, state_shift, page_table,
              g_mix, w_in, b_gate, mu_shift, w0, w_decay_up, a0, w_iclr_up, w_gate_up,
              k_k, k_a, r_k, ln_x_w, ln_x_b, w_proj_attn, w_proj_rwkv, w_out,
              g_ffn, w_ffn_gate, w_ffn_up, w_ffn_down, g_final):
    page_ksum = jnp.sum(cache_k, axis=2, dtype=F32)
    past = page_table.shape[1] * PAGE_SIZE
    pos_p = jnp.arange(x_prompt.shape[1])
    pos_s = past + jnp.arange(x_sample.shape[1])
    n_p = x_prompt.shape[0]
    hp, hs = x_prompt, x_sample
    kps, vps, wps, sps, kss, vss, wss, sss = [], [], [], [], [], [], [], []
    for l in range(DEPTH):
        p = {'g_mix': g_mix[l], 'w_in': w_in[l], 'b_gate': b_gate[l], 'mu_shift': mu_shift[l],
             'w0': w0[l], 'w_decay_up': w_decay_up[l], 'a0': a0[l], 'w_iclr_up': w_iclr_up[l],
             'w_gate_up': w_gate_up[l], 'k_k': k_k[l], 'k_a': k_a[l], 'r_k': r_k[l],
             'ln_x_w': ln_x_w[l], 'ln_x_b': ln_x_b[l], 'w_proj_attn': w_proj_attn[l],
             'w_proj_rwkv': w_proj_rwkv[l], 'w_out': w_out[l], 'g_ffn': g_ffn[l],
             'w_ffn_gate': w_ffn_gate[l], 'w_ffn_up': w_ffn_up[l], 'w_ffn_down': w_ffn_down[l]}
        hp, kp, vp, wp, sp = hybrid_layer(
            hp, pos_p, moba_prompt,
            jnp.zeros((n_p, SHIFT_W), hp.dtype),
            jnp.zeros((n_p, N_RWKV_HEADS, RWKV_HEAD_DIM, RWKV_HEAD_DIM), F32), p)
        hs, ks_, vs_, ws_, ss_ = hybrid_layer(
            hs, pos_s,
            lambda q, k, v, l=l: moba_sample(q, k, v, cache_k, cache_v, page_ksum, page_table, l),
            state_shift[l], state_wkv[l], p)
        kps.append(kp)
        vps.append(vp)
        wps.append(wp.astype(state_wkv.dtype))
        sps.append(sp.astype(state_shift.dtype))
        kss.append(ks_)
        vss.append(vs_)
        wss.append(ws_.astype(state_wkv.dtype))
        sss.append(ss_.astype(state_shift.dtype))
    y_prompt = rms_norm(hp, g_final)
    y_sample = rms_norm(hs, g_final)
    return (y_prompt, y_sample, jnp.stack(kps), jnp.stack(vps), jnp.stack(wps), jnp.stack(sps),
            jnp.stack(kss), jnp.stack(vss), jnp.stack(wss), jnp.stack(sss))
```

```python
import functools
import math

import numpy as np
import jax
import jax.numpy as jnp
from jax import lax
from jax.experimental import pallas as pl
from jax.experimental.pallas import tpu as pltpu

F32 = jnp.float32
BF16 = jnp.bfloat16

ATT_HEAD_DIM = 128
ROT_DIM = ATT_HEAD_DIM // 4
ROPE_THETA = 500000.0
MOBA_BLOCK = 256
MOBA_TOP_K = 3
PAGE_SIZE = 128
RWKV_HEAD_DIM = 64
DECAY_RANK = 64
ICLR_RANK = 64
GATE_RANK = 160
RMS_EPS = 1e-6
GN_EPS = 64e-5
L2_EPS = 1e-12

LANES = 128
LOW_W = DECAY_RANK + ICLR_RANK + GATE_RANK
LOW_PAD = -(-LOW_W // LANES) * LANES
WKV_CHUNK = 64
VMEM_LIMIT = 56 * 1024 * 1024

NN = (((1,), (0,)), ((), ()))
NT = (((1,), (1,)), ((), ()))
TN = (((0,), (0,)), ((), ()))


def _cparams(sem):
    return pltpu.CompilerParams(dimension_semantics=sem, vmem_limit_bytes=VMEM_LIMIT)


def _dg(a, b, dims=NN):
    return lax.dot_general(a, b, dims, preferred_element_type=F32)


def _split2(x):
    hi = x.astype(BF16)
    lo = (x - hi.astype(F32)).astype(BF16)
    return hi, lo


def _split3(x):
    hi = x.astype(BF16)
    r1 = x - hi.astype(F32)
    mid = r1.astype(BF16)
    lo = (r1 - mid.astype(F32)).astype(BF16)
    return hi, mid, lo


def _dot3(a, b, dims=NN):
    ah, al = _split2(a)
    bh, bl = _split2(b)
    return _dg(ah, bh, dims) + (_dg(ah, bl, dims) + _dg(al, bh, dims))


def _dot3_pre(a, bh, bl, dims=NN):
    ah, al = _split2(a)
    return _dg(ah, bh, dims) + (_dg(ah, bl, dims) + _dg(al, bh, dims))


def _dot_exact_rhs(a, b_bf16, dims=NN):
    hi, mid, lo = _split3(a)
    return _dg(hi, b_bf16, dims) + (_dg(mid, b_bf16, dims) + _dg(lo, b_bf16, dims))


def _dot_exact_lhs(a_bf16, b, dims=NN):
    hi, mid, lo = _split3(b)
    return _dg(a_bf16, hi, dims) + (_dg(a_bf16, mid, dims) + _dg(a_bf16, lo, dims))


def _bdot3(pattern, a, b):
    ah, al = _split2(a)
    bh, bl = _split2(b)
    e = functools.partial(jnp.einsum, pattern, preferred_element_type=F32)
    return e(ah, bh) + (e(ah, bl) + e(al, bh))


def _rms(x, g):
    return x * lax.rsqrt(jnp.mean(x * x, axis=-1, keepdims=True) + RMS_EPS) * g


def _norm_mm_body(*refs, epilogue, n_aux):
    x_ref, g_ref, w_ref = refs[:3]
    aux = refs[3:3 + n_aux]
    o_ref, xn_ref = refs[3 + n_aux:]
    j = pl.program_id(1)

    @pl.when(j == 0)
    def _():
        xn_ref[...] = _rms(x_ref[...], g_ref[...]).astype(BF16)

    acc = _dg(xn_ref[...], w_ref[...])
    o_ref[...] = epilogue(acc, j, *aux).astype(o_ref.dtype)


def _epi_none(acc, j):
    return acc


def _epi_rope(acc, j, cos_ref, sa_ref, sb_ref):
    c, sa, sb = cos_ref[...], sa_ref[...], sb_ref[...]
    half = ROT_DIM // 2
    outs = []
    for h in range(acc.shape[1] // ATT_HEAD_DIM):
        xh = acc[:, h * ATT_HEAD_DIM:(h + 1) * ATT_HEAD_DIM]
        outs.append(xh * c + pltpu.roll(xh, ATT_HEAD_DIM - half, 1) * sa + pltpu.roll(xh, half, 1) * sb)
    rot = jnp.concatenate(outs, axis=1)
    rot = rot * jnp.where(j == 0, ATT_HEAD_DIM ** -0.5, 1.0).astype(F32)
    return jnp.where(j < 2, rot, acc)


def _epi_sigmoid_bias(acc, j, b_ref):
    return jax.nn.sigmoid(acc + b_ref[...])


def _norm_matmul(x, g, w, tm, tn, epilogue=_epi_none, aux=(), aux_specs=(), out_dtype=F32):
    m, d = x.shape
    n = w.shape[1]
    body = functools.partial(_norm_mm_body, epilogue=epilogue, n_aux=len(aux))
    return pl.pallas_call(
        body,
        grid=(m // tm, n // tn),
        in_specs=[pl.BlockSpec((tm, d), lambda i, j: (i, 0)),
                  pl.BlockSpec((1, d), lambda i, j: (0, 0)),
                  pl.BlockSpec((d, tn), lambda i, j: (0, j)), *aux_specs],
        out_specs=pl.BlockSpec((tm, tn), lambda i, j: (i, j)),
        out_shape=jax.ShapeDtypeStruct((m, n), out_dtype),
        scratch_shapes=[pltpu.VMEM((tm, d), BF16)],
        compiler_params=_cparams(("parallel", "arbitrary")),
    )(x, g, w, *aux)


def _rope_tables(pos):
    half = ROT_DIM // 2
    inv = jnp.exp(jnp.arange(half, dtype=F32) * (-2.0 * math.log(ROPE_THETA) / ROT_DIM))
    ang = pos.astype(F32)[:, None] * inv[None, :]
    cos, sin = jnp.cos(ang), jnp.sin(ang)
    s = pos.shape[0]
    z = lambda w: jnp.zeros((s, w), F32)
    cos_t = jnp.concatenate([cos, cos, jnp.ones((s, ATT_HEAD_DIM - ROT_DIM), F32)], axis=1)
    sin_a = jnp.concatenate([-sin, z(ATT_HEAD_DIM - half)], axis=1)
    sin_b = jnp.concatenate([z(half), sin, z(ATT_HEAD_DIM - ROT_DIM)], axis=1)
    return cos_t, sin_a, sin_b


def _moba_prompt_body(q_ref, k_ref, v_ref, o_ref, km_ref, *, nb):
    qb = pl.program_id(2)
    blk = MOBA_BLOCK

    @pl.when(qb == 0)
    def _():
        km_ref[...] = jnp.zeros_like(km_ref)
        for j in range(nb):
            km_ref[j:j + 1, :] = jnp.mean(k_ref[j * blk:(j + 1) * blk, :], axis=0, keepdims=True)

    q = q_ref[...]
    gate = _dot3(q, km_ref[...], NT)
    col = lax.broadcasted_iota(jnp.int32, gate.shape, 1)
    valid = col < qb
    gm = jnp.where(valid, gate, -jnp.inf)
    cnt = jnp.zeros(gate.shape, F32)
    for i in range(nb):
        gi = gm[:, i:i + 1]
        cnt = cnt + jnp.where(gi > gm, 1.0, 0.0) + jnp.where(gi == gm, (col > i).astype(F32), 0.0)
    sel = jnp.where(valid, jnp.where(cnt < MOBA_TOP_K, 1.0, 0.0), 0.0)

    qh = q.astype(BF16)
    row = lax.broadcasted_iota(jnp.int32, (blk, blk), 0)
    colk = lax.broadcasted_iota(jnp.int32, (blk, blk), 1)
    start = pl.multiple_of(qb * blk, blk)
    s = _dg(qh, k_ref[pl.ds(start, blk), :].astype(BF16), NT)
    s = jnp.where(colk <= row, s, -jnp.inf)
    m = jnp.max(s, axis=1, keepdims=True)
    p = jnp.exp(s - m)
    l = jnp.sum(p, axis=1, keepdims=True)
    acc = _dg(p.astype(BF16), v_ref[pl.ds(start, blk), :].astype(BF16))

    def step(j, carry):
        m, l, acc = carry
        st = pl.multiple_of(j * blk, blk)
        sj = _dg(qh, k_ref[pl.ds(st, blk), :].astype(BF16), NT)
        on = jnp.max(jnp.where(col == j, sel, 0.0), axis=1, keepdims=True)
        sj = jnp.where(on > 0.0, sj, -jnp.inf)
        m_new = jnp.maximum(m, jnp.max(sj, axis=1, keepdims=True))
        alpha = jnp.exp(m - m_new)
        pj = jnp.exp(sj - m_new)
        l = alpha * l + jnp.sum(pj, axis=1, keepdims=True)
        acc = alpha * acc + _dg(pj.astype(BF16), v_ref[pl.ds(st, blk), :].astype(BF16))
        return m_new, l, acc

    m, l, acc = lax.fori_loop(0, qb, step, (m, l, acc))
    o_ref[...] = (acc / l).astype(o_ref.dtype)


def _moba_prompt(qkv, n, s):
    att_w = qkv.shape[1] // 3
    nh = att_w // ATT_HEAD_DIM
    nb = s // MOBA_BLOCK
    nbq = s // MOBA_BLOCK
    return pl.pallas_call(
        functools.partial(_moba_prompt_body, nb=nb),
        grid=(n, nh, nbq),
        in_specs=[pl.BlockSpec((MOBA_BLOCK, ATT_HEAD_DIM), lambda b, h, i: (b * nbq + i, h)),
                  pl.BlockSpec((s, ATT_HEAD_DIM), lambda b, h, i: (b, nh + h)),
                  pl.BlockSpec((s, ATT_HEAD_DIM), lambda b, h, i: (b, 2 * nh + h))],
        out_specs=pl.BlockSpec((MOBA_BLOCK, ATT_HEAD_DIM), lambda b, h, i: (b * nbq + i, h)),
        out_shape=jax.ShapeDtypeStruct((n * s, att_w), BF16),
        scratch_shapes=[pltpu.VMEM((max(nb, 8), ATT_HEAD_DIM), F32)],
        compiler_params=_cparams(("parallel", "parallel", "arbitrary")),
    )(qkv, qkv, qkv)


def _page_blocksum_body(pt_ref, ck_ref, o_ref, *, ppb):
    p = pl.program_id(1)
    part = jnp.sum(ck_ref[...], axis=0, keepdims=True)

    @pl.when(p % ppb == 0)
    def _():
        o_ref[...] = part

    @pl.when(p % ppb != 0)
    def _():
        o_ref[...] += part


def _page_blocksum(cache_k3, page_table):
    db, n_pages = page_table.shape
    att_w = cache_k3.shape[2]
    ppb = MOBA_BLOCK // PAGE_SIZE
    gs = pltpu.PrefetchScalarGridSpec(
        num_scalar_prefetch=1, grid=(db, n_pages),
        in_specs=[pl.BlockSpec((None, PAGE_SIZE, att_w), lambda b, p, pt: (pt[b, p], 0, 0))],
        out_specs=pl.BlockSpec((None, None, 1, att_w), lambda b, p, pt: (b, p // ppb, 0, 0)))
    return pl.pallas_call(
        functools.partial(_page_blocksum_body, ppb=ppb), grid_spec=gs,
        out_shape=jax.ShapeDtypeStruct((db, n_pages // ppb, 1, att_w), F32),
        compiler_params=_cparams(("parallel", "arbitrary")),
    )(page_table, cache_k3)


def _decode_select_body(bs_ref, q_ref, sel_ref, *, nbp, q_blk, nh):
    bs = bs_ref[...]
    q = q_ref[...]
    it = lax.broadcasted_iota(jnp.int32, (nbp, 1), 0)
    for h in range(nh):
        sl = slice(h * ATT_HEAD_DIM, (h + 1) * ATT_HEAD_DIM)
        g = jnp.sum(q[:, sl] * (bs[:, sl] / MOBA_BLOCK), axis=1, keepdims=True)
        g = jnp.where(it < q_blk, g, -jnp.inf)
        for t in range(MOBA_TOP_K):
            mx = jnp.max(g, axis=0, keepdims=True)
            idx = jnp.min(jnp.where(g == mx, it, nbp), axis=0, keepdims=True)
            sel_ref[t:t + 1, h:h + 1] = jnp.where(mx > -jnp.inf, idx, -1)
            g = jnp.where(it == idx, -jnp.inf, g)


def _decode_select(blk_sum, q, q_blk):
    db, nbp, att_w = blk_sum.shape
    nh = att_w // ATT_HEAD_DIM
    return pl.pallas_call(
        functools.partial(_decode_select_body, nbp=nbp, q_blk=q_blk, nh=nh),
        grid=(db,),
        in_specs=[pl.BlockSpec((None, nbp, att_w), lambda b: (b, 0, 0)),
                  pl.BlockSpec((None, 1, att_w), lambda b: (b, 0, 0))],
        out_specs=pl.BlockSpec((None, MOBA_TOP_K, nh), lambda b: (b, 0, 0)),
        out_shape=jax.ShapeDtypeStruct((db, MOBA_TOP_K, nh), jnp.int32),
        compiler_params=_cparams(("parallel",)),
    )(blk_sum, q.reshape(db, 1, att_w))


def _decode_attn_body(pt_ref, sel_ref, q_ref, kn_ref, vn_ref, ck_ref, cv_ref, o_ref, m_ref, l_ref, acc_ref,
                      *, ppb, nsteps, nh):
    b, h, t = pl.program_id(0), pl.program_id(1), pl.program_id(2)
    q = q_ref[...]

    @pl.when(t == 0)
    def _():
        m_ref[...] = jnp.sum(q * kn_ref[...], axis=1, keepdims=True)
        l_ref[...] = jnp.ones_like(l_ref)
        acc_ref[...] = vn_ref[...]

    ok = sel_ref[(b * MOBA_TOP_K + t // ppb) * nh + h] >= 0
    s = _dg(q.astype(BF16), ck_ref[...].astype(BF16), NT)
    s = jnp.where(ok, s, -jnp.inf)
    m = m_ref[...]
    m_new = jnp.maximum(m, jnp.max(s, axis=1, keepdims=True))
    alpha = jnp.exp(m - m_new)
    p = jnp.exp(s - m_new)
    l_ref[...] = alpha * l_ref[...] + jnp.sum(p, axis=1, keepdims=True)
    acc_ref[...] = alpha * acc_ref[...] + _dg(p.astype(BF16), cv_ref[...].astype(BF16))
    m_ref[...] = m_new

    @pl.when(t == nsteps - 1)
    def _():
        o_ref[...] = (acc_ref[...] / l_ref[...]).astype(o_ref.dtype)


def _decode_attn(page_table, sel, q, k_new, v_new, cache_k3, cache_v3):
    db, att_w = q.shape
    nh = att_w // ATT_HEAD_DIM
    ppb = MOBA_BLOCK // PAGE_SIZE
    nsteps = MOBA_TOP_K * ppb

    def page_map(b, h, t, pt, sl):
        blk = jnp.maximum(sl[(b * MOBA_TOP_K + t // ppb) * nh + h], 0)
        return (pt[b, blk * ppb + t % ppb], 0, h)

    row = pl.BlockSpec((None, 1, ATT_HEAD_DIM), lambda b, h, t, pt, sl: (b, 0, h))
    page = pl.BlockSpec((None, PAGE_SIZE, ATT_HEAD_DIM), page_map)
    gs = pltpu.PrefetchScalarGridSpec(
        num_scalar_prefetch=2, grid=(db, nh, nsteps),
        in_specs=[row, row, row, page, page],
        out_specs=row,
        scratch_shapes=[pltpu.VMEM((1, 1), F32), pltpu.VMEM((1, 1), F32), pltpu.VMEM((1, ATT_HEAD_DIM), F32)])
    r3 = lambda a: a.reshape(db, 1, att_w)
    out = pl.pallas_call(
        functools.partial(_decode_attn_body, ppb=ppb, nsteps=nsteps, nh=nh), grid_spec=gs,
        out_shape=jax.ShapeDtypeStruct((db, 1, att_w), BF16),
        compiler_params=_cparams(("parallel", "parallel", "arbitrary")),
    )(page_table, sel.reshape(-1), r3(q), r3(k_new), r3(v_new), cache_k3, cache_v3)
    return out.reshape(db, att_w)


def _seg_sum(x, e_bf16):
    outs = [_dot_exact_rhs(x[:, c:c + LANES], e_bf16) for c in range(0, x.shape[1], LANES)]
    return outs[0] if len(outs) == 1 else jnp.concatenate(outs, axis=1)


def _rwkv_features(rkv, low, prev_rkv, prev_low, par):
    (mu_rkv, mu_low, w0, a0, k_k, k_a, r_k, wd_h, wd_l, wi_h, wi_l, wg_h, wg_l, e128) = par
    w = w0.shape[1]
    mix = rkv + mu_rkv * (prev_rkv - rkv)
    mlow = low + mu_low * (prev_low - low)
    r, k, v = mix[:, :w], mix[:, w:2 * w], mix[:, 2 * w:]
    lane = lax.broadcasted_iota(jnp.int32, mlow.shape, 1)
    feat = jnp.where(lane < DECAY_RANK, jnp.tanh(mlow),
                     jnp.where(lane < DECAY_RANK + ICLR_RANK, mlow,
                               jnp.where(lane < LOW_W, jax.nn.sigmoid(mlow), 0.0)))
    z = w0 + _dot3_pre(feat, wd_h, wd_l)
    nz = -z
    w_log = -(jnp.maximum(nz, 0.0) + jnp.log1p(jnp.exp(-jnp.abs(nz)))) - 0.5
    logw = -jnp.exp(w_log)
    a = jax.nn.sigmoid(a0 + _dot3_pre(feat, wi_h, wi_l))
    g = _dot3_pre(feat, wg_h, wg_l)
    kk = k * k_k
    kk = kk / jnp.maximum(jnp.sqrt(_seg_sum(kk * kk, e128)), L2_EPS)
    k2 = k * (1.0 + (a - 1.0) * k_a)
    bonus = _seg_sum(r * k2 * r_k, e128) * v
    return r, k2, v, kk, a, logw, g, bonus


_N_PAR = 14


def _rwkv_prep_body(*refs, tm, chunk, tiles_per_seq):
    rkv_ref, low_ref, prkv_ref, plow_ref, frkv_ref, flow_ref, ltri_ref = refs[:7]
    par = tuple(r[...] for r in refs[7:7 + _N_PAR])
    at_ref, bt_ref, kt_ref, rt_ref, v_ref, g_ref, bon_ref, pt_ref = refs[7 + _N_PAR:]
    i = pl.program_id(0)
    rkv, low = rkv_ref[...], low_ref[...]
    seq_start = i % tiles_per_seq == 0

    def prev_of(x, tail_ref, first_ref):
        first = jnp.where(seq_start, first_ref[...], tail_ref[7:8, :])
        rowi = lax.broadcasted_iota(jnp.int32, x.shape, 0)
        return jnp.where(rowi == 0, first, pltpu.roll(x, 1, 0))

    prev_rkv = prev_of(rkv, prkv_ref, frkv_ref)
    prev_low = prev_of(low, plow_ref, flow_ref)
    r, k2, v, kk, a, logw, g, bonus = _rwkv_features(rkv, low, prev_rkv, prev_low, par)
    cum = _dot_exact_lhs(ltri_ref[...], logw)
    e_in = jnp.exp(cum)
    e_out = jnp.exp(-cum)
    at_ref[...] = -kk * jnp.exp(cum - logw)
    bt_ref[...] = kk * a * e_out
    kt_ref[...] = k2 * e_out
    rt_ref[...] = r * e_in
    v_ref[...] = v
    g_ref[...] = g
    bon_ref[...] = bonus
    for c in range(tm // chunk):
        pt_ref[c] = e_in[(c + 1) * chunk - 1:(c + 1) * chunk, :]


def _wkv_chunk_body(at_ref, bt_ref, kt_ref, rt_ref, v_ref, pt_ref, g_ref, bon_ref, lnw_ref, lnb_ref, e_ref,
                    rw_ref, st_ref, s_scr, *, nc, chunk):
    T = chunk
    blk = pl.program_id(2)

    @pl.when(blk == 0)
    def _():
        s_scr[...] = jnp.zeros_like(s_scr)

    c3 = lambda ref: ref[...].reshape(nc, T, LANES)
    At, Bt, Kt, Rt, V = c3(at_ref), c3(bt_ref), c3(kt_ref), c3(rt_ref), c3(v_ref)
    lane = lax.broadcasted_iota(jnp.int32, (1, 1, LANES), 2)
    ti = lax.broadcasted_iota(jnp.int32, (1, T, T), 1)
    tj = lax.broadcasted_iota(jnp.int32, (1, T, T), 2)
    strict, incl = tj < ti, tj <= ti
    eye = jnp.where(ti == tj, 1.0, 0.0).astype(F32)
    bi = lax.broadcasted_iota(jnp.int32, (LANES, LANES), 0) // RWKV_HEAD_DIM
    bj = lax.broadcasted_iota(jnp.int32, (LANES, LANES), 1) // RWKV_HEAD_DIM
    same_head = bi == bj

    ahat = jnp.zeros((nc, T, LANES), F32)
    uhat = jnp.zeros((nc, T, LANES), F32)
    g_rb, g_rk = [], []
    for h in range(LANES // RWKV_HEAD_DIM):
        mh = (lane // RWKV_HEAD_DIM) == h
        Ah = jnp.where(mh, At, 0.0)
        Rh = jnp.where(mh, Rt, 0.0)
        gab = jnp.where(strict, _bdot3('bid,bjd->bij', Ah, Bt), 0.0)
        gak = jnp.where(strict, _bdot3('bid,bjd->bij', Ah, Kt), 0.0)
        g_rb.append(jnp.where(incl, _bdot3('bid,bjd->bij', Rh, Bt), 0.0))
        g_rk.append(jnp.where(incl, _bdot3('bid,bjd->bij', Rh, Kt), 0.0))
        x = eye + gab
        lp = gab
        span = 2
        while span < T:
            lp = _bdot3('bij,bjk->bik', lp, lp)
            x = _bdot3('bij,bjk->bik', x, eye + lp)
            span *= 2
        ahat = ahat + _bdot3('bij,bjd->bid', x, Ah)
        uhat = uhat + jnp.where(mh, _bdot3('bij,bjd->bid', x, _bdot3('bij,bjd->bid', gak, V)), 0.0)
    kv = jnp.where(same_head[None], _bdot3('btv,btk->bvk', V, Kt), 0.0)

    s = s_scr[...]
    ys = []
    for c in range(nc):
        u = _dot3(ahat[c], s, NT) + uhat[c]
        y = _dot3(Rt[c], s, NT)
        for h in range(LANES // RWKV_HEAD_DIM):
            mh2 = (lane[0] // RWKV_HEAD_DIM) == h
            y = y + jnp.where(mh2, _dot3(g_rb[h][c], u) + _dot3(g_rk[h][c], V[c]), 0.0)
        ys.append(y)
        s = (s + jnp.where(same_head, _dot3(u, Bt[c], TN), 0.0) + kv[c]) * pt_ref[c]
    s_scr[...] = s
    st_ref[...] = s

    y = jnp.concatenate(ys, axis=0)
    e128 = e_ref[...]
    mu = _seg_sum(y, e128) / RWKV_HEAD_DIM
    yc = y - mu
    var = _seg_sum(yc * yc, e128) / RWKV_HEAD_DIM
    yn = yc * lax.rsqrt(var + GN_EPS) * lnw_ref[...] + lnb_ref[...]
    rw_ref[...] = ((yn + bon_ref[...]) * g_ref[...]).astype(rw_ref.dtype)


def _wkv_step_body(r_ref, k_ref, v_ref, kk_ref, a_ref, lw_ref, g_ref, bon_ref, lnw_ref, lnb_ref, s0_ref,
                   rw_ref, s1_ref):
    s0 = s0_ref[...]
    d = s0.shape[0]
    r, k, v, kk, a = r_ref[...], k_ref[...], v_ref[...], kk_ref[...], a_ref[...]
    eye = lax.broadcasted_iota(jnp.int32, (d, d), 0) == lax.broadcasted_iota(jnp.int32, (d, d), 1)
    v_col = jnp.sum(jnp.where(eye, v, 0.0), axis=1, keepdims=True)
    sa = jnp.sum(s0 * (-kk), axis=1, keepdims=True)
    s1 = s0 * jnp.exp(lw_ref[...]) + sa * (kk * a) + v_col * k
    s1_ref[...] = s1
    y_col = jnp.sum(s1 * r, axis=1, keepdims=True)
    y = jnp.sum(jnp.where(eye, y_col, 0.0), axis=0, keepdims=True)
    mu = jnp.mean(y, axis=1, keepdims=True)
    var = jnp.mean(jnp.square(y - mu), axis=1, keepdims=True)
    yn = (y - mu) * lax.rsqrt(var + GN_EPS) * lnw_ref[...] + lnb_ref[...]
    rw_ref[...] = ((yn + bon_ref[...]) * g_ref[...]).astype(rw_ref.dtype)


def _rwkv_feat_body(*refs):
    rkv_ref, low_ref, prkv_ref, plow_ref = refs[:4]
    par = tuple(r[...] for r in refs[4:4 + _N_PAR])
    outs = refs[4 + _N_PAR:]
    vals = _rwkv_features(rkv_ref[...], low_ref[...], prkv_ref[...], plow_ref[...], par)
    for o, x in zip(outs, vals):
        o[...] = x


def _rwkv_params(p, rwkv_w):
    def pad_up(wt, r0):
        full = jnp.zeros((LOW_PAD, rwkv_w), F32).at[r0:r0 + wt.shape[0]].set(wt)
        hi = full.astype(BF16)
        return hi, (full - hi.astype(F32)).astype(BF16)

    wd_h, wd_l = pad_up(p['w_decay_up'], 0)
    wi_h, wi_l = pad_up(p['w_iclr_up'], DECAY_RANK)
    wg_h, wg_l = pad_up(p['w_gate_up'], DECAY_RANK + ICLR_RANK)
    hd = np.arange(LANES) // RWKV_HEAD_DIM
    e128 = jnp.asarray(hd[:, None] == hd[None, :], BF16)
    row = lambda t: t.reshape(1, -1).astype(F32)
    mu = p['mu_shift']
    mu_rkv = row(mu[:3 * rwkv_w])
    mu_low = row(jnp.pad(mu[3 * rwkv_w:], (0, LOW_PAD - LOW_W)))
    return (mu_rkv, mu_low, row(p['w0']), row(p['a0']), row(p['k_k']), row(p['k_a']), row(p['r_k']),
            wd_h, wd_l, wi_h, wi_l, wg_h, wg_l, e128)


def _full_spec(a):
    nd = a.ndim
    return pl.BlockSpec(a.shape, lambda *_: (0,) * nd)


def _rwkv_prompt(rkv, low, par, ln_w, ln_b, n, s):
    m = n * s
    w = rkv.shape[1] // 3
    tm = min(256, s)
    chunk = WKV_CHUNK
    cpt = tm // chunk
    hd = np.arange(tm) // chunk
    ltri = jnp.asarray((hd[:, None] == hd[None, :]) & (np.arange(tm)[:, None] >= np.arange(tm)[None, :]), BF16)
    zeros_rkv = jnp.zeros((n, 1, 3 * w), F32)
    zeros_low = jnp.zeros((n, 1, LOW_PAD), F32)
    tiles_per_seq = s // tm
    tail = lambda width: pl.BlockSpec((8, width), lambda i: (jnp.maximum(i * (tm // 8) - 1, 0), 0))
    first = lambda width: pl.BlockSpec((None, 1, width), lambda i: (i // tiles_per_seq, 0, 0))
    big = pl.BlockSpec((tm, w), lambda i: (i, 0))
    outs = pl.pallas_call(
        functools.partial(_rwkv_prep_body, tm=tm, chunk=chunk, tiles_per_seq=tiles_per_seq),
        grid=(m // tm,),
        in_specs=[pl.BlockSpec((tm, 3 * w), lambda i: (i, 0)), pl.BlockSpec((tm, LOW_PAD), lambda i: (i, 0)),
                  tail(3 * w), tail(LOW_PAD), first(3 * w), first(LOW_PAD), _full_spec(ltri),
                  *[_full_spec(t) for t in par]],
        out_specs=[big] * 7 + [pl.BlockSpec((cpt, 1, w), lambda i: (i, 0, 0))],
        out_shape=[jax.ShapeDtypeStruct((m, w), F32)] * 7 + [jax.ShapeDtypeStruct((m // chunk, 1, w), F32)],
        compiler_params=_cparams(("parallel",)),
    )(rkv, low, rkv, low, zeros_rkv, zeros_low, ltri, *par)
    at, bt, kt, rt, v, g, bon, ptot = outs

    nc = cpt
    tb = nc * chunk
    npairs = w // LANES
    nblk = s // tb
    tok = pl.BlockSpec((tb, LANES), lambda b, p, i: (b * nblk + i, p))
    vec = pl.BlockSpec((1, LANES), lambda b, p, i: (0, p))
    rw, st = pl.pallas_call(
        functools.partial(_wkv_chunk_body, nc=nc, chunk=chunk),
        grid=(n, npairs, nblk),
        in_specs=[tok, tok, tok, tok, tok,
                  pl.BlockSpec((nc, 1, LANES), lambda b, p, i: (b * nblk + i, 0, p)),
                  tok, tok, vec, vec, _full_spec(par[-1])],
        out_specs=[tok, pl.BlockSpec((None, None, LANES, LANES), lambda b, p, i: (b, p, 0, 0))],
        out_shape=[jax.ShapeDtypeStruct((m, w), BF16), jax.ShapeDtypeStruct((n, npairs, LANES, LANES), F32)],
        scratch_shapes=[pltpu.VMEM((LANES, LANES), F32)],
        compiler_params=_cparams(("parallel", "parallel", "arbitrary")),
    )(at, bt, kt, rt, v, ptot, g, bon, ln_w.reshape(1, w), ln_b.reshape(1, w), par[-1])
    hp = LANES // RWKV_HEAD_DIM
    d = RWKV_HEAD_DIM
    st = jnp.stack([st[:, :, j * d:(j + 1) * d, j * d:(j + 1) * d] for j in range(hp)], axis=2)
    return rw, st.reshape(n, npairs * hp, d, d)


def _rwkv_sample(rkv, low, prev_rkv, prev_low, state0, par, ln_w, ln_b):
    db = rkv.shape[0]
    w = rkv.shape[1] // 3
    d = RWKV_HEAD_DIM
    nh = w // d
    ins = (rkv, low, prev_rkv, prev_low, *par)
    feats = pl.pallas_call(
        _rwkv_feat_body,
        in_specs=[_full_spec(t) for t in ins],
        out_specs=[pl.BlockSpec((db, w), lambda: (0, 0))] * 8,
        out_shape=[jax.ShapeDtypeStruct((db, w), F32)] * 8,
        compiler_params=pltpu.CompilerParams(vmem_limit_bytes=VMEM_LIMIT),
    )(*ins)
    hv = lambda t: t.reshape(db, nh, 1, d)
    row = pl.BlockSpec((None, None, 1, d), lambda b, h: (b, h, 0, 0))
    prow = pl.BlockSpec((None, 1, d), lambda b, h: (h, 0, 0))
    mat = pl.BlockSpec((None, None, d, d), lambda b, h: (b, h, 0, 0))
    rw, s1 = pl.pallas_call(
        _wkv_step_body,
        grid=(db, nh),
        in_specs=[row] * 8 + [prow, prow, mat],
        out_specs=[row, mat],
        out_shape=[jax.ShapeDtypeStruct((db, nh, 1, d), BF16), jax.ShapeDtypeStruct((db, nh, d, d), F32)],
        compiler_params=_cparams(("parallel", "parallel")),
    )(*[hv(t) for t in feats], ln_w.reshape(nh, 1, d), ln_b.reshape(nh, 1, d), state0)
    return rw.reshape(db, w), s1


def _merge_body(attn_ref, rw_ref, ga_ref, gb_ref, wa_ref, wr_ref, o_ref):
    ya = _dg(attn_ref[...], wa_ref[...])
    yb = _dg(rw_ref[...], wr_ref[...])
    o_ref[...] = (ga_ref[...] * ya + gb_ref[...] * yb).astype(o_ref.dtype)


def _merge(attn, rw, gates, wa, wr, tm, tn):
    m, ka = attn.shape
    kr = rw.shape[1]
    d = wa.shape[1]
    nj = d // tn
    return pl.pallas_call(
        _merge_body,
        grid=(m // tm, nj),
        in_specs=[pl.BlockSpec((tm, ka), lambda i, j: (i, 0)), pl.BlockSpec((tm, kr), lambda i, j: (i, 0)),
                  pl.BlockSpec((tm, tn), lambda i, j: (i, j)), pl.BlockSpec((tm, tn), lambda i, j: (i, nj + j)),
                  pl.BlockSpec((ka, tn), lambda i, j: (0, j)), pl.BlockSpec((kr, tn), lambda i, j: (0, j))],
        out_specs=pl.BlockSpec((tm, tn), lambda i, j: (i, j)),
        out_shape=jax.ShapeDtypeStruct((m, d), BF16),
        compiler_params=_cparams(("parallel", "parallel")),
    )(attn, rw, gates, gates, wa, wr)


def _resid_mm_body(a_ref, w_ref, x_ref, o_ref):
    o_ref[...] = x_ref[...] + _dg(a_ref[...], w_ref[...])


def _resid_matmul(a, w, x, tm, tn):
    m, k = a.shape
    n = w.shape[1]
    return pl.pallas_call(
        _resid_mm_body,
        grid=(m // tm, n // tn),
        in_specs=[pl.BlockSpec((tm, k), lambda i, j: (i, 0)), pl.BlockSpec((k, tn), lambda i, j: (0, j)),
                  pl.BlockSpec((tm, tn), lambda i, j: (i, j))],
        out_specs=pl.BlockSpec((tm, tn), lambda i, j: (i, j)),
        out_shape=jax.ShapeDtypeStruct((m, n), F32),
        compiler_params=_cparams(("parallel", "parallel")),
    )(a, w, x)


def _ffn_up_body(x_ref, g_ref, wg_ref, wu_ref, o_ref, xn_ref):
    @pl.when(pl.program_id(1) == 0)
    def _():
        xn_ref[...] = _rms(x_ref[...], g_ref[...]).astype(BF16)

    xn = xn_ref[...]
    o_ref[...] = (jax.nn.silu(_dg(xn, wg_ref[...])) * _dg(xn, wu_ref[...])).astype(o_ref.dtype)


def _ffn_up(x, g, wg, wu, tm, tn):
    m, d = x.shape
    f = wg.shape[1]
    wspec = pl.BlockSpec((d, tn), lambda i, j: (0, j))
    return pl.pallas_call(
        _ffn_up_body,
        grid=(m // tm, f // tn),
        in_specs=[pl.BlockSpec((tm, d), lambda i, j: (i, 0)), pl.BlockSpec((1, d), lambda i, j: (0, 0)), wspec, wspec],
        out_specs=pl.BlockSpec((tm, tn), lambda i, j: (i, j)),
        out_shape=jax.ShapeDtypeStruct((m, f), BF16),
        scratch_shapes=[pltpu.VMEM((tm, d), BF16)],
        compiler_params=_cparams(("parallel", "arbitrary")),
    )(x, g, wg, wu)


def _ffn_down_body(h_ref, w_ref, x_ref, g_ref, o_ref, acc_ref, *, nk):
    k = pl.program_id(1)

    @pl.when(k == 0)
    def _():
        acc_ref[...] = jnp.zeros_like(acc_ref)

    acc_ref[...] += _dg(h_ref[...], w_ref[...])

    @pl.when(k == nk - 1)
    def _():
        o_ref[...] = _rms(x_ref[...] + acc_ref[...], g_ref[...])


def _ffn_down(h, w, x, g, tm, tk):
    m, f = h.shape
    d = w.shape[1]
    nk = f // tk
    return pl.pallas_call(
        functools.partial(_ffn_down_body, nk=nk),
        grid=(m // tm, nk),
        in_specs=[pl.BlockSpec((tm, tk), lambda i, k: (i, k)), pl.BlockSpec((tk, d), lambda i, k: (k, 0)),
                  pl.BlockSpec((tm, d), lambda i, k: (i, 0)), pl.BlockSpec((1, d), lambda i, k: (0, 0))],
        out_specs=pl.BlockSpec((tm, d), lambda i, k: (i, 0)),
        out_shape=jax.ShapeDtypeStruct((m, d), F32),
        scratch_shapes=[pltpu.VMEM((tm, d), F32)],
        compiler_params=_cparams(("parallel", "arbitrary")),
    )(h, w, x, g)


def _pick(total, pref):
    t = min(pref, total)
    while total % t:
        t //= 2
    return t


def _layer(x2, n, s, pos, wts, par, p, attend, rwkv):
    m, d = x2.shape
    att_w = wts['qkv'].shape[1] // 3
    rw_w = wts['rkv'].shape[1] // 3
    tm = _pick(m, 512)
    g_mix = p['g_mix'].reshape(1, d)

    cos_t, sin_a, sin_b = _rope_tables(pos)
    per = max(s // tm, 1)
    if s < tm:
        cos_t, sin_a, sin_b = (jnp.tile(t, (tm // s, 1)) for t in (cos_t, sin_a, sin_b))
    tab = pl.BlockSpec((tm, ATT_HEAD_DIM), lambda i, j: (i % per, 0))
    qkv = _norm_matmul(x2, g_mix, wts['qkv'], tm, att_w, _epi_rope, (cos_t, sin_a, sin_b), (tab, tab, tab))
    rkv = _norm_matmul(x2, g_mix, wts['rkv'], tm, rw_w)
    low = _norm_matmul(x2, g_mix, wts['low'], tm, LOW_PAD)
    tn_g = _pick(2 * d, 1024)
    gates = _norm_matmul(x2, g_mix, wts['gate'], tm, tn_g, _epi_sigmoid_bias, (p['b_gate'].reshape(1, 2 * d),),
                         (pl.BlockSpec((1, tn_g), lambda i, j: (0, j)),))

    attn = attend(qkv)
    rw, wkv_new = rwkv(rkv, low)

    merged = _merge(attn, rw, gates, wts['proj_attn'], wts['proj_rwkv'], tm, _pick(d, 1024))
    hid = _resid_matmul(merged, wts['out'], x2, tm, _pick(d, 1024))
    f = wts['ffn_gate'].shape[1]
    h = _ffn_up(hid, p['g_ffn'].reshape(1, d), wts['ffn_gate'], wts['ffn_up'], tm, _pick(f, 512))
    y = _ffn_down(h, wts['ffn_down'], hid, p['g_final'].reshape(1, d), tm, _pick(f, 512))

    k = qkv[:, att_w:2 * att_w]
    v = qkv[:, 2 * att_w:]
    sh_last = jnp.concatenate([rkv.reshape(n, s, -1)[:, -1], low.reshape(n, s, -1)[:, -1, :LOW_W]], axis=1)
    return y, k, v, wkv_new, sh_last


def kernel(x_prompt, x_sample, cache_k, cache_v, state_wkv, state_shift, page_table, g_mix, w_in, b_gate, mu_shift, w0, w_decay_up, a0, w_iclr_up, w_gate_up, k_k, k_a, r_k, ln_x_w, ln_x_b, w_proj_attn, w_proj_rwkv, w_out, g_ffn, w_ffn_gate, w_ffn_up, w_ffn_down, g_final):
    depth = w_in.shape[0]
    assert depth == 1, "single-layer trunk"
    n, s, d = x_prompt.shape
    db, ds, _ = x_sample.shape
    assert ds == 1, "one new token per decode sequence"
    _, n_pool, page, nh, dh = cache_k.shape
    assert page == PAGE_SIZE and dh == ATT_HEAD_DIM
    att_w = nh * dh
    rw_w = w0.shape[1]
    n_pages = page_table.shape[1]
    past = n_pages * PAGE_SIZE
    assert s % MOBA_BLOCK == 0 and past % MOBA_BLOCK == 0
    l = 0

    wi = w_in[l]
    o = 3 * att_w
    wts = {
        'qkv': wi[:, :o].astype(BF16),
        'rkv': wi[:, o:o + 3 * rw_w].astype(BF16),
        'low': jnp.pad(wi[:, o + 3 * rw_w:o + 3 * rw_w + LOW_W], ((0, 0), (0, LOW_PAD - LOW_W))).astype(BF16),
        'gate': wi[:, o + 3 * rw_w + LOW_W:].astype(BF16),
        'proj_attn': w_proj_attn[l].astype(BF16), 'proj_rwkv': w_proj_rwkv[l].astype(BF16),
        'out': w_out[l].astype(BF16), 'ffn_gate': w_ffn_gate[l].astype(BF16),
        'ffn_up': w_ffn_up[l].astype(BF16), 'ffn_down': w_ffn_down[l].astype(BF16),
    }
    p = {'g_mix': g_mix[l], 'b_gate': b_gate[l], 'g_ffn': g_ffn[l], 'g_final': g_final,
         'mu_shift': mu_shift[l], 'w0': w0[l], 'w_decay_up': w_decay_up[l], 'a0': a0[l],
         'w_iclr_up': w_iclr_up[l], 'w_gate_up': w_gate_up[l], 'k_k': k_k[l], 'k_a': k_a[l], 'r_k': r_k[l]}
    par = _rwkv_params(p, rw_w)
    ln_w, ln_b = ln_x_w[l], ln_x_b[l]

    yp, kp, vp, wp, sp = _layer(
        x_prompt.reshape(n * s, d), n, s, jnp.arange(s), wts, par, p,
        lambda qkv: _moba_prompt(qkv, n, s),
        lambda rkv, low: _rwkv_prompt(rkv, low, par, ln_w, ln_b, n, s))

    ck3 = cache_k[l].reshape(n_pool, PAGE_SIZE, att_w)
    cv3 = cache_v[l].reshape(n_pool, PAGE_SIZE, att_w)
    q_blk = past // MOBA_BLOCK

    def attend_sample(qkv):
        q, k_new, v_new = qkv[:, :att_w], qkv[:, att_w:2 * att_w], qkv[:, 2 * att_w:]
        blk_sum = _page_blocksum(ck3, page_table).reshape(db, q_blk, att_w)
        sel = _decode_select(blk_sum, q, q_blk)
        return _decode_attn(page_table, sel, q, k_new, v_new, ck3, cv3)

    sh_prev = state_shift[l]
    prev_rkv = sh_prev[:, :3 * rw_w]
    prev_low = jnp.pad(sh_prev[:, 3 * rw_w:], ((0, 0), (0, LOW_PAD - LOW_W)))
    ys, ks_, vs_, ws_, ss_ = _layer(
        x_sample.reshape(db * ds, d), db, ds, past + jnp.arange(ds), wts, par, p,
        attend_sample,
        lambda rkv, low: _rwkv_sample(rkv, low, prev_rkv, prev_low, state_wkv[l], par, ln_w, ln_b))

    return (yp.reshape(n, s, d), ys.reshape(db, ds, d),
            kp.reshape(1, n, s, nh, dh), vp.reshape(1, n, s, nh, dh),
            wp.astype(state_wkv.dtype)[None], sp.astype(state_shift.dtype)[None],
            ks_.reshape(1, db, ds, nh, dh), vs_.reshape(1, db, ds, nh, dh),
            ws_.astype(state_wkv.dtype)[None], ss_.astype(state_shift.dtype)[None])
```

```python
import functools
import math

import numpy as np
import jax
import jax.numpy as jnp
from jax import lax
from jax.experimental import pallas as pl
from jax.experimental.pallas import tpu as pltpu

F32 = jnp.float32
BF16 = jnp.bfloat16

ATT_HEAD_DIM = 128
ROT_DIM = ATT_HEAD_DIM // 4
ROPE_THETA = 500000.0
MOBA_BLOCK = 256
MOBA_TOP_K = 3
PAGE_SIZE = 128
RWKV_HEAD_DIM = 64
DECAY_RANK = 64
ICLR_RANK = 64
GATE_RANK = 160
RMS_EPS = 1e-6
GN_EPS = 64e-5
L2_EPS = 1e-12

LANES = 128
LOW_W = DECAY_RANK + ICLR_RANK + GATE_RANK
LOW_PAD = -(-LOW_W // LANES) * LANES
WKV_CHUNK = 64
VMEM_LIMIT = 56 * 1024 * 1024

NN = (((1,), (0,)), ((), ()))
NT = (((1,), (1,)), ((), ()))
TN = (((0,), (0,)), ((), ()))


def _cparams(sem):
    return pltpu.CompilerParams(dimension_semantics=sem, vmem_limit_bytes=VMEM_LIMIT)


def _dg(a, b, dims=NN):
    return lax.dot_general(a, b, dims, preferred_element_type=F32)


def _split2(x):
    hi = x.astype(BF16)
    lo = (x - hi.astype(F32)).astype(BF16)
    return hi, lo


def _split3(x):
    hi = x.astype(BF16)
    r1 = x - hi.astype(F32)
    mid = r1.astype(BF16)
    lo = (r1 - mid.astype(F32)).astype(BF16)
    return hi, mid, lo


def _dot3(a, b, dims=NN):
    ah, al = _split2(a)
    bh, bl = _split2(b)
    return _dg(ah, bh, dims) + (_dg(ah, bl, dims) + _dg(al, bh, dims))


def _dot3_pre(a, bh, bl, dims=NN):
    ah, al = _split2(a)
    return _dg(ah, bh, dims) + (_dg(ah, bl, dims) + _dg(al, bh, dims))


def _dot_exact_rhs(a, b_bf16, dims=NN):
    hi, mid, lo = _split3(a)
    return _dg(hi, b_bf16, dims) + (_dg(mid, b_bf16, dims) + _dg(lo, b_bf16, dims))


def _dot_exact_lhs(a_bf16, b, dims=NN):
    hi, mid, lo = _split3(b)
    return _dg(a_bf16, hi, dims) + (_dg(a_bf16, mid, dims) + _dg(a_bf16, lo, dims))


def _rms(x, g):
    return x * lax.rsqrt(jnp.mean(x * x, axis=-1, keepdims=True) + RMS_EPS) * g


def _norm_mm_body(*refs, epilogue, n_aux):
    x_ref, g_ref, w_ref = refs[:3]
    aux = refs[3:3 + n_aux]
    o_ref, xn_ref = refs[3 + n_aux:]
    j = pl.program_id(1)

    @pl.when(j == 0)
    def _():
        xn_ref[...] = _rms(x_ref[...], g_ref[...]).astype(BF16)

    acc = _dg(xn_ref[...], w_ref[...])
    o_ref[...] = epilogue(acc, j, *aux).astype(o_ref.dtype)


def _epi_none(acc, j):
    return acc


def _epi_rope(acc, j, cos_ref, sa_ref, sb_ref):
    c, sa, sb = cos_ref[...], sa_ref[...], sb_ref[...]
    half = ROT_DIM // 2
    outs = []
    for h in range(acc.shape[1] // ATT_HEAD_DIM):
        xh = acc[:, h * ATT_HEAD_DIM:(h + 1) * ATT_HEAD_DIM]
        outs.append(xh * c + pltpu.roll(xh, ATT_HEAD_DIM - half, 1) * sa + pltpu.roll(xh, half, 1) * sb)
    rot = jnp.concatenate(outs, axis=1)
    rot = rot * jnp.where(j == 0, ATT_HEAD_DIM ** -0.5, 1.0).astype(F32)
    return jnp.where(j < 2, rot, acc)


def _epi_sigmoid_bias(acc, j, b_ref):
    return jax.nn.sigmoid(acc + b_ref[...])


def _norm_matmul(x, g, w, tm, tn, name, epilogue=_epi_none, aux=(), aux_specs=(), out_dtype=F32):
    m, d = x.shape
    n = w.shape[1]
    body = functools.partial(_norm_mm_body, epilogue=epilogue, n_aux=len(aux))
    return pl.pallas_call(
        body,
        grid=(m // tm, n // tn),
        in_specs=[pl.BlockSpec((tm, d), lambda i, j: (i, 0)),
                  pl.BlockSpec((1, d), lambda i, j: (0, 0)),
                  pl.BlockSpec((d, tn), lambda i, j: (0, j)), *aux_specs],
        out_specs=pl.BlockSpec((tm, tn), lambda i, j: (i, j)),
        out_shape=jax.ShapeDtypeStruct((m, n), out_dtype),
        scratch_shapes=[pltpu.VMEM((tm, d), BF16)],
        compiler_params=_cparams(("parallel", "arbitrary")),
        name=name,
    )(x, g, w, *aux)


def _rope_tables(pos):
    half = ROT_DIM // 2
    inv = jnp.exp(jnp.arange(half, dtype=F32) * (-2.0 * math.log(ROPE_THETA) / ROT_DIM))
    ang = pos.astype(F32)[:, None] * inv[None, :]
    cos, sin = jnp.cos(ang), jnp.sin(ang)
    s = pos.shape[0]
    z = lambda w: jnp.zeros((s, w), F32)
    cos_t = jnp.concatenate([cos, cos, jnp.ones((s, ATT_HEAD_DIM - ROT_DIM), F32)], axis=1)
    sin_a = jnp.concatenate([-sin, z(ATT_HEAD_DIM - half)], axis=1)
    sin_b = jnp.concatenate([z(half), sin, z(ATT_HEAD_DIM - ROT_DIM)], axis=1)
    return cos_t, sin_a, sin_b


def _moba_prompt_body(q_ref, k_ref, v_ref, o_ref, km_ref, kb_ref, vt_ref, sel_ref, *, nb):
    qb = pl.program_id(2)
    blk = MOBA_BLOCK

    @pl.when(qb == 0)
    def _():
        km_ref[...] = jnp.zeros_like(km_ref)
        for j in range(nb):
            kj = k_ref[j * blk:(j + 1) * blk, :]
            km_ref[j:j + 1, :] = jnp.mean(kj, axis=0, keepdims=True)
            kb_ref[j] = kj.astype(BF16)
            vt_ref[j] = v_ref[j * blk:(j + 1) * blk, :].T.astype(BF16)

    q = q_ref[...]
    gate = _dot3(km_ref[...], q, NT)
    rowi = lax.broadcasted_iota(jnp.int32, gate.shape, 0)
    valid = rowi < qb
    gm = jnp.where(valid, gate, -jnp.inf)
    cnt = jnp.zeros(gate.shape, F32)
    for i in range(nb):
        gi = gm[i:i + 1, :]
        cnt = cnt + jnp.where(gi > gm, 1.0, 0.0) + jnp.where(gi == gm, (rowi > i).astype(F32), 0.0)
    sel_ref[...] = jnp.where(valid, jnp.where(cnt < MOBA_TOP_K, 1.0, 0.0), 0.0)

    qh = q.astype(BF16)
    keyi = lax.broadcasted_iota(jnp.int32, (blk, blk), 0)
    qi = lax.broadcasted_iota(jnp.int32, (blk, blk), 1)
    s = jnp.where(keyi <= qi, _dg(kb_ref[qb], qh, NT), -jnp.inf)
    m = jnp.max(s, axis=0, keepdims=True)
    p = jnp.exp(s - m)
    l = jnp.sum(p, axis=0, keepdims=True)
    acc = _dg(vt_ref[qb], p.astype(BF16))

    def step(i, carry):
        m, l, acc = carry
        j0, j1 = 2 * i, 2 * i + 1
        s0 = jnp.where(sel_ref[pl.ds(j0, 1), :] > 0.0, _dg(kb_ref[j0], qh, NT), -jnp.inf)
        s1 = jnp.where(sel_ref[pl.ds(j1, 1), :] > 0.0, _dg(kb_ref[j1], qh, NT), -jnp.inf)
        m_new = jnp.maximum(m, jnp.maximum(jnp.max(s0, axis=0, keepdims=True), jnp.max(s1, axis=0, keepdims=True)))
        alpha = jnp.exp(m - m_new)
        p0 = jnp.exp(s0 - m_new)
        p1 = jnp.exp(s1 - m_new)
        l = alpha * l + (jnp.sum(p0, axis=0, keepdims=True) + jnp.sum(p1, axis=0, keepdims=True))
        acc = alpha * acc + (_dg(vt_ref[j0], p0.astype(BF16)) + _dg(vt_ref[j1], p1.astype(BF16)))
        return m_new, l, acc

    m, l, acc = lax.fori_loop(0, (qb + 1) // 2, step, (m, l, acc))
    o_ref[...] = (acc / l).T.astype(o_ref.dtype)


def _moba_prompt(qkv, n, s):
    att_w = qkv.shape[1] // 3
    nh = att_w // ATT_HEAD_DIM
    nb = s // MOBA_BLOCK
    nbp = -(-(nb + 1) // 8) * 8
    return pl.pallas_call(
        functools.partial(_moba_prompt_body, nb=nb),
        grid=(n, nh, nb),
        in_specs=[pl.BlockSpec((MOBA_BLOCK, ATT_HEAD_DIM), lambda b, h, i: (b * nb + i, h)),
                  pl.BlockSpec((s, ATT_HEAD_DIM), lambda b, h, i: (b, nh + h)),
                  pl.BlockSpec((s, ATT_HEAD_DIM), lambda b, h, i: (b, 2 * nh + h))],
        out_specs=pl.BlockSpec((MOBA_BLOCK, ATT_HEAD_DIM), lambda b, h, i: (b * nb + i, h)),
        out_shape=jax.ShapeDtypeStruct((n * s, att_w), BF16),
        scratch_shapes=[pltpu.VMEM((nbp, ATT_HEAD_DIM), F32),
                        pltpu.VMEM((nb, MOBA_BLOCK, ATT_HEAD_DIM), BF16),
                        pltpu.VMEM((nb, ATT_HEAD_DIM, MOBA_BLOCK), BF16),
                        pltpu.VMEM((nbp, MOBA_BLOCK), F32)],
        compiler_params=_cparams(("parallel", "parallel", "arbitrary")),
        name="moba_prompt",
    )(qkv, qkv, qkv)


def _ring_step(copies, g, nsteps):
    slot = g % 2

    @pl.when(g == 0)
    def _():
        for c in copies(0, 0):
            c.start()

    @pl.when(g + 1 < nsteps)
    def _():
        for c in copies(g + 1, 1 - slot):
            c.start()

    for c in copies(g, slot):
        c.wait()
    return slot


def _page_blocksum_body(pt_ref, ck_ref, o_ref, buf, sem, *, pps, ppb, steps_per_seq, nsteps, layer):
    g = pl.program_id(0)

    def copies(step, slot):
        b = step // steps_per_seq
        p0 = (step % steps_per_seq) * pps
        return [pltpu.make_async_copy(ck_ref.at[layer, pt_ref[b, p0 + j]], buf.at[slot, j], sem.at[slot])
                for j in range(pps)]

    slot = _ring_step(copies, g, nsteps)
    for j in range(pps // ppb):
        acc = jnp.sum(buf[slot, j * ppb], axis=0)
        for t in range(1, ppb):
            acc = acc + jnp.sum(buf[slot, j * ppb + t], axis=0)
        o_ref[j] = acc


def _page_blocksum(cache_k, page_table, layer):
    db, n_pages = page_table.shape
    _, _, page, nh, dh = cache_k.shape
    ppb = MOBA_BLOCK // PAGE_SIZE
    pps = ppb
    while pps * 2 <= 8 and n_pages % (pps * 2) == 0:
        pps *= 2
    steps_per_seq = n_pages // pps
    nsteps = db * steps_per_seq
    gs = pltpu.PrefetchScalarGridSpec(
        num_scalar_prefetch=1, grid=(nsteps,),
        in_specs=[pl.BlockSpec(memory_space=pl.ANY)],
        out_specs=pl.BlockSpec((None, pps // ppb, nh, dh),
                               lambda g, pt: (g // steps_per_seq, g % steps_per_seq, 0, 0)),
        scratch_shapes=[pltpu.VMEM((2, pps, page, nh, dh), F32), pltpu.SemaphoreType.DMA((2,))])
    return pl.pallas_call(
        functools.partial(_page_blocksum_body, pps=pps, ppb=ppb, steps_per_seq=steps_per_seq, nsteps=nsteps,
                          layer=layer),
        grid_spec=gs,
        out_shape=jax.ShapeDtypeStruct((db, n_pages // ppb, nh, dh), F32),
        compiler_params=_cparams(("arbitrary",)),
        name="page_blocksum",
    )(page_table, cache_k)


def _decode_select_body(bs_ref, q_ref, sel_ref, *, nbp, q_blk, nh):
    bs = bs_ref[...]
    q = q_ref[...]
    it = lax.broadcasted_iota(jnp.int32, (nbp, 1), 0)
    for h in range(nh):
        sl = slice(h * ATT_HEAD_DIM, (h + 1) * ATT_HEAD_DIM)
        g = jnp.sum(q[:, sl] * (bs[:, sl] / MOBA_BLOCK), axis=1, keepdims=True)
        g = jnp.where(it < q_blk, g, -jnp.inf)
        for t in range(MOBA_TOP_K):
            mx = jnp.max(g, axis=0, keepdims=True)
            idx = jnp.min(jnp.where(g == mx, it, nbp), axis=0, keepdims=True)
            sel_ref[t:t + 1, h:h + 1] = jnp.where(mx > -jnp.inf, idx, -1)
            g = jnp.where(it == idx, -jnp.inf, g)


def _decode_select(blk_sum, q, q_blk):
    db, nbp, att_w = blk_sum.shape
    nh = att_w // ATT_HEAD_DIM
    return pl.pallas_call(
        functools.partial(_decode_select_body, nbp=nbp, q_blk=q_blk, nh=nh),
        grid=(db,),
        in_specs=[pl.BlockSpec((None, nbp, att_w), lambda b: (b, 0, 0)),
                  pl.BlockSpec((None, 1, att_w), lambda b: (b, 0, 0))],
        out_specs=pl.BlockSpec((None, MOBA_TOP_K, nh), lambda b: (b, 0, 0)),
        out_shape=jax.ShapeDtypeStruct((db, MOBA_TOP_K, nh), jnp.int32),
        compiler_params=_cparams(("parallel",)),
        name="decode_select",
    )(blk_sum, q.reshape(db, 1, att_w))


def _decode_attn_body(pt_ref, sel_ref, q_ref, kn_ref, vn_ref, ck_ref, cv_ref, o_ref, kbuf, vbuf, sem,
                      *, ppb, nh, nsteps, layer):
    g = pl.program_id(0)
    npg = MOBA_TOP_K * ppb

    def copies(step, slot):
        b, h = step // nh, step % nh
        cps = []
        for t in range(npg):
            blk = jnp.maximum(sel_ref[(b * MOBA_TOP_K + t // ppb) * nh + h], 0)
            page = pt_ref[b, blk * ppb + t % ppb]
            cps.append(pltpu.make_async_copy(ck_ref.at[layer, page, :, h, :], kbuf.at[slot, t], sem.at[0, slot]))
            cps.append(pltpu.make_async_copy(cv_ref.at[layer, page, :, h, :], vbuf.at[slot, t], sem.at[1, slot]))
        return cps

    slot = _ring_step(copies, g, nsteps)
    b, h = g // nh, g % nh
    q = q_ref[...]
    qh = q.astype(BF16)
    s_own = jnp.sum(q * kn_ref[...], axis=1, keepdims=True)
    m = s_own
    ss = []
    for t in range(npg):
        ok = sel_ref[(b * MOBA_TOP_K + t // ppb) * nh + h] >= 0
        st = jnp.where(ok, _dg(qh, kbuf[slot, t].astype(BF16), NT), -jnp.inf)
        ss.append(st)
        m = jnp.maximum(m, jnp.max(st, axis=1, keepdims=True))
    p_own = jnp.exp(s_own - m)
    l = p_own
    acc = p_own * vn_ref[...]
    for t in range(npg):
        p = jnp.exp(ss[t] - m)
        l = l + jnp.sum(p, axis=1, keepdims=True)
        acc = acc + _dg(p.astype(BF16), vbuf[slot, t].astype(BF16))
    o_ref[...] = (acc / l).astype(o_ref.dtype)


def _decode_attn(page_table, sel, q, k_new, v_new, cache_k, cache_v, layer):
    db, att_w = q.shape
    nh = att_w // ATT_HEAD_DIM
    ppb = MOBA_BLOCK // PAGE_SIZE
    npg = MOBA_TOP_K * ppb
    nsteps = db * nh
    row = pl.BlockSpec((None, 1, ATT_HEAD_DIM), lambda g, pt, sl: (g // nh, 0, g % nh))
    hbm = pl.BlockSpec(memory_space=pl.ANY)
    gs = pltpu.PrefetchScalarGridSpec(
        num_scalar_prefetch=2, grid=(nsteps,),
        in_specs=[row, row, row, hbm, hbm],
        out_specs=row,
        scratch_shapes=[pltpu.VMEM((2, npg, PAGE_SIZE, ATT_HEAD_DIM), F32),
                        pltpu.VMEM((2, npg, PAGE_SIZE, ATT_HEAD_DIM), F32),
                        pltpu.SemaphoreType.DMA((2, 2))])
    r3 = lambda a: a.reshape(db, 1, att_w)
    out = pl.pallas_call(
        functools.partial(_decode_attn_body, ppb=ppb, nh=nh, nsteps=nsteps, layer=layer), grid_spec=gs,
        out_shape=jax.ShapeDtypeStruct((db, 1, att_w), BF16),
        compiler_params=_cparams(("arbitrary",)),
        name="decode_attn",
    )(page_table, sel.reshape(-1), r3(q), r3(k_new), r3(v_new), cache_k, cache_v)
    return out.reshape(db, att_w)


def _seg_sum(x, e_bf16):
    outs = [_dot_exact_rhs(x[:, c:c + LANES], e_bf16) for c in range(0, x.shape[1], LANES)]
    return outs[0] if len(outs) == 1 else jnp.concatenate(outs, axis=1)


def _rwkv_features(rkv, low, prev_rkv, prev_low, par):
    (mu_rkv, mu_low, w0, a0, k_k, k_a, r_k, wd_h, wd_l, wi_h, wi_l, wg_h, wg_l, e128) = par
    w = w0.shape[1]
    mix = rkv + mu_rkv * (prev_rkv - rkv)
    mlow = low + mu_low * (prev_low - low)
    r, k, v = mix[:, :w], mix[:, w:2 * w], mix[:, 2 * w:]
    lane = lax.broadcasted_iota(jnp.int32, mlow.shape, 1)
    feat = jnp.where(lane < DECAY_RANK, jnp.tanh(mlow),
                     jnp.where(lane < DECAY_RANK + ICLR_RANK, mlow,
                               jnp.where(lane < LOW_W, jax.nn.sigmoid(mlow), 0.0)))
    z = w0 + _dot3_pre(feat, wd_h, wd_l)
    nz = -z
    w_log = -(jnp.maximum(nz, 0.0) + jnp.log1p(jnp.exp(-jnp.abs(nz)))) - 0.5
    logw = -jnp.exp(w_log)
    a = jax.nn.sigmoid(a0 + _dot3_pre(feat, wi_h, wi_l))
    g = _dot3_pre(feat, wg_h, wg_l)
    kk = k * k_k
    kk = kk / jnp.maximum(jnp.sqrt(_seg_sum(kk * kk, e128)), L2_EPS)
    k2 = k * (1.0 + (a - 1.0) * k_a)
    bonus = _seg_sum(r * k2 * r_k, e128) * v
    return r, k2, v, kk, a, logw, g, bonus


_N_PAR = 14


def _rwkv_prep_body(*refs, tm, chunk, tiles_per_seq):
    rkv_ref, low_ref, prkv_ref, plow_ref, frkv_ref, flow_ref, ltri_ref = refs[:7]
    par = tuple(r[...] for r in refs[7:7 + _N_PAR])
    at_ref, bt_ref, kt_ref, rt_ref, v_ref, g_ref, bon_ref, pt_ref = refs[7 + _N_PAR:]
    i = pl.program_id(0)
    rkv, low = rkv_ref[...], low_ref[...]
    seq_start = i % tiles_per_seq == 0

    def prev_of(x, tail_ref, first_ref):
        first = jnp.where(seq_start, first_ref[...], tail_ref[7:8, :])
        rowi = lax.broadcasted_iota(jnp.int32, x.shape, 0)
        return jnp.where(rowi == 0, first, pltpu.roll(x, 1, 0))

    prev_rkv = prev_of(rkv, prkv_ref, frkv_ref)
    prev_low = prev_of(low, plow_ref, flow_ref)
    r, k2, v, kk, a, logw, g, bonus = _rwkv_features(rkv, low, prev_rkv, prev_low, par)
    cum = _dot_exact_lhs(ltri_ref[...], logw)
    e_in = jnp.exp(cum)
    e_out = jnp.exp(-cum)
    at_ref[...] = (-kk * jnp.exp(cum - logw)).astype(at_ref.dtype)
    bt_ref[...] = (kk * a * e_out).astype(bt_ref.dtype)
    kt_ref[...] = (k2 * e_out).astype(kt_ref.dtype)
    rt_ref[...] = (r * e_in).astype(rt_ref.dtype)
    v_ref[...] = v.astype(v_ref.dtype)
    g_ref[...] = g
    bon_ref[...] = bonus
    for c in range(tm // chunk):
        pt_ref[c] = e_in[(c + 1) * chunk - 1:(c + 1) * chunk, :]


def _wkv_chunk_body(at_ref, bt_ref, kt_ref, rt_ref, v_ref, pt_ref, g_ref, bon_ref, lnw_ref, lnb_ref, e_ref,
                    rw_ref, st_ref, s_scr, *, nc, chunk, npairs):
    T = chunk
    hp = LANES // RWKV_HEAD_DIM
    W = hp * T

    @pl.when(pl.program_id(1) == 0)
    def _():
        s_scr[...] = jnp.zeros_like(s_scr)

    lane_head = lax.broadcasted_iota(jnp.int32, (1, 1, LANES), 2) // RWKV_HEAD_DIM
    ri = lax.broadcasted_iota(jnp.int32, (1, W, W), 1)
    ci = lax.broadcasted_iota(jnp.int32, (1, W, W), 2)
    eye = jnp.where(ri == ci, 1.0, 0.0).astype(F32)
    zero = jnp.zeros((), BF16)
    e1 = functools.partial(jnp.einsum, preferred_element_type=F32)
    gram = functools.partial(e1, 'bid,bjd->bij')
    mm = functools.partial(e1, 'bij,bjd->bid')
    mm_nt = functools.partial(e1, 'bwk,bvk->bwv')
    bf = lambda t: t.astype(BF16)
    e128 = e_ref[...]
    lnw, lnb = lnw_ref[...], lnb_ref[...]

    def by_pair(x):
        return jnp.stack([x[:, p * LANES:(p + 1) * LANES] for p in range(npairs)], axis=0)

    def stacked(ref, rows):
        x = by_pair(ref[rows, :])
        return jnp.concatenate([jnp.where(lane_head == h, x, zero) for h in range(hp)], axis=1)

    def chunk_step(c, carry):
        rows = pl.ds(pl.multiple_of(c * T, T), T)
        a2, b2, k2, r2, v2 = (stacked(r, rows) for r in (at_ref, bt_ref, kt_ref, rt_ref, v_ref))
        gab = jnp.where(ci < ri, gram(a2, b2), 0.0)
        gak = bf(jnp.where(ci < ri, gram(a2, k2), 0.0))
        grb = bf(jnp.where(ci <= ri, gram(r2, b2), 0.0))
        grk = bf(jnp.where(ci <= ri, gram(r2, k2), 0.0))
        x = eye + gab
        lp = gab
        span = 2
        while span < T:
            lp = mm(bf(lp), bf(lp))
            x = mm(bf(x), bf(eye + lp))
            span *= 2
        xb = bf(x)
        ahat = bf(mm(xb, a2))
        uhat = mm(xb, bf(mm(gak, v2)))
        kv = e1('btv,btk->bvk', v2, k2)

        s = s_scr[...]
        sb = bf(s)
        u = mm_nt(ahat, sb) + uhat
        ub = bf(u)
        y2 = mm_nt(r2, sb) + (mm(grb, ub) + mm(grk, v2))
        s_scr[...] = (s + e1('bwv,bwk->bvk', ub, b2) + kv) * by_pair(pt_ref[c])

        y3 = y2[:, :T]
        for h in range(1, hp):
            y3 = y3 + y2[:, h * T:(h + 1) * T]
        y = jnp.concatenate([y3[p] for p in range(npairs)], axis=1)
        mu = _seg_sum(y, e128) / RWKV_HEAD_DIM
        yc = y - mu
        var = _seg_sum(yc * yc, e128) / RWKV_HEAD_DIM
        yn = yc * lax.rsqrt(var + GN_EPS) * lnw + lnb
        rw_ref[rows, :] = ((yn + bon_ref[rows, :]) * g_ref[rows, :]).astype(rw_ref.dtype)
        return carry

    lax.fori_loop(0, nc, chunk_step, 0)
    st_ref[...] = s_scr[...]


def _wkv_step_body(r_ref, k_ref, v_ref, kk_ref, a_ref, lw_ref, g_ref, bon_ref, lnw_ref, lnb_ref, s0_ref,
                   rw_ref, s1_ref, *, nh):
    d = s0_ref.shape[-1]
    eye = lax.broadcasted_iota(jnp.int32, (d, d), 0) == lax.broadcasted_iota(jnp.int32, (d, d), 1)
    for h in range(nh):
        s0 = s0_ref[h]
        r, k, v, kk, a = r_ref[h], k_ref[h], v_ref[h], kk_ref[h], a_ref[h]
        v_col = jnp.sum(jnp.where(eye, v, 0.0), axis=1, keepdims=True)
        sa = jnp.sum(s0 * (-kk), axis=1, keepdims=True)
        s1 = s0 * jnp.exp(lw_ref[h]) + sa * (kk * a) + v_col * k
        s1_ref[h] = s1
        y_col = jnp.sum(s1 * r, axis=1, keepdims=True)
        y = jnp.sum(jnp.where(eye, y_col, 0.0), axis=0, keepdims=True)
        mu = jnp.mean(y, axis=1, keepdims=True)
        var = jnp.mean(jnp.square(y - mu), axis=1, keepdims=True)
        yn = (y - mu) * lax.rsqrt(var + GN_EPS) * lnw_ref[h] + lnb_ref[h]
        rw_ref[h] = ((yn + bon_ref[h]) * g_ref[h]).astype(rw_ref.dtype)


def _rwkv_feat_body(*refs):
    rkv_ref, low_ref, prkv_ref, plow_ref = refs[:4]
    par = tuple(r[...] for r in refs[4:4 + _N_PAR])
    outs = refs[4 + _N_PAR:]
    vals = _rwkv_features(rkv_ref[...], low_ref[...], prkv_ref[...], plow_ref[...], par)
    for o, x in zip(outs, vals):
        o[...] = x


def _rwkv_params(p, rwkv_w):
    def pad_up(wt, r0):
        full = jnp.zeros((LOW_PAD, rwkv_w), F32).at[r0:r0 + wt.shape[0]].set(wt)
        hi = full.astype(BF16)
        return hi, (full - hi.astype(F32)).astype(BF16)

    wd_h, wd_l = pad_up(p['w_decay_up'], 0)
    wi_h, wi_l = pad_up(p['w_iclr_up'], DECAY_RANK)
    wg_h, wg_l = pad_up(p['w_gate_up'], DECAY_RANK + ICLR_RANK)
    hd = np.arange(LANES) // RWKV_HEAD_DIM
    e128 = jnp.asarray(hd[:, None] == hd[None, :], BF16)
    row = lambda t: t.reshape(1, -1).astype(F32)
    mu = p['mu_shift']
    mu_rkv = row(mu[:3 * rwkv_w])
    mu_low = row(jnp.pad(mu[3 * rwkv_w:], (0, LOW_PAD - LOW_W)))
    return (mu_rkv, mu_low, row(p['w0']), row(p['a0']), row(p['k_k']), row(p['k_a']), row(p['r_k']),
            wd_h, wd_l, wi_h, wi_l, wg_h, wg_l, e128)


def _full_spec(a):
    nd = a.ndim
    return pl.BlockSpec(a.shape, lambda *_: (0,) * nd)


def _rwkv_prompt(rkv, low, par, ln_w, ln_b, n, s):
    m = n * s
    w = rkv.shape[1] // 3
    tm = min(256, s)
    chunk = WKV_CHUNK
    cpt = tm // chunk
    hd = np.arange(tm) // chunk
    ltri = jnp.asarray((hd[:, None] == hd[None, :]) & (np.arange(tm)[:, None] >= np.arange(tm)[None, :]), BF16)
    zeros_rkv = jnp.zeros((n, 1, 3 * w), F32)
    zeros_low = jnp.zeros((n, 1, LOW_PAD), F32)
    tiles_per_seq = s // tm
    tail = lambda width: pl.BlockSpec((8, width), lambda i: (jnp.maximum(i * (tm // 8) - 1, 0), 0))
    first = lambda width: pl.BlockSpec((None, 1, width), lambda i: (i // tiles_per_seq, 0, 0))
    big = pl.BlockSpec((tm, w), lambda i: (i, 0))
    outs = pl.pallas_call(
        functools.partial(_rwkv_prep_body, tm=tm, chunk=chunk, tiles_per_seq=tiles_per_seq),
        grid=(m // tm,),
        in_specs=[pl.BlockSpec((tm, 3 * w), lambda i: (i, 0)), pl.BlockSpec((tm, LOW_PAD), lambda i: (i, 0)),
                  tail(3 * w), tail(LOW_PAD), first(3 * w), first(LOW_PAD), _full_spec(ltri),
                  *[_full_spec(t) for t in par]],
        out_specs=[big] * 7 + [pl.BlockSpec((cpt, 1, w), lambda i: (i, 0, 0))],
        out_shape=[jax.ShapeDtypeStruct((m, w), BF16)] * 5 + [jax.ShapeDtypeStruct((m, w), F32)] * 2
        + [jax.ShapeDtypeStruct((m // chunk, 1, w), F32)],
        compiler_params=_cparams(("parallel",)),
        name="rwkv_prep",
    )(rkv, low, rkv, low, zeros_rkv, zeros_low, ltri, *par)
    at, bt, kt, rt, v, g, bon, ptot = outs

    tb = _pick(s, 512)
    nc = tb // chunk
    npairs = w // LANES
    nblk = s // tb
    tok = pl.BlockSpec((tb, w), lambda b, i: (b * nblk + i, 0))
    vec = pl.BlockSpec((1, w), lambda b, i: (0, 0))
    rw, st = pl.pallas_call(
        functools.partial(_wkv_chunk_body, nc=nc, chunk=chunk, npairs=npairs),
        grid=(n, nblk),
        in_specs=[tok, tok, tok, tok, tok,
                  pl.BlockSpec((nc, 1, w), lambda b, i: (b * nblk + i, 0, 0)),
                  tok, tok, vec, vec, _full_spec(par[-1])],
        out_specs=[tok, pl.BlockSpec((None, npairs, LANES, LANES), lambda b, i: (b, 0, 0, 0))],
        out_shape=[jax.ShapeDtypeStruct((m, w), BF16), jax.ShapeDtypeStruct((n, npairs, LANES, LANES), F32)],
        scratch_shapes=[pltpu.VMEM((npairs, LANES, LANES), F32)],
        compiler_params=_cparams(("parallel", "arbitrary")),
        name="wkv_chunks",
    )(at, bt, kt, rt, v, ptot, g, bon, ln_w.reshape(1, w), ln_b.reshape(1, w), par[-1])
    hp = LANES // RWKV_HEAD_DIM
    d = RWKV_HEAD_DIM
    st = jnp.stack([st[:, :, j * d:(j + 1) * d, j * d:(j + 1) * d] for j in range(hp)], axis=2)
    return rw, st.reshape(n, npairs * hp, d, d)


def _rwkv_sample(rkv, low, prev_rkv, prev_low, state0, par, ln_w, ln_b):
    db = rkv.shape[0]
    w = rkv.shape[1] // 3
    d = RWKV_HEAD_DIM
    nh = w // d
    ins = (rkv, low, prev_rkv, prev_low, *par)
    feats = pl.pallas_call(
        _rwkv_feat_body,
        in_specs=[_full_spec(t) for t in ins],
        out_specs=[pl.BlockSpec((db, w), lambda: (0, 0))] * 8,
        out_shape=[jax.ShapeDtypeStruct((db, w), F32)] * 8,
        compiler_params=pltpu.CompilerParams(vmem_limit_bytes=VMEM_LIMIT),
        name="rwkv_feats",
    )(*ins)
    hv = lambda t: t.reshape(db, nh, 1, d)
    row = pl.BlockSpec((None, nh, 1, d), lambda b: (b, 0, 0, 0))
    prow = pl.BlockSpec((nh, 1, d), lambda b: (0, 0, 0))
    mat = pl.BlockSpec((None, nh, d, d), lambda b: (b, 0, 0, 0))
    rw, s1 = pl.pallas_call(
        functools.partial(_wkv_step_body, nh=nh),
        grid=(db,),
        in_specs=[row] * 8 + [prow, prow, mat],
        out_specs=[row, mat],
        out_shape=[jax.ShapeDtypeStruct((db, nh, 1, d), BF16), jax.ShapeDtypeStruct((db, nh, d, d), F32)],
        compiler_params=_cparams(("parallel",)),
        name="wkv_step",
    )(*[hv(t) for t in feats], ln_w.reshape(nh, 1, d), ln_b.reshape(nh, 1, d), state0)
    return rw.reshape(db, w), s1


def _merge_body(attn_ref, rw_ref, ga_ref, gb_ref, wa_ref, wr_ref, o_ref):
    ya = _dg(attn_ref[...], wa_ref[...])
    yb = _dg(rw_ref[...], wr_ref[...])
    o_ref[...] = (ga_ref[...] * ya + gb_ref[...] * yb).astype(o_ref.dtype)


def _merge(attn, rw, gates, wa, wr, tm, tn):
    m, ka = attn.shape
    kr = rw.shape[1]
    d = wa.shape[1]
    nj = d // tn
    return pl.pallas_call(
        _merge_body,
        grid=(m // tm, nj),
        in_specs=[pl.BlockSpec((tm, ka), lambda i, j: (i, 0)), pl.BlockSpec((tm, kr), lambda i, j: (i, 0)),
                  pl.BlockSpec((tm, tn), lambda i, j: (i, j)), pl.BlockSpec((tm, tn), lambda i, j: (i, nj + j)),
                  pl.BlockSpec((ka, tn), lambda i, j: (0, j)), pl.BlockSpec((kr, tn), lambda i, j: (0, j))],
        out_specs=pl.BlockSpec((tm, tn), lambda i, j: (i, j)),
        out_shape=jax.ShapeDtypeStruct((m, d), BF16),
        compiler_params=_cparams(("parallel", "parallel")),
        name="merge",
    )(attn, rw, gates, gates, wa, wr)


def _resid_mm_body(a_ref, w_ref, x_ref, o_ref):
    o_ref[...] = x_ref[...] + _dg(a_ref[...], w_ref[...])


def _resid_matmul(a, w, x, tm, tn):
    m, k = a.shape
    n = w.shape[1]
    return pl.pallas_call(
        _resid_mm_body,
        grid=(m // tm, n // tn),
        in_specs=[pl.BlockSpec((tm, k), lambda i, j: (i, 0)), pl.BlockSpec((k, tn), lambda i, j: (0, j)),
                  pl.BlockSpec((tm, tn), lambda i, j: (i, j))],
        out_specs=pl.BlockSpec((tm, tn), lambda i, j: (i, j)),
        out_shape=jax.ShapeDtypeStruct((m, n), F32),
        compiler_params=_cparams(("parallel", "parallel")),
        name="out_proj",
    )(a, w, x)


def _ffn_up_body(x_ref, g_ref, wg_ref, wu_ref, o_ref, xn_ref):
    @pl.when(pl.program_id(1) == 0)
    def _():
        xn_ref[...] = _rms(x_ref[...], g_ref[...]).astype(BF16)

    xn = xn_ref[...]
    o_ref[...] = (jax.nn.silu(_dg(xn, wg_ref[...])) * _dg(xn, wu_ref[...])).astype(o_ref.dtype)


def _ffn_up(x, g, wg, wu, tm, tn):
    m, d = x.shape
    f = wg.shape[1]
    wspec = pl.BlockSpec((d, tn), lambda i, j: (0, j))
    return pl.pallas_call(
        _ffn_up_body,
        grid=(m // tm, f // tn),
        in_specs=[pl.BlockSpec((tm, d), lambda i, j: (i, 0)), pl.BlockSpec((1, d), lambda i, j: (0, 0)), wspec, wspec],
        out_specs=pl.BlockSpec((tm, tn), lambda i, j: (i, j)),
        out_shape=jax.ShapeDtypeStruct((m, f), BF16),
        scratch_shapes=[pltpu.VMEM((tm, d), BF16)],
        compiler_params=_cparams(("parallel", "arbitrary")),
        name="ffn_up",
    )(x, g, wg, wu)


def _ffn_down_body(h_ref, w_ref, x_ref, g_ref, o_ref, *, nj, tn):
    j = pl.program_id(1)
    val = x_ref[...] + _dg(h_ref[...], w_ref[...])
    for t in range(nj):
        @pl.when(j == t)
        def _(t=t):
            o_ref[:, t * tn:(t + 1) * tn] = val

    @pl.when(j == nj - 1)
    def _():
        o_ref[...] = _rms(o_ref[...], g_ref[...])


def _ffn_down(h, w, x, g, tm, tn):
    m, f = h.shape
    d = w.shape[1]
    nj = d // tn
    return pl.pallas_call(
        functools.partial(_ffn_down_body, nj=nj, tn=tn),
        grid=(m // tm, nj),
        in_specs=[pl.BlockSpec((tm, f), lambda i, j: (i, 0)), pl.BlockSpec((f, tn), lambda i, j: (0, j)),
                  pl.BlockSpec((tm, tn), lambda i, j: (i, j)), pl.BlockSpec((1, d), lambda i, j: (0, 0))],
        out_specs=pl.BlockSpec((tm, d), lambda i, j: (i, 0)),
        out_shape=jax.ShapeDtypeStruct((m, d), F32),
        compiler_params=_cparams(("parallel", "arbitrary")),
        name="ffn_down",
    )(h, w, x, g)


def _pick(total, pref):
    t = min(pref, total)
    while total % t:
        t //= 2
    return t


def _layer(x2, n, s, pos, wts, par, p, attend, rwkv):
    m, d = x2.shape
    att_w = wts['qkv'].shape[1] // 3
    rw_w = wts['rkv'].shape[1] // 3
    tm = _pick(s, 1024) if s % 8 == 0 else _pick(m, 1024)
    assert s % tm == 0 or tm % s == 0
    g_mix = p['g_mix'].reshape(1, d)

    cos_t, sin_a, sin_b = _rope_tables(pos)
    per = max(s // tm, 1)
    if s < tm:
        cos_t, sin_a, sin_b = (jnp.tile(t, (tm // s, 1)) for t in (cos_t, sin_a, sin_b))
    tab = pl.BlockSpec((tm, ATT_HEAD_DIM), lambda i, j: (i % per, 0))
    qkv = _norm_matmul(x2, g_mix, wts['qkv'], tm, att_w, 'proj_qkv', _epi_rope, (cos_t, sin_a, sin_b), (tab, tab, tab))
    rkv = _norm_matmul(x2, g_mix, wts['rkv'], tm, rw_w, 'proj_rkv')
    low = _norm_matmul(x2, g_mix, wts['low'], tm, LOW_PAD, 'proj_low')
    tn_g = _pick(2 * d, 1024)
    gates = _norm_matmul(x2, g_mix, wts['gate'], tm, tn_g, 'proj_gates', _epi_sigmoid_bias, (p['b_gate'].reshape(1, 2 * d),),
                         (pl.BlockSpec((1, tn_g), lambda i, j: (0, j)),), out_dtype=BF16)

    attn = attend(qkv)
    rw, wkv_new = rwkv(rkv, low)

    merged = _merge(attn, rw, gates, wts['proj_attn'], wts['proj_rwkv'], tm, _pick(d, 1024))
    hid = _resid_matmul(merged, wts['out'], x2, tm, _pick(d, 1024))
    f = wts['ffn_gate'].shape[1]
    h = _ffn_up(hid, p['g_ffn'].reshape(1, d), wts['ffn_gate'], wts['ffn_up'], tm, _pick(f, 512))
    y = _ffn_down(h, wts['ffn_down'], hid, p['g_final'].reshape(1, d), _pick(m, 512), _pick(d, 512))

    k = qkv[:, att_w:2 * att_w]
    v = qkv[:, 2 * att_w:]
    sh_last = jnp.concatenate([rkv.reshape(n, s, -1)[:, -1], low.reshape(n, s, -1)[:, -1, :LOW_W]], axis=1)
    return y, k, v, wkv_new, sh_last


def kernel(x_prompt, x_sample, cache_k, cache_v, state_wkv, state_shift, page_table, g_mix, w_in, b_gate, mu_shift, w0, w_decay_up, a0, w_iclr_up, w_gate_up, k_k, k_a, r_k, ln_x_w, ln_x_b, w_proj_attn, w_proj_rwkv, w_out, g_ffn, w_ffn_gate, w_ffn_up, w_ffn_down, g_final):
    depth = w_in.shape[0]
    assert depth == 1, "single-layer trunk"
    n, s, d = x_prompt.shape
    db, ds, _ = x_sample.shape
    assert ds == 1, "one new token per decode sequence"
    _, n_pool, page, nh, dh = cache_k.shape
    assert page == PAGE_SIZE and dh == ATT_HEAD_DIM
    att_w = nh * dh
    rw_w = w0.shape[1]
    n_pages = page_table.shape[1]
    past = n_pages * PAGE_SIZE
    assert s % MOBA_BLOCK == 0 and past % MOBA_BLOCK == 0
    l = 0

    wi = w_in[l]
    o = 3 * att_w
    wts = {
        'qkv': wi[:, :o].astype(BF16),
        'rkv': wi[:, o:o + 3 * rw_w].astype(BF16),
        'low': jnp.pad(wi[:, o + 3 * rw_w:o + 3 * rw_w + LOW_W], ((0, 0), (0, LOW_PAD - LOW_W))).astype(BF16),
        'gate': wi[:, o + 3 * rw_w + LOW_W:].astype(BF16),
        'proj_attn': w_proj_attn[l].astype(BF16), 'proj_rwkv': w_proj_rwkv[l].astype(BF16),
        'out': w_out[l].astype(BF16), 'ffn_gate': w_ffn_gate[l].astype(BF16),
        'ffn_up': w_ffn_up[l].astype(BF16), 'ffn_down': w_ffn_down[l].astype(BF16),
    }
    p = {'g_mix': g_mix[l], 'b_gate': b_gate[l], 'g_ffn': g_ffn[l], 'g_final': g_final,
         'mu_shift': mu_shift[l], 'w0': w0[l], 'w_decay_up': w_decay_up[l], 'a0': a0[l],
         'w_iclr_up': w_iclr_up[l], 'w_gate_up': w_gate_up[l], 'k_k': k_k[l], 'k_a': k_a[l], 'r_k': r_k[l]}
    par = _rwkv_params(p, rw_w)
    ln_w, ln_b = ln_x_w[l], ln_x_b[l]

    yp, kp, vp, wp, sp = _layer(
        x_prompt.reshape(n * s, d), n, s, jnp.arange(s), wts, par, p,
        lambda qkv: _moba_prompt(qkv, n, s),
        lambda rkv, low: _rwkv_prompt(rkv, low, par, ln_w, ln_b, n, s))

    q_blk = past // MOBA_BLOCK

    def attend_sample(qkv):
        q, k_new, v_new = qkv[:, :att_w], qkv[:, att_w:2 * att_w], qkv[:, 2 * att_w:]
        blk_sum = _page_blocksum(cache_k, page_table, l).reshape(db, q_blk, att_w)
        sel = _decode_select(blk_sum, q, q_blk)
        return _decode_attn(page_table, sel, q, k_new, v_new, cache_k, cache_v, l)

    sh_prev = state_shift[l]
    prev_rkv = sh_prev[:, :3 * rw_w]
    prev_low = jnp.pad(sh_prev[:, 3 * rw_w:], ((0, 0), (0, LOW_PAD - LOW_W)))
    ys, ks_, vs_, ws_, ss_ = _layer(
        x_sample.reshape(db * ds, d), db, ds, past + jnp.arange(ds), wts, par, p,
        attend_sample,
        lambda rkv, low: _rwkv_sample(rkv, low, prev_rkv, prev_low, state_wkv[l], par, ln_w, ln_b))

    return (yp.reshape(n, s, d), ys.reshape(db, ds, d),
            kp.reshape(1, n, s, nh, dh), vp.reshape(1, n, s, nh, dh),
            wp.astype(state_wkv.dtype)[None], sp.astype(state_shift.dtype)[None],
            ks_.reshape(1, db, ds, nh, dh), vs_.reshape(1, db, ds, nh, dh),
            ws_.astype(state_wkv.dtype)[None], ss_.astype(state_shift.dtype)[None])
```

```python
import functools
import math

import numpy as np
import jax
import jax.numpy as jnp
from jax import lax
from jax.experimental import pallas as pl
from jax.experimental.pallas import tpu as pltpu

F32 = jnp.float32
BF16 = jnp.bfloat16

ATT_HEAD_DIM = 128
ROT_DIM = ATT_HEAD_DIM // 4
ROPE_THETA = 500000.0
MOBA_BLOCK = 256
MOBA_TOP_K = 3
PAGE_SIZE = 128
RWKV_HEAD_DIM = 64
DECAY_RANK = 64
ICLR_RANK = 64
GATE_RANK = 160
RMS_EPS = 1e-6
GN_EPS = 64e-5
L2_EPS = 1e-12

LANES = 128
LOW_W = DECAY_RANK + ICLR_RANK + GATE_RANK
LOW_PAD = -(-LOW_W // LANES) * LANES
WKV_CHUNK = 64
VMEM_LIMIT = 56 * 1024 * 1024

NN = (((1,), (0,)), ((), ()))
NT = (((1,), (1,)), ((), ()))
TN = (((0,), (0,)), ((), ()))


def _cparams(sem):
    return pltpu.CompilerParams(dimension_semantics=sem, vmem_limit_bytes=VMEM_LIMIT)


def _dg(a, b, dims=NN):
    return lax.dot_general(a, b, dims, preferred_element_type=F32)


def _split2(x):
    hi = x.astype(BF16)
    lo = (x - hi.astype(F32)).astype(BF16)
    return hi, lo


def _dot3(a, b, dims=NN):
    ah, al = _split2(a)
    bh, bl = _split2(b)
    return _dg(ah, bh, dims) + (_dg(ah, bl, dims) + _dg(al, bh, dims))


def _dot3_pre(a, bh, bl, dims=NN):
    ah, al = _split2(a)
    return _dg(ah, bh, dims) + (_dg(ah, bl, dims) + _dg(al, bh, dims))


def _dot_exact_rhs(a, b_bf16, dims=NN):
    hi, lo = _split2(a)
    return _dg(hi, b_bf16, dims) + _dg(lo, b_bf16, dims)


def _dot_exact_lhs(a_bf16, b, dims=NN):
    hi, lo = _split2(b)
    return _dg(a_bf16, hi, dims) + _dg(a_bf16, lo, dims)


def _rms(x, g):
    return x * lax.rsqrt(jnp.mean(x * x, axis=-1, keepdims=True) + RMS_EPS) * g


def _norm_mm_body(*refs, epilogue, n_aux):
    x_ref, g_ref, w_ref = refs[:3]
    aux = refs[3:3 + n_aux]
    o_ref, xn_ref = refs[3 + n_aux:]
    j = pl.program_id(1)

    @pl.when(j == 0)
    def _():
        xn_ref[...] = _rms(x_ref[...], g_ref[...]).astype(BF16)

    acc = _dg(xn_ref[...], w_ref[...])
    o_ref[...] = epilogue(acc, j, *aux).astype(o_ref.dtype)


def _epi_none(acc, j):
    return acc


def _epi_rope(acc, j, cos_ref, sa_ref, sb_ref):
    c, sa, sb = cos_ref[...], sa_ref[...], sb_ref[...]
    half = ROT_DIM // 2
    outs = []
    for h in range(acc.shape[1] // ATT_HEAD_DIM):
        xh = acc[:, h * ATT_HEAD_DIM:(h + 1) * ATT_HEAD_DIM]
        outs.append(xh * c + pltpu.roll(xh, ATT_HEAD_DIM - half, 1) * sa + pltpu.roll(xh, half, 1) * sb)
    rot = jnp.concatenate(outs, axis=1)
    rot = rot * jnp.where(j == 0, ATT_HEAD_DIM ** -0.5, 1.0).astype(F32)
    return jnp.where(j < 2, rot, acc)


def _epi_sigmoid_bias(acc, j, b_ref):
    return jax.nn.sigmoid(acc + b_ref[...])


def _norm_matmul(x, g, w, tm, tn, name, epilogue=_epi_none, aux=(), aux_specs=(), out_dtype=F32):
    m, d = x.shape
    n = w.shape[1]
    body = functools.partial(_norm_mm_body, epilogue=epilogue, n_aux=len(aux))
    return pl.pallas_call(
        body,
        grid=(m // tm, n // tn),
        in_specs=[pl.BlockSpec((tm, d), lambda i, j: (i, 0)),
                  pl.BlockSpec((1, d), lambda i, j: (0, 0)),
                  pl.BlockSpec((d, tn), lambda i, j: (0, j)), *aux_specs],
        out_specs=pl.BlockSpec((tm, tn), lambda i, j: (i, j)),
        out_shape=jax.ShapeDtypeStruct((m, n), out_dtype),
        scratch_shapes=[pltpu.VMEM((tm, d), BF16)],
        compiler_params=_cparams(("parallel", "arbitrary")),
        name=name,
    )(x, g, w, *aux)


def _rope_tables(pos):
    half = ROT_DIM // 2
    inv = jnp.exp(jnp.arange(half, dtype=F32) * (-2.0 * math.log(ROPE_THETA) / ROT_DIM))
    ang = pos.astype(F32)[:, None] * inv[None, :]
    cos, sin = jnp.cos(ang), jnp.sin(ang)
    s = pos.shape[0]
    z = lambda w: jnp.zeros((s, w), F32)
    cos_t = jnp.concatenate([cos, cos, jnp.ones((s, ATT_HEAD_DIM - ROT_DIM), F32)], axis=1)
    sin_a = jnp.concatenate([-sin, z(ATT_HEAD_DIM - half)], axis=1)
    sin_b = jnp.concatenate([z(half), sin, z(ATT_HEAD_DIM - ROT_DIM)], axis=1)
    return cos_t, sin_a, sin_b


def _moba_prompt_body(q_ref, k_ref, v_ref, o_ref, km_ref, kb_ref, vt_ref, s_ref, *, nb):
    qb = pl.program_id(2)
    blk = MOBA_BLOCK
    sub = 8

    @pl.when(qb == 0)
    def _():
        km_ref[...] = jnp.zeros_like(km_ref)
        for j in range(nb):
            kj = k_ref[j * blk:(j + 1) * blk, :]
            km_ref[j:j + 1, :] = jnp.mean(kj, axis=0, keepdims=True)
            kb_ref[j] = kj.astype(BF16)
            vt_ref[j] = v_ref[j * blk:(j + 1) * blk, :].T.astype(BF16)

    q = q_ref[...]
    gate = _dot3(km_ref[...], q, NT)
    rowi = lax.broadcasted_iota(jnp.int32, gate.shape, 0)
    valid = rowi < qb
    gm = jnp.where(valid, gate, -jnp.inf)
    cnt = jnp.zeros(gate.shape, F32)
    for i in range(nb):
        gi = gm[i:i + 1, :]
        cnt = cnt + jnp.where(gi > gm, 1.0, 0.0) + jnp.where(gi == gm, (rowi > i).astype(F32), 0.0)
    sel = jnp.where(valid, jnp.where(cnt < MOBA_TOP_K, 1.0, 0.0), 0.0)

    qh = q.astype(BF16)
    keyi = lax.broadcasted_iota(jnp.int32, (blk, blk), 0)
    qi = lax.broadcasted_iota(jnp.int32, (blk, blk), 1)
    fold = lambda t, op: op(t.reshape(blk // sub, sub, blk), axis=0)

    def sweep(qbv):
        pm = None
        for j in range(qbv + 1):
            s = _dg(kb_ref[j], qh, NT)
            s = jnp.where(keyi <= qi if j == qbv else sel[j:j + 1, :] > 0.0, s, -jnp.inf)
            s_ref[j] = s
            t = fold(s, jnp.max)
            pm = t if pm is None else jnp.maximum(pm, t)
        m = jnp.max(pm, axis=0, keepdims=True)
        lp = jnp.zeros((sub, blk), F32)
        acc = jnp.zeros((ATT_HEAD_DIM, blk), F32)
        for j in range(qbv + 1):
            p = jnp.exp(s_ref[j] - m)
            lp = lp + fold(p, jnp.sum)
            acc = acc + _dg(vt_ref[j], p.astype(BF16))
        l = jnp.sum(lp, axis=0, keepdims=True)
        o_ref[...] = (acc / l).T.astype(o_ref.dtype)

    for qbv in range(nb):
        pl.when(qb == qbv)(functools.partial(sweep, qbv))


def _moba_prompt(qkv, n, s):
    att_w = qkv.shape[1] // 3
    nh = att_w // ATT_HEAD_DIM
    nb = s // MOBA_BLOCK
    nbp = -(-nb // 8) * 8
    return pl.pallas_call(
        functools.partial(_moba_prompt_body, nb=nb),
        grid=(n, nh, nb),
        in_specs=[pl.BlockSpec((MOBA_BLOCK, ATT_HEAD_DIM), lambda b, h, i: (b * nb + i, h)),
                  pl.BlockSpec((s, ATT_HEAD_DIM), lambda b, h, i: (b, nh + h)),
                  pl.BlockSpec((s, ATT_HEAD_DIM), lambda b, h, i: (b, 2 * nh + h))],
        out_specs=pl.BlockSpec((MOBA_BLOCK, ATT_HEAD_DIM), lambda b, h, i: (b * nb + i, h)),
        out_shape=jax.ShapeDtypeStruct((n * s, att_w), BF16),
        scratch_shapes=[pltpu.VMEM((nbp, ATT_HEAD_DIM), F32),
                        pltpu.VMEM((nb, MOBA_BLOCK, ATT_HEAD_DIM), BF16),
                        pltpu.VMEM((nb, ATT_HEAD_DIM, MOBA_BLOCK), BF16),
                        pltpu.VMEM((nb, MOBA_BLOCK, MOBA_BLOCK), F32)],
        compiler_params=_cparams(("parallel", "parallel", "arbitrary")),
        name="moba_prompt",
    )(qkv, qkv, qkv)


def _ring_step(copies, g, nsteps):
    slot = g % 2

    @pl.when(g == 0)
    def _():
        for c in copies(0, 0):
            c.start()

    @pl.when(g + 1 < nsteps)
    def _():
        for c in copies(g + 1, 1 - slot):
            c.start()

    for c in copies(g, slot):
        c.wait()
    return slot


def _page_blocksum_body(pt_ref, ck_ref, o_ref, buf, sem, *, pps, ppb, steps_per_seq, nsteps, layer):
    g = pl.program_id(0)

    def copies(step, slot):
        b = step // steps_per_seq
        p0 = (step % steps_per_seq) * pps
        return [pltpu.make_async_copy(ck_ref.at[layer, pt_ref[b, p0 + j]], buf.at[slot, j], sem.at[slot])
                for j in range(pps)]

    slot = _ring_step(copies, g, nsteps)
    for j in range(pps // ppb):
        acc = jnp.sum(buf[slot, j * ppb], axis=0)
        for t in range(1, ppb):
            acc = acc + jnp.sum(buf[slot, j * ppb + t], axis=0)
        o_ref[j] = acc


def _page_blocksum(cache_k, page_table, layer):
    db, n_pages = page_table.shape
    _, _, page, nh, dh = cache_k.shape
    ppb = MOBA_BLOCK // PAGE_SIZE
    pps = ppb
    while pps * 2 <= 8 and n_pages % (pps * 2) == 0:
        pps *= 2
    steps_per_seq = n_pages // pps
    nsteps = db * steps_per_seq
    gs = pltpu.PrefetchScalarGridSpec(
        num_scalar_prefetch=1, grid=(nsteps,),
        in_specs=[pl.BlockSpec(memory_space=pl.ANY)],
        out_specs=pl.BlockSpec((None, pps // ppb, nh, dh),
                               lambda g, pt: (g // steps_per_seq, g % steps_per_seq, 0, 0)),
        scratch_shapes=[pltpu.VMEM((2, pps, page, nh, dh), F32), pltpu.SemaphoreType.DMA((2,))])
    return pl.pallas_call(
        functools.partial(_page_blocksum_body, pps=pps, ppb=ppb, steps_per_seq=steps_per_seq, nsteps=nsteps,
                          layer=layer),
        grid_spec=gs,
        out_shape=jax.ShapeDtypeStruct((db, n_pages // ppb, nh, dh), F32),
        compiler_params=_cparams(("arbitrary",)),
        name="page_blocksum",
    )(page_table, cache_k)


def _decode_select_body(bs_ref, q_ref, sel_ref, *, nbp, q_blk, nh):
    bs = bs_ref[...]
    q = q_ref[...]
    it = lax.broadcasted_iota(jnp.int32, (nbp, 1), 0)
    for h in range(nh):
        sl = slice(h * ATT_HEAD_DIM, (h + 1) * ATT_HEAD_DIM)
        g = jnp.sum(q[:, sl] * (bs[:, sl] / MOBA_BLOCK), axis=1, keepdims=True)
        g = jnp.where(it < q_blk, g, -jnp.inf)
        for t in range(MOBA_TOP_K):
            mx = jnp.max(g, axis=0, keepdims=True)
            idx = jnp.min(jnp.where(g == mx, it, nbp), axis=0, keepdims=True)
            sel_ref[t:t + 1, h:h + 1] = jnp.where(mx > -jnp.inf, idx, -1)
            g = jnp.where(it == idx, -jnp.inf, g)


def _decode_select(blk_sum, q, q_blk):
    db, nbp, att_w = blk_sum.shape
    nh = att_w // ATT_HEAD_DIM
    return pl.pallas_call(
        functools.partial(_decode_select_body, nbp=nbp, q_blk=q_blk, nh=nh),
        grid=(db,),
        in_specs=[pl.BlockSpec((None, nbp, att_w), lambda b: (b, 0, 0)),
                  pl.BlockSpec((None, 1, att_w), lambda b: (b, 0, 0))],
        out_specs=pl.BlockSpec((None, MOBA_TOP_K, nh), lambda b: (b, 0, 0)),
        out_shape=jax.ShapeDtypeStruct((db, MOBA_TOP_K, nh), jnp.int32),
        compiler_params=_cparams(("parallel",)),
        name="decode_select",
    )(blk_sum, q.reshape(db, 1, att_w))


def _decode_attn_body(pt_ref, sel_ref, q_ref, kn_ref, vn_ref, ck_ref, cv_ref, o_ref, kbuf, vbuf, sem,
                      *, ppb, nh, nsteps, layer):
    g = pl.program_id(0)
    npg = MOBA_TOP_K * ppb

    def copies(step, slot):
        b, h = step // nh, step % nh
        cps = []
        for t in range(npg):
            blk = jnp.maximum(sel_ref[(b * MOBA_TOP_K + t // ppb) * nh + h], 0)
            page = pt_ref[b, blk * ppb + t % ppb]
            cps.append(pltpu.make_async_copy(ck_ref.at[layer, page, :, h, :], kbuf.at[slot, t], sem.at[0, slot]))
            cps.append(pltpu.make_async_copy(cv_ref.at[layer, page, :, h, :], vbuf.at[slot, t], sem.at[1, slot]))
        return cps

    slot = _ring_step(copies, g, nsteps)
    b, h = g // nh, g % nh
    q = q_ref[...]
    qh = q.astype(BF16)
    s_own = jnp.sum(q * kn_ref[...], axis=1, keepdims=True)
    m = s_own
    ss = []
    for t in range(npg):
        ok = sel_ref[(b * MOBA_TOP_K + t // ppb) * nh + h] >= 0
        st = jnp.where(ok, _dg(qh, kbuf[slot, t].astype(BF16), NT), -jnp.inf)
        ss.append(st)
        m = jnp.maximum(m, jnp.max(st, axis=1, keepdims=True))
    p_own = jnp.exp(s_own - m)
    l = p_own
    acc = p_own * vn_ref[...]
    for t in range(npg):
        p = jnp.exp(ss[t] - m)
        l = l + jnp.sum(p, axis=1, keepdims=True)
        acc = acc + _dg(p.astype(BF16), vbuf[slot, t].astype(BF16))
    o_ref[...] = (acc / l).astype(o_ref.dtype)


def _decode_attn(page_table, sel, q, k_new, v_new, cache_k, cache_v, layer):
    db, att_w = q.shape
    nh = att_w // ATT_HEAD_DIM
    ppb = MOBA_BLOCK // PAGE_SIZE
    npg = MOBA_TOP_K * ppb
    nsteps = db * nh
    row = pl.BlockSpec((None, 1, ATT_HEAD_DIM), lambda g, pt, sl: (g // nh, 0, g % nh))
    hbm = pl.BlockSpec(memory_space=pl.ANY)
    gs = pltpu.PrefetchScalarGridSpec(
        num_scalar_prefetch=2, grid=(nsteps,),
        in_specs=[row, row, row, hbm, hbm],
        out_specs=row,
        scratch_shapes=[pltpu.VMEM((2, npg, PAGE_SIZE, ATT_HEAD_DIM), F32),
                        pltpu.VMEM((2, npg, PAGE_SIZE, ATT_HEAD_DIM), F32),
                        pltpu.SemaphoreType.DMA((2, 2))])
    r3 = lambda a: a.reshape(db, 1, att_w)
    out = pl.pallas_call(
        functools.partial(_decode_attn_body, ppb=ppb, nh=nh, nsteps=nsteps, layer=layer), grid_spec=gs,
        out_shape=jax.ShapeDtypeStruct((db, 1, att_w), BF16),
        compiler_params=_cparams(("arbitrary",)),
        name="decode_attn",
    )(page_table, sel.reshape(-1), r3(q), r3(k_new), r3(v_new), cache_k, cache_v)
    return out.reshape(db, att_w)


def _seg_sum(x, e_bf16):
    outs = [_dot_exact_rhs(x[:, c:c + LANES], e_bf16) for c in range(0, x.shape[1], LANES)]
    return outs[0] if len(outs) == 1 else jnp.concatenate(outs, axis=1)


def _rwkv_features(rkv, low, prev_rkv, prev_low, par, single_pass):
    (mu_rkv, mu_low, w0, a0, k_k, k_a, r_k, wd_h, wd_l, wi_h, wi_l, wg_h, wg_l, e128) = par
    w = w0.shape[1]
    mix = rkv + mu_rkv * (prev_rkv - rkv)
    mlow = low + mu_low * (prev_low - low)
    r, k, v = mix[:, :w], mix[:, w:2 * w], mix[:, 2 * w:]
    lane = lax.broadcasted_iota(jnp.int32, mlow.shape, 1)
    feat = jnp.where(lane < DECAY_RANK, jnp.tanh(mlow),
                     jnp.where(lane < DECAY_RANK + ICLR_RANK, mlow,
                               jnp.where(lane < LOW_W, jax.nn.sigmoid(mlow), 0.0)))
    z = w0 + _dot3_pre(feat, wd_h, wd_l)
    nz = -z
    w_log = -(jnp.maximum(nz, 0.0) + jnp.log1p(jnp.exp(-jnp.abs(nz)))) - 0.5
    logw = -jnp.exp(w_log)
    if single_pass:
        fb = feat.astype(BF16)
        up_i, up_g = _dg(fb, wi_h), _dg(fb, wg_h)
    else:
        up_i, up_g = _dot3_pre(feat, wi_h, wi_l), _dot3_pre(feat, wg_h, wg_l)
    a = jax.nn.sigmoid(a0 + up_i)
    g = up_g
    kk = k * k_k
    kk = kk / jnp.maximum(jnp.sqrt(_seg_sum(kk * kk, e128)), L2_EPS)
    k2 = k * (1.0 + (a - 1.0) * k_a)
    bonus = _seg_sum(r * k2 * r_k, e128) * v
    return r, k2, v, kk, a, logw, g, bonus


_N_PAR = 14


def _rwkv_prep_body(*refs, tm, chunk, tiles_per_seq):
    rkv_ref, low_ref, prkv_ref, plow_ref, frkv_ref, flow_ref, ltri_ref = refs[:7]
    par = tuple(r[...] for r in refs[7:7 + _N_PAR])
    at_ref, bt_ref, kt_ref, rt_ref, v_ref, g_ref, bon_ref, pt_ref = refs[7 + _N_PAR:]
    i = pl.program_id(0)
    rkv, low = rkv_ref[...], low_ref[...]
    seq_start = i % tiles_per_seq == 0

    def prev_of(x, tail_ref, first_ref):
        first = jnp.where(seq_start, first_ref[...], tail_ref[7:8, :])
        rowi = lax.broadcasted_iota(jnp.int32, x.shape, 0)
        return jnp.where(rowi == 0, first, pltpu.roll(x, 1, 0))

    prev_rkv = prev_of(rkv, prkv_ref, frkv_ref)
    prev_low = prev_of(low, plow_ref, flow_ref)
    r, k2, v, kk, a, logw, g, bonus = _rwkv_features(rkv, low, prev_rkv, prev_low, par, True)
    cum = _dot_exact_lhs(ltri_ref[...], logw)
    e_in = jnp.exp(cum)
    e_out = jnp.exp(-cum)
    at_ref[...] = (-kk * jnp.exp(cum - logw)).astype(at_ref.dtype)
    bt_ref[...] = (kk * a * e_out).astype(bt_ref.dtype)
    kt_ref[...] = (k2 * e_out).astype(kt_ref.dtype)
    rt_ref[...] = (r * e_in).astype(rt_ref.dtype)
    v_ref[...] = v.astype(v_ref.dtype)
    g_ref[...] = g
    bon_ref[...] = bonus
    for c in range(tm // chunk):
        pt_ref[c] = e_in[(c + 1) * chunk - 1:(c + 1) * chunk, :]


def _wkv_chunk_body(at_ref, bt_ref, kt_ref, rt_ref, v_ref, pt_ref, g_ref, bon_ref, lnw_ref, lnb_ref, e_ref,
                    rw_ref, st_ref, s_scr, *, nc, chunk, npairs):
    T = chunk
    hp = LANES // RWKV_HEAD_DIM
    W = hp * T

    @pl.when(pl.program_id(1) == 0)
    def _():
        s_scr[...] = jnp.zeros_like(s_scr)

    lane_head = lax.broadcasted_iota(jnp.int32, (1, 1, LANES), 2) // RWKV_HEAD_DIM
    ri = lax.broadcasted_iota(jnp.int32, (1, W, W), 1)
    ci = lax.broadcasted_iota(jnp.int32, (1, W, W), 2)
    eye = jnp.where(ri == ci, 1.0, 0.0).astype(F32)
    zero = jnp.zeros((), BF16)
    e1 = functools.partial(jnp.einsum, preferred_element_type=F32)
    gram = functools.partial(e1, 'bid,bjd->bij')
    mm = functools.partial(e1, 'bij,bjd->bid')
    mm_nt = functools.partial(e1, 'bwk,bvk->bwv')
    bf = lambda t: t.astype(BF16)
    e128 = e_ref[...]
    lnw, lnb = lnw_ref[...], lnb_ref[...]

    def by_pair(x):
        return jnp.stack([x[:, p * LANES:(p + 1) * LANES] for p in range(npairs)], axis=0)

    def stacked(ref, rows):
        x = by_pair(ref[rows, :])
        return jnp.concatenate([jnp.where(lane_head == h, x, zero) for h in range(hp)], axis=1)

    def chunk_step(c, carry):
        rows = pl.ds(pl.multiple_of(c * T, T), T)
        a2, b2, k2, r2, v2 = (stacked(r, rows) for r in (at_ref, bt_ref, kt_ref, rt_ref, v_ref))
        bk = jnp.concatenate([b2, k2], axis=1)
        ga = gram(a2, bk)
        gr = gram(r2, bk)
        gab = jnp.where(ci < ri, ga[:, :, :W], 0.0)
        gak = bf(jnp.where(ci < ri, ga[:, :, W:], 0.0))
        grbk = bf(jnp.concatenate([jnp.where(ci <= ri, gr[:, :, :W], 0.0),
                                   jnp.where(ci <= ri, gr[:, :, W:], 0.0)], axis=2))
        x = eye + gab
        lp = gab
        span = 2
        while span < T:
            lp = mm(bf(lp), bf(lp))
            x = mm(bf(x), bf(eye + lp))
            span *= 2
        aw = jnp.concatenate([a2, bf(mm(gak, v2))], axis=2)
        xa = mm(bf(x), aw)
        ahat = bf(xa[:, :, :LANES])
        uhat = xa[:, :, LANES:]

        s = s_scr[...]
        sb = bf(s)
        u = mm_nt(ahat, sb) + uhat
        uv = jnp.concatenate([bf(u), v2], axis=1)
        y2 = mm_nt(r2, sb) + mm(grbk, uv)
        s_scr[...] = (s + e1('bwv,bwk->bvk', uv, bk)) * by_pair(pt_ref[c])

        y3 = y2[:, :T]
        for h in range(1, hp):
            y3 = y3 + y2[:, h * T:(h + 1) * T]
        y = jnp.concatenate([y3[p] for p in range(npairs)], axis=1)
        mu = _seg_sum(y, e128) / RWKV_HEAD_DIM
        yc = y - mu
        var = _seg_sum(yc * yc, e128) / RWKV_HEAD_DIM
        yn = yc * lax.rsqrt(var + GN_EPS) * lnw + lnb
        rw_ref[rows, :] = ((yn + bon_ref[rows, :]) * g_ref[rows, :]).astype(rw_ref.dtype)
        return carry

    lax.fori_loop(0, nc, chunk_step, 0)
    st_ref[...] = s_scr[...]


def _wkv_step_body(r_ref, k_ref, v_ref, kk_ref, a_ref, lw_ref, g_ref, bon_ref, lnw_ref, lnb_ref, s0_ref,
                   rw_ref, s1_ref, *, nh):
    d = s0_ref.shape[-1]
    eye = lax.broadcasted_iota(jnp.int32, (d, d), 0) == lax.broadcasted_iota(jnp.int32, (d, d), 1)
    for h in range(nh):
        s0 = s0_ref[h]
        r, k, v, kk, a = r_ref[h], k_ref[h], v_ref[h], kk_ref[h], a_ref[h]
        v_col = jnp.sum(jnp.where(eye, v, 0.0), axis=1, keepdims=True)
        sa = jnp.sum(s0 * (-kk), axis=1, keepdims=True)
        s1 = s0 * jnp.exp(lw_ref[h]) + sa * (kk * a) + v_col * k
        s1_ref[h] = s1
        y_col = jnp.sum(s1 * r, axis=1, keepdims=True)
        y = jnp.sum(jnp.where(eye, y_col, 0.0), axis=0, keepdims=True)
        mu = jnp.mean(y, axis=1, keepdims=True)
        var = jnp.mean(jnp.square(y - mu), axis=1, keepdims=True)
        yn = (y - mu) * lax.rsqrt(var + GN_EPS) * lnw_ref[h] + lnb_ref[h]
        rw_ref[h] = ((yn + bon_ref[h]) * g_ref[h]).astype(rw_ref.dtype)


def _rwkv_feat_body(*refs):
    rkv_ref, low_ref, prkv_ref, plow_ref = refs[:4]
    par = tuple(r[...] for r in refs[4:4 + _N_PAR])
    outs = refs[4 + _N_PAR:]
    vals = _rwkv_features(rkv_ref[...], low_ref[...], prkv_ref[...], plow_ref[...], par, False)
    for o, x in zip(outs, vals):
        o[...] = x


def _rwkv_params(p, rwkv_w):
    def pad_up(wt, r0):
        full = jnp.zeros((LOW_PAD, rwkv_w), F32).at[r0:r0 + wt.shape[0]].set(wt)
        hi = full.astype(BF16)
        return hi, (full - hi.astype(F32)).astype(BF16)

    wd_h, wd_l = pad_up(p['w_decay_up'], 0)
    wi_h, wi_l = pad_up(p['w_iclr_up'], DECAY_RANK)
    wg_h, wg_l = pad_up(p['w_gate_up'], DECAY_RANK + ICLR_RANK)
    hd = np.arange(LANES) // RWKV_HEAD_DIM
    e128 = jnp.asarray(hd[:, None] == hd[None, :], BF16)
    row = lambda t: t.reshape(1, -1).astype(F32)
    mu = p['mu_shift']
    mu_rkv = row(mu[:3 * rwkv_w])
    mu_low = row(jnp.pad(mu[3 * rwkv_w:], (0, LOW_PAD - LOW_W)))
    return (mu_rkv, mu_low, row(p['w0']), row(p['a0']), row(p['k_k']), row(p['k_a']), row(p['r_k']),
            wd_h, wd_l, wi_h, wi_l, wg_h, wg_l, e128)


def _full_spec(a):
    nd = a.ndim
    return pl.BlockSpec(a.shape, lambda *_: (0,) * nd)


def _rwkv_prompt(rkv, low, par, ln_w, ln_b, n, s):
    m = n * s
    w = rkv.shape[1] // 3
    tm = min(256, s)
    chunk = WKV_CHUNK
    cpt = tm // chunk
    hd = np.arange(tm) // chunk
    ltri = jnp.asarray((hd[:, None] == hd[None, :]) & (np.arange(tm)[:, None] >= np.arange(tm)[None, :]), BF16)
    zeros_rkv = jnp.zeros((n, 1, 3 * w), F32)
    zeros_low = jnp.zeros((n, 1, LOW_PAD), F32)
    tiles_per_seq = s // tm
    tail = lambda width: pl.BlockSpec((8, width), lambda i: (jnp.maximum(i * (tm // 8) - 1, 0), 0))
    first = lambda width: pl.BlockSpec((None, 1, width), lambda i: (i // tiles_per_seq, 0, 0))
    big = pl.BlockSpec((tm, w), lambda i: (i, 0))
    outs = pl.pallas_call(
        functools.partial(_rwkv_prep_body, tm=tm, chunk=chunk, tiles_per_seq=tiles_per_seq),
        grid=(m // tm,),
        in_specs=[pl.BlockSpec((tm, 3 * w), lambda i: (i, 0)), pl.BlockSpec((tm, LOW_PAD), lambda i: (i, 0)),
                  tail(3 * w), tail(LOW_PAD), first(3 * w), first(LOW_PAD), _full_spec(ltri),
                  *[_full_spec(t) for t in par]],
        out_specs=[big] * 7 + [pl.BlockSpec((cpt, 1, w), lambda i: (i, 0, 0))],
        out_shape=[jax.ShapeDtypeStruct((m, w), BF16)] * 5 + [jax.ShapeDtypeStruct((m, w), F32)] * 2
        + [jax.ShapeDtypeStruct((m // chunk, 1, w), F32)],
        compiler_params=_cparams(("parallel",)),
        name="rwkv_prep",
    )(rkv, low, rkv, low, zeros_rkv, zeros_low, ltri, *par)
    at, bt, kt, rt, v, g, bon, ptot = outs

    tb = _pick(s, 512)
    nc = tb // chunk
    npairs = w // LANES
    nblk = s // tb
    tok = pl.BlockSpec((tb, w), lambda b, i: (b * nblk + i, 0))
    vec = pl.BlockSpec((1, w), lambda b, i: (0, 0))
    rw, st = pl.pallas_call(
        functools.partial(_wkv_chunk_body, nc=nc, chunk=chunk, npairs=npairs),
        grid=(n, nblk),
        in_specs=[tok, tok, tok, tok, tok,
                  pl.BlockSpec((nc, 1, w), lambda b, i: (b * nblk + i, 0, 0)),
                  tok, tok, vec, vec, _full_spec(par[-1])],
        out_specs=[tok, pl.BlockSpec((None, npairs, LANES, LANES), lambda b, i: (b, 0, 0, 0))],
        out_shape=[jax.ShapeDtypeStruct((m, w), BF16), jax.ShapeDtypeStruct((n, npairs, LANES, LANES), F32)],
        scratch_shapes=[pltpu.VMEM((npairs, LANES, LANES), F32)],
        compiler_params=_cparams(("parallel", "arbitrary")),
        name="wkv_chunks",
    )(at, bt, kt, rt, v, ptot, g, bon, ln_w.reshape(1, w), ln_b.reshape(1, w), par[-1])
    hp = LANES // RWKV_HEAD_DIM
    d = RWKV_HEAD_DIM
    st = jnp.stack([st[:, :, j * d:(j + 1) * d, j * d:(j + 1) * d] for j in range(hp)], axis=2)
    return rw, st.reshape(n, npairs * hp, d, d)


def _rwkv_sample(rkv, low, prev_rkv, prev_low, state0, par, ln_w, ln_b):
    db = rkv.shape[0]
    w = rkv.shape[1] // 3
    d = RWKV_HEAD_DIM
    nh = w // d
    ins = (rkv, low, prev_rkv, prev_low, *par)
    feats = pl.pallas_call(
        _rwkv_feat_body,
        in_specs=[_full_spec(t) for t in ins],
        out_specs=[pl.BlockSpec((db, w), lambda: (0, 0))] * 8,
        out_shape=[jax.ShapeDtypeStruct((db, w), F32)] * 8,
        compiler_params=pltpu.CompilerParams(vmem_limit_bytes=VMEM_LIMIT),
        name="rwkv_feats",
    )(*ins)
    hv = lambda t: t.reshape(db, nh, 1, d)
    row = pl.BlockSpec((None, nh, 1, d), lambda b: (b, 0, 0, 0))
    prow = pl.BlockSpec((nh, 1, d), lambda b: (0, 0, 0))
    mat = pl.BlockSpec((None, nh, d, d), lambda b: (b, 0, 0, 0))
    rw, s1 = pl.pallas_call(
        functools.partial(_wkv_step_body, nh=nh),
        grid=(db,),
        in_specs=[row] * 8 + [prow, prow, mat],
        out_specs=[row, mat],
        out_shape=[jax.ShapeDtypeStruct((db, nh, 1, d), BF16), jax.ShapeDtypeStruct((db, nh, d, d), F32)],
        compiler_params=_cparams(("parallel",)),
        name="wkv_step",
    )(*[hv(t) for t in feats], ln_w.reshape(nh, 1, d), ln_b.reshape(nh, 1, d), state0)
    return rw.reshape(db, w), s1


def _merge_body(attn_ref, rw_ref, ga_ref, gb_ref, wa_ref, wr_ref, o_ref):
    ya = _dg(attn_ref[...], wa_ref[...])
    yb = _dg(rw_ref[...], wr_ref[...])
    o_ref[...] = (ga_ref[...] * ya + gb_ref[...] * yb).astype(o_ref.dtype)


def _merge(attn, rw, gates, wa, wr, tm, tn):
    m, ka = attn.shape
    kr = rw.shape[1]
    d = wa.shape[1]
    nj = d // tn
    return pl.pallas_call(
        _merge_body,
        grid=(m // tm, nj),
        in_specs=[pl.BlockSpec((tm, ka), lambda i, j: (i, 0)), pl.BlockSpec((tm, kr), lambda i, j: (i, 0)),
                  pl.BlockSpec((tm, tn), lambda i, j: (i, j)), pl.BlockSpec((tm, tn), lambda i, j: (i, nj + j)),
                  pl.BlockSpec((ka, tn), lambda i, j: (0, j)), pl.BlockSpec((kr, tn), lambda i, j: (0, j))],
        out_specs=pl.BlockSpec((tm, tn), lambda i, j: (i, j)),
        out_shape=jax.ShapeDtypeStruct((m, d), BF16),
        compiler_params=_cparams(("parallel", "parallel")),
        name="merge",
    )(attn, rw, gates, gates, wa, wr)


def _resid_mm_body(a_ref, w_ref, x_ref, o_ref):
    o_ref[...] = x_ref[...] + _dg(a_ref[...], w_ref[...])


def _resid_matmul(a, w, x, tm, tn):
    m, k = a.shape
    n = w.shape[1]
    return pl.pallas_call(
        _resid_mm_body,
        grid=(m // tm, n // tn),
        in_specs=[pl.BlockSpec((tm, k), lambda i, j: (i, 0)), pl.BlockSpec((k, tn), lambda i, j: (0, j)),
                  pl.BlockSpec((tm, tn), lambda i, j: (i, j))],
        out_specs=pl.BlockSpec((tm, tn), lambda i, j: (i, j)),
        out_shape=jax.ShapeDtypeStruct((m, n), F32),
        compiler_params=_cparams(("parallel", "parallel")),
        name="out_proj",
    )(a, w, x)


def _ffn_up_body(x_ref, g_ref, wg_ref, wu_ref, o_ref, xn_ref):
    @pl.when(pl.program_id(1) == 0)
    def _():
        xn_ref[...] = _rms(x_ref[...], g_ref[...]).astype(BF16)

    xn = xn_ref[...]
    o_ref[...] = (jax.nn.silu(_dg(xn, wg_ref[...])) * _dg(xn, wu_ref[...])).astype(o_ref.dtype)


def _ffn_up(x, g, wg, wu, tm, tn):
    m, d = x.shape
    f = wg.shape[1]
    wspec = pl.BlockSpec((d, tn), lambda i, j: (0, j))
    return pl.pallas_call(
        _ffn_up_body,
        grid=(m // tm, f // tn),
        in_specs=[pl.BlockSpec((tm, d), lambda i, j: (i, 0)), pl.BlockSpec((1, d), lambda i, j: (0, 0)), wspec, wspec],
        out_specs=pl.BlockSpec((tm, tn), lambda i, j: (i, j)),
        out_shape=jax.ShapeDtypeStruct((m, f), BF16),
        scratch_shapes=[pltpu.VMEM((tm, d), BF16)],
        compiler_params=_cparams(("parallel", "arbitrary")),
        name="ffn_up",
    )(x, g, wg, wu)


def _ffn_down_body(h_ref, w_ref, x_ref, g_ref, o_ref, *, nj, tn):
    j = pl.program_id(1)
    val = x_ref[...] + _dg(h_ref[...], w_ref[...])
    for t in range(nj):
        @pl.when(j == t)
        def _(t=t):
            o_ref[:, t * tn:(t + 1) * tn] = val

    @pl.when(j == nj - 1)
    def _():
        o_ref[...] = _rms(o_ref[...], g_ref[...])


def _ffn_down(h, w, x, g, tm, tn):
    m, f = h.shape
    d = w.shape[1]
    nj = d // tn
    return pl.pallas_call(
        functools.partial(_ffn_down_body, nj=nj, tn=tn),
        grid=(m // tm, nj),
        in_specs=[pl.BlockSpec((tm, f), lambda i, j: (i, 0)), pl.BlockSpec((f, tn), lambda i, j: (0, j)),
                  pl.BlockSpec((tm, tn), lambda i, j: (i, j)), pl.BlockSpec((1, d), lambda i, j: (0, 0))],
        out_specs=pl.BlockSpec((tm, d), lambda i, j: (i, 0)),
        out_shape=jax.ShapeDtypeStruct((m, d), F32),
        compiler_params=_cparams(("parallel", "arbitrary")),
        name="ffn_down",
    )(h, w, x, g)


def _pick(total, pref):
    t = min(pref, total)
    while total % t:
        t //= 2
    return t


def _layer(x2, n, s, pos, wts, par, p, attend, rwkv):
    m, d = x2.shape
    att_w = wts['qkv'].shape[1] // 3
    rw_w = wts['rkv'].shape[1] // 3
    tm = _pick(s, 1024) if s % 8 == 0 else _pick(m, 1024)
    assert s % tm == 0 or tm % s == 0
    g_mix = p['g_mix'].reshape(1, d)

    cos_t, sin_a, sin_b = _rope_tables(pos)
    per = max(s // tm, 1)
    if s < tm:
        cos_t, sin_a, sin_b = (jnp.tile(t, (tm // s, 1)) for t in (cos_t, sin_a, sin_b))
    tab = pl.BlockSpec((tm, ATT_HEAD_DIM), lambda i, j: (i % per, 0))
    qkv = _norm_matmul(x2, g_mix, wts['qkv'], tm, att_w, 'proj_qkv', _epi_rope, (cos_t, sin_a, sin_b), (tab, tab, tab))
    rkv = _norm_matmul(x2, g_mix, wts['rkv'], tm, rw_w, 'proj_rkv')
    low = _norm_matmul(x2, g_mix, wts['low'], tm, LOW_PAD, 'proj_low')
    tn_g = _pick(2 * d, 1024)
    gates = _norm_matmul(x2, g_mix, wts['gate'], tm, tn_g, 'proj_gates', _epi_sigmoid_bias, (p['b_gate'].reshape(1, 2 * d),),
                         (pl.BlockSpec((1, tn_g), lambda i, j: (0, j)),), out_dtype=BF16)

    attn = attend(qkv)
    rw, wkv_new = rwkv(rkv, low)

    merged = _merge(attn, rw, gates, wts['proj_attn'], wts['proj_rwkv'], tm, _pick(d, 1024))
    hid = _resid_matmul(merged, wts['out'], x2, tm, _pick(d, 1024))
    f = wts['ffn_gate'].shape[1]
    h = _ffn_up(hid, p['g_ffn'].reshape(1, d), wts['ffn_gate'], wts['ffn_up'], tm, _pick(f, 512))
    y = _ffn_down(h, wts['ffn_down'], hid, p['g_final'].reshape(1, d), _pick(m, 512), _pick(d, 512))

    k = qkv[:, att_w:2 * att_w]
    v = qkv[:, 2 * att_w:]
    sh_last = jnp.concatenate([rkv.reshape(n, s, -1)[:, -1], low.reshape(n, s, -1)[:, -1, :LOW_W]], axis=1)
    return y, k, v, wkv_new, sh_last


def kernel(x_prompt, x_sample, cache_k, cache_v, state_wkv, state_shift, page_table, g_mix, w_in, b_gate, mu_shift, w0, w_decay_up, a0, w_iclr_up, w_gate_up, k_k, k_a, r_k, ln_x_w, ln_x_b, w_proj_attn, w_proj_rwkv, w_out, g_ffn, w_ffn_gate, w_ffn_up, w_ffn_down, g_final):
    depth = w_in.shape[0]
    assert depth == 1, "single-layer trunk"
    n, s, d = x_prompt.shape
    db, ds, _ = x_sample.shape
    assert ds == 1, "one new token per decode sequence"
    _, n_pool, page, nh, dh = cache_k.shape
    assert page == PAGE_SIZE and dh == ATT_HEAD_DIM
    att_w = nh * dh
    rw_w = w0.shape[1]
    n_pages = page_table.shape[1]
    past = n_pages * PAGE_SIZE
    assert s % MOBA_BLOCK == 0 and past % MOBA_BLOCK == 0
    l = 0

    wi = w_in[l]
    o = 3 * att_w
    wts = {
        'qkv': wi[:, :o].astype(BF16),
        'rkv': wi[:, o:o + 3 * rw_w].astype(BF16),
        'low': jnp.pad(wi[:, o + 3 * rw_w:o + 3 * rw_w + LOW_W], ((0, 0), (0, LOW_PAD - LOW_W))).astype(BF16),
        'gate': wi[:, o + 3 * rw_w + LOW_W:].astype(BF16),
        'proj_attn': w_proj_attn[l].astype(BF16), 'proj_rwkv': w_proj_rwkv[l].astype(BF16),
        'out': w_out[l].astype(BF16), 'ffn_gate': w_ffn_gate[l].astype(BF16),
        'ffn_up': w_ffn_up[l].astype(BF16), 'ffn_down': w_ffn_down[l].astype(BF16),
    }
    p = {'g_mix': g_mix[l], 'b_gate': b_gate[l], 'g_ffn': g_ffn[l], 'g_final': g_final,
         'mu_shift': mu_shift[l], 'w0': w0[l], 'w_decay_up': w_decay_up[l], 'a0': a0[l],
         'w_iclr_up': w_iclr_up[l], 'w_gate_up': w_gate_up[l], 'k_k': k_k[l], 'k_a': k_a[l], 'r_k': r_k[l]}
    par = _rwkv_params(p, rw_w)
    ln_w, ln_b = ln_x_w[l], ln_x_b[l]

    yp, kp, vp, wp, sp = _layer(
        x_prompt.reshape(n * s, d), n, s, jnp.arange(s), wts, par, p,
        lambda qkv: _moba_prompt(qkv, n, s),
        lambda rkv, low: _rwkv_prompt(rkv, low, par, ln_w, ln_b, n, s))

    q_blk = past // MOBA_BLOCK

    def attend_sample(qkv):
        q, k_new, v_new = qkv[:, :att_w], qkv[:, att_w:2 * att_w], qkv[:, 2 * att_w:]
        blk_sum = _page_blocksum(cache_k, page_table, l).reshape(db, q_blk, att_w)
        sel = _decode_select(blk_sum, q, q_blk)
        return _decode_attn(page_table, sel, q, k_new, v_new, cache_k, cache_v, l)

    sh_prev = state_shift[l]
    prev_rkv = sh_prev[:, :3 * rw_w]
    prev_low = jnp.pad(sh_prev[:, 3 * rw_w:], ((0, 0), (0, LOW_PAD - LOW_W)))
    ys, ks_, vs_, ws_, ss_ = _layer(
        x_sample.reshape(db * ds, d), db, ds, past + jnp.arange(ds), wts, par, p,
        attend_sample,
        lambda rkv, low: _rwkv_sample(rkv, low, prev_rkv, prev_low, state_wkv[l], par, ln_w, ln_b))

    return (yp.reshape(n, s, d), ys.reshape(db, ds, d),
            kp.reshape(1, n, s, nh, dh), vp.reshape(1, n, s, nh, dh),
            wp.astype(state_wkv.dtype)[None], sp.astype(state_shift.dtype)[None],
            ks_.reshape(1, db, ds, nh, dh), vs_.reshape(1, db, ds, nh, dh),
            ws_.astype(state_wkv.dtype)[None], ss_.astype(state_shift.dtype)[None])
```

```python
import functools
import math
from typing import NamedTuple

import numpy as np
import jax
import jax.numpy as jnp
from jax import lax
from jax.experimental import pallas as pl
from jax.experimental.pallas import tpu as pltpu

F32 = jnp.float32
BF16 = jnp.bfloat16

ATT_HEAD_DIM = 128
ROT_DIM = ATT_HEAD_DIM // 4
ROPE_THETA = 500000.0
MOBA_BLOCK = 256
MOBA_TOP_K = 3
PAGE_SIZE = 128
RWKV_HEAD_DIM = 64
DECAY_RANK = 64
ICLR_RANK = 64
GATE_RANK = 160
RMS_EPS = 1e-6
GN_EPS = 64e-5
L2_EPS = 1e-12

LANES = 128
LOW_W = DECAY_RANK + ICLR_RANK + GATE_RANK
LOW_PAD = -(-LOW_W // LANES) * LANES
WKV_CHUNK = 64
VMEM_LIMIT = 56 * 1024 * 1024

NN = (((1,), (0,)), ((), ()))
NT = (((1,), (1,)), ((), ()))
TN = (((0,), (0,)), ((), ()))


def _cparams(sem):
    return pltpu.CompilerParams(dimension_semantics=sem, vmem_limit_bytes=VMEM_LIMIT)


def _dg(a, b, dims=NN):
    return lax.dot_general(a, b, dims, preferred_element_type=F32)


def _split2(x):
    hi = x.astype(BF16)
    lo = (x - hi.astype(F32)).astype(BF16)
    return hi, lo


def _dot3(a, b, dims=NN):
    ah, al = _split2(a)
    bh, bl = _split2(b)
    return _dg(ah, bh, dims) + (_dg(ah, bl, dims) + _dg(al, bh, dims))


def _dot3_pre(a, bh, bl, dims=NN):
    ah, al = _split2(a)
    return _dg(ah, bh, dims) + (_dg(ah, bl, dims) + _dg(al, bh, dims))


def _dot_exact_rhs(a, b_bf16, dims=NN):
    hi, lo = _split2(a)
    return _dg(hi, b_bf16, dims) + _dg(lo, b_bf16, dims)


def _dot_exact_lhs(a_bf16, b, dims=NN):
    hi, lo = _split2(b)
    return _dg(a_bf16, hi, dims) + _dg(a_bf16, lo, dims)


def _rms(x, g):
    return x * lax.rsqrt(jnp.mean(x * x, axis=-1, keepdims=True) + RMS_EPS) * g


def _norm_mm_body(*refs, epilogue, n_aux):
    x_ref, g_ref, w_ref = refs[:3]
    aux = refs[3:3 + n_aux]
    o_ref, xn_ref = refs[3 + n_aux:]
    j = pl.program_id(1)

    @pl.when(j == 0)
    def _():
        xn_ref[...] = _rms(x_ref[...], g_ref[...]).astype(BF16)

    acc = _dg(xn_ref[...], w_ref[...])
    o_ref[...] = epilogue(acc, j, *aux).astype(o_ref.dtype)


def _mm_body(*refs, epilogue, n_aux):
    a_ref, w_ref = refs[:2]
    aux = refs[2:2 + n_aux]
    o_ref, = refs[2 + n_aux:]
    o_ref[...] = epilogue(_dg(a_ref[...], w_ref[...]), pl.program_id(1), *aux).astype(o_ref.dtype)


def _epi_none(acc, j):
    return acc


def _epi_rope(acc, j, cos_ref, sa_ref, sb_ref):
    c, sa, sb = cos_ref[...], sa_ref[...], sb_ref[...]
    half = ROT_DIM // 2
    outs = []
    for h in range(acc.shape[1] // ATT_HEAD_DIM):
        xh = acc[:, h * ATT_HEAD_DIM:(h + 1) * ATT_HEAD_DIM]
        outs.append(xh * c + pltpu.roll(xh, ATT_HEAD_DIM - half, 1) * sa + pltpu.roll(xh, half, 1) * sb)
    rot = jnp.concatenate(outs, axis=1)
    rot = rot * jnp.where(j == 0, ATT_HEAD_DIM ** -0.5, 1.0).astype(F32)
    return jnp.where(j < 2, rot, acc)


def _epi_sigmoid_bias(acc, j, b_ref):
    return jax.nn.sigmoid(acc + b_ref[...])


def _norm_matmul(x, g, w, tm, tn, name, epilogue=_epi_none, aux=(), aux_specs=(), out_dtype=F32):
    m, d = x.shape
    n = w.shape[1]
    body = functools.partial(_norm_mm_body, epilogue=epilogue, n_aux=len(aux))
    return pl.pallas_call(
        body,
        grid=(m // tm, n // tn),
        in_specs=[pl.BlockSpec((tm, d), lambda i, j: (i, 0)),
                  pl.BlockSpec((1, d), lambda i, j: (0, 0)),
                  pl.BlockSpec((d, tn), lambda i, j: (0, j)), *aux_specs],
        out_specs=[pl.BlockSpec((tm, tn), lambda i, j: (i, j)), pl.BlockSpec((tm, d), lambda i, j: (i, 0))],
        out_shape=[jax.ShapeDtypeStruct((m, n), out_dtype), jax.ShapeDtypeStruct((m, d), BF16)],
        compiler_params=_cparams(("parallel", "arbitrary")),
        name=name,
    )(x, g, w, *aux)


def _matmul(a, w, tm, tn, name, epilogue=_epi_none, aux=(), aux_specs=(), out_dtype=F32):
    m, k = a.shape
    n = w.shape[1]
    return pl.pallas_call(
        functools.partial(_mm_body, epilogue=epilogue, n_aux=len(aux)),
        grid=(m // tm, n // tn),
        in_specs=[pl.BlockSpec((tm, k), lambda i, j: (i, 0)), pl.BlockSpec((k, tn), lambda i, j: (0, j)), *aux_specs],
        out_specs=pl.BlockSpec((tm, tn), lambda i, j: (i, j)),
        out_shape=jax.ShapeDtypeStruct((m, n), out_dtype),
        compiler_params=_cparams(("parallel", "parallel")),
        name=name,
    )(a, w, *aux)


def _rope_tables(pos):
    half = ROT_DIM // 2
    inv = jnp.exp(jnp.arange(half, dtype=F32) * (-2.0 * math.log(ROPE_THETA) / ROT_DIM))
    ang = pos.astype(F32)[:, None] * inv[None, :]
    cos, sin = jnp.cos(ang), jnp.sin(ang)
    s = pos.shape[0]
    z = lambda w: jnp.zeros((s, w), F32)
    cos_t = jnp.concatenate([cos, cos, jnp.ones((s, ATT_HEAD_DIM - ROT_DIM), F32)], axis=1)
    sin_a = jnp.concatenate([-sin, z(ATT_HEAD_DIM - half)], axis=1)
    sin_b = jnp.concatenate([z(half), sin, z(ATT_HEAD_DIM - ROT_DIM)], axis=1)
    return cos_t, sin_a, sin_b


def _moba_prompt_body(pt_ref, q_ref, k_ref, v_ref, ck_ref, o_ref, bs_ref, km_ref, kb_ref, vt_ref, s_ref,
                      pbuf, psum, psem, osem, *, nb, nh, pg):
    qb = pl.program_id(2)
    blk = MOBA_BLOCK
    sub = 8
    g = (pl.program_id(0) * nh + pl.program_id(1)) * nb + qb
    slot = g % 2

    def page_copies(step, slot_):
        b = step // pg.per_seq
        p0 = (step % pg.per_seq) * pg.pps
        return [pltpu.make_async_copy(ck_ref.at[pg.layer, pt_ref[b, p0 + j]], pbuf.at[slot_, j], psem.at[slot_])
                for j in range(pg.pps)]

    @pl.when(g == 0)
    def _():
        for c in page_copies(0, 0):
            c.start()

    @pl.when(g + 1 < pg.groups)
    def _():
        for c in page_copies(g + 1, 1 - slot):
            c.start()

    @pl.when(qb == 0)
    def _():
        km_ref[...] = jnp.zeros_like(km_ref)
        for j in range(nb):
            kj = k_ref[j * blk:(j + 1) * blk, :]
            km_ref[j:j + 1, :] = jnp.mean(kj, axis=0, keepdims=True)
            kb_ref[j] = kj.astype(BF16)
            vt_ref[j] = v_ref[j * blk:(j + 1) * blk, :].T.astype(BF16)

    q = q_ref[...]
    gate = _dot3(km_ref[...], q, NT)
    rowi = lax.broadcasted_iota(jnp.int32, gate.shape, 0)
    valid = rowi < qb
    gm = jnp.where(valid, gate, -jnp.inf)
    cnt = jnp.zeros(gate.shape, F32)
    for i in range(nb):
        gi = gm[i:i + 1, :]
        cnt = cnt + jnp.where(gi > gm, 1.0, 0.0) + jnp.where(gi == gm, (rowi > i).astype(F32), 0.0)
    sel = jnp.where(valid, jnp.where(cnt < MOBA_TOP_K, 1.0, 0.0), 0.0)

    qh = q.astype(BF16)
    keyi = lax.broadcasted_iota(jnp.int32, (blk, blk), 0)
    qi = lax.broadcasted_iota(jnp.int32, (blk, blk), 1)
    fold = lambda t, op: op(t.reshape(blk // sub, sub, blk), axis=0)

    def sweep(qbv):
        pm = None
        for j in range(qbv + 1):
            s = _dg(kb_ref[j], qh, NT)
            s = jnp.where(keyi <= qi if j == qbv else sel[j:j + 1, :] > 0.0, s, -jnp.inf)
            s_ref[j] = s
            t = fold(s, jnp.max)
            pm = t if pm is None else jnp.maximum(pm, t)
        m = jnp.max(pm, axis=0, keepdims=True)
        lp = jnp.zeros((sub, blk), F32)
        acc = jnp.zeros((ATT_HEAD_DIM, blk), F32)
        for j in range(qbv + 1):
            p = jnp.exp(s_ref[j] - m)
            lp = lp + fold(p, jnp.sum)
            acc = acc + _dg(vt_ref[j], p.astype(BF16))
        l = jnp.sum(lp, axis=0, keepdims=True)
        o_ref[...] = (acc / l).T.astype(o_ref.dtype)

    for qbv in range(nb):
        pl.when(qb == qbv)(functools.partial(sweep, qbv))

    @pl.when(g < pg.groups)
    def _():
        for c in page_copies(g, slot):
            c.wait()
        for j in range(pg.pps // pg.ppb):
            acc = jnp.sum(pbuf[slot, j * pg.ppb], axis=0)
            for t in range(1, pg.ppb):
                acc = acc + jnp.sum(pbuf[slot, j * pg.ppb + t], axis=0)
            psum[j] = acc
        nbs = pg.pps // pg.ppb
        out = pltpu.make_async_copy(psum, bs_ref.at[g // pg.per_seq, pl.ds((g % pg.per_seq) * nbs, nbs)], osem)
        out.start()
        out.wait()


class _PageGroups(NamedTuple):
    pps: int
    ppb: int
    per_seq: int
    groups: int
    layer: int


def _moba_prompt(qkv, n, s, page_table, cache_k, layer):
    att_w = qkv.shape[1] // 3
    nh = att_w // ATT_HEAD_DIM
    nb = s // MOBA_BLOCK
    nbp = -(-nb // 8) * 8
    db, n_pages = page_table.shape
    _, _, page, ch, dh = cache_k.shape
    ppb = MOBA_BLOCK // PAGE_SIZE
    nsteps = n * nh * nb
    pps = ppb
    while (db * n_pages // pps > nsteps or pps * 2 <= 8) and n_pages % (pps * 2) == 0:
        pps *= 2
    pg = _PageGroups(pps, ppb, n_pages // pps, db * n_pages // pps, layer)
    assert pg.groups <= nsteps, "more page groups than attention grid steps"
    gs = pltpu.PrefetchScalarGridSpec(
        num_scalar_prefetch=1, grid=(n, nh, nb),
        in_specs=[pl.BlockSpec((MOBA_BLOCK, ATT_HEAD_DIM), lambda b, h, i, pt: (b * nb + i, h)),
                  pl.BlockSpec((s, ATT_HEAD_DIM), lambda b, h, i, pt: (b, nh + h)),
                  pl.BlockSpec((s, ATT_HEAD_DIM), lambda b, h, i, pt: (b, 2 * nh + h)),
                  pl.BlockSpec(memory_space=pl.ANY)],
        out_specs=[pl.BlockSpec((MOBA_BLOCK, ATT_HEAD_DIM), lambda b, h, i, pt: (b * nb + i, h)),
                   pl.BlockSpec(memory_space=pl.ANY)],
        scratch_shapes=[pltpu.VMEM((nbp, ATT_HEAD_DIM), F32),
                        pltpu.VMEM((nb, MOBA_BLOCK, ATT_HEAD_DIM), BF16),
                        pltpu.VMEM((nb, ATT_HEAD_DIM, MOBA_BLOCK), BF16),
                        pltpu.VMEM((nb, MOBA_BLOCK, MOBA_BLOCK), F32),
                        pltpu.VMEM((2, pps, page, ch, dh), F32),
                        pltpu.VMEM((pps // ppb, ch, dh), F32),
                        pltpu.SemaphoreType.DMA((2,)), pltpu.SemaphoreType.DMA(())])
    return pl.pallas_call(
        functools.partial(_moba_prompt_body, nb=nb, nh=nh, pg=pg),
        grid_spec=gs,
        out_shape=[jax.ShapeDtypeStruct((n * s, att_w), BF16),
                   jax.ShapeDtypeStruct((db, n_pages // ppb, ch, dh), F32)],
        compiler_params=_cparams(("arbitrary", "arbitrary", "arbitrary")),
        name="moba_prompt",
    )(page_table, qkv, qkv, qkv, cache_k)


def _ring_step(copies, g, nsteps):
    slot = g % 2

    @pl.when(g == 0)
    def _():
        for c in copies(0, 0):
            c.start()

    @pl.when(g + 1 < nsteps)
    def _():
        for c in copies(g + 1, 1 - slot):
            c.start()

    for c in copies(g, slot):
        c.wait()
    return slot


def _decode_select_body(bs_ref, q_ref, sel_ref, *, nbp, q_blk, nh):
    bs = bs_ref[...]
    q = q_ref[...]
    it = lax.broadcasted_iota(jnp.int32, (nbp, 1), 0)
    for h in range(nh):
        sl = slice(h * ATT_HEAD_DIM, (h + 1) * ATT_HEAD_DIM)
        g = jnp.sum(q[:, sl] * (bs[:, sl] / MOBA_BLOCK), axis=1, keepdims=True)
        g = jnp.where(it < q_blk, g, -jnp.inf)
        for t in range(MOBA_TOP_K):
            mx = jnp.max(g, axis=0, keepdims=True)
            idx = jnp.min(jnp.where(g == mx, it, nbp), axis=0, keepdims=True)
            sel_ref[t:t + 1, h:h + 1] = jnp.where(mx > -jnp.inf, idx, -1)
            g = jnp.where(it == idx, -jnp.inf, g)


def _decode_select(blk_sum, q, q_blk):
    db, nbp, att_w = blk_sum.shape
    nh = att_w // ATT_HEAD_DIM
    return pl.pallas_call(
        functools.partial(_decode_select_body, nbp=nbp, q_blk=q_blk, nh=nh),
        grid=(db,),
        in_specs=[pl.BlockSpec((None, nbp, att_w), lambda b: (b, 0, 0)),
                  pl.BlockSpec((None, 1, att_w), lambda b: (b, 0, 0))],
        out_specs=pl.BlockSpec((None, MOBA_TOP_K, nh), lambda b: (b, 0, 0)),
        out_shape=jax.ShapeDtypeStruct((db, MOBA_TOP_K, nh), jnp.int32),
        compiler_params=_cparams(("parallel",)),
        name="decode_select",
    )(blk_sum, q.reshape(db, 1, att_w))


def _decode_attn_body(pt_ref, sel_ref, q_ref, kn_ref, vn_ref, ck_ref, cv_ref, o_ref, kbuf, vbuf, sem,
                      *, ppb, nh, nsteps, layer):
    g = pl.program_id(0)
    npg = MOBA_TOP_K * ppb

    def copies(step, slot):
        b, h = step // nh, step % nh
        cps = []
        for t in range(npg):
            blk = jnp.maximum(sel_ref[(b * MOBA_TOP_K + t // ppb) * nh + h], 0)
            page = pt_ref[b, blk * ppb + t % ppb]
            cps.append(pltpu.make_async_copy(ck_ref.at[layer, page, :, h, :], kbuf.at[slot, t], sem.at[0, slot]))
            cps.append(pltpu.make_async_copy(cv_ref.at[layer, page, :, h, :], vbuf.at[slot, t], sem.at[1, slot]))
        return cps

    slot = _ring_step(copies, g, nsteps)
    b, h = g // nh, g % nh
    q = q_ref[...]
    qh = q.astype(BF16)
    s_own = jnp.sum(q * kn_ref[...], axis=1, keepdims=True)
    m = s_own
    ss = []
    for t in range(npg):
        ok = sel_ref[(b * MOBA_TOP_K + t // ppb) * nh + h] >= 0
        st = jnp.where(ok, _dg(qh, kbuf[slot, t].astype(BF16), NT), -jnp.inf)
        ss.append(st)
        m = jnp.maximum(m, jnp.max(st, axis=1, keepdims=True))
    p_own = jnp.exp(s_own - m)
    l = p_own
    acc = p_own * vn_ref[...]
    for t in range(npg):
        p = jnp.exp(ss[t] - m)
        l = l + jnp.sum(p, axis=1, keepdims=True)
        acc = acc + _dg(p.astype(BF16), vbuf[slot, t].astype(BF16))
    o_ref[...] = (acc / l).astype(o_ref.dtype)


def _decode_attn(page_table, sel, q, k_new, v_new, cache_k, cache_v, layer):
    db, att_w = q.shape
    nh = att_w // ATT_HEAD_DIM
    ppb = MOBA_BLOCK // PAGE_SIZE
    npg = MOBA_TOP_K * ppb
    nsteps = db * nh
    row = pl.BlockSpec((None, 1, ATT_HEAD_DIM), lambda g, pt, sl: (g // nh, 0, g % nh))
    hbm = pl.BlockSpec(memory_space=pl.ANY)
    gs = pltpu.PrefetchScalarGridSpec(
        num_scalar_prefetch=2, grid=(nsteps,),
        in_specs=[row, row, row, hbm, hbm],
        out_specs=row,
        scratch_shapes=[pltpu.VMEM((2, npg, PAGE_SIZE, ATT_HEAD_DIM), F32),
                        pltpu.VMEM((2, npg, PAGE_SIZE, ATT_HEAD_DIM), F32),
                        pltpu.SemaphoreType.DMA((2, 2))])
    r3 = lambda a: a.reshape(db, 1, att_w)
    out = pl.pallas_call(
        functools.partial(_decode_attn_body, ppb=ppb, nh=nh, nsteps=nsteps, layer=layer), grid_spec=gs,
        out_shape=jax.ShapeDtypeStruct((db, 1, att_w), BF16),
        compiler_params=_cparams(("arbitrary",)),
        name="decode_attn",
    )(page_table, sel.reshape(-1), r3(q), r3(k_new), r3(v_new), cache_k, cache_v)
    return out.reshape(db, att_w)


def _seg_sum(x, e_bf16):
    outs = [_dot_exact_rhs(x[:, c:c + LANES], e_bf16) for c in range(0, x.shape[1], LANES)]
    return outs[0] if len(outs) == 1 else jnp.concatenate(outs, axis=1)


def _rwkv_features(rkv, low, prev_rkv, prev_low, par, single_pass):
    (mu_rkv, mu_low, w0, a0, k_k, k_a, r_k, wd_h, wd_l, wi_h, wi_l, wg_h, wg_l, e128) = par
    w = w0.shape[1]
    mix = rkv + mu_rkv * (prev_rkv - rkv)
    mlow = low + mu_low * (prev_low - low)
    r, k, v = mix[:, :w], mix[:, w:2 * w], mix[:, 2 * w:]
    lane = lax.broadcasted_iota(jnp.int32, mlow.shape, 1)
    feat = jnp.where(lane < DECAY_RANK, jnp.tanh(mlow),
                     jnp.where(lane < DECAY_RANK + ICLR_RANK, mlow,
                               jnp.where(lane < LOW_W, jax.nn.sigmoid(mlow), 0.0)))
    z = w0 + _dot3_pre(feat, wd_h, wd_l)
    nz = -z
    w_log = -(jnp.maximum(nz, 0.0) + jnp.log1p(jnp.exp(-jnp.abs(nz)))) - 0.5
    logw = -jnp.exp(w_log)
    if single_pass:
        fb = feat.astype(BF16)
        up_i, up_g = _dg(fb, wi_h), _dg(fb, wg_h)
    else:
        up_i, up_g = _dot3_pre(feat, wi_h, wi_l), _dot3_pre(feat, wg_h, wg_l)
    a = jax.nn.sigmoid(a0 + up_i)
    g = up_g
    kk = k * k_k
    kk = kk / jnp.maximum(jnp.sqrt(_seg_sum(kk * kk, e128)), L2_EPS)
    k2 = k * (1.0 + (a - 1.0) * k_a)
    bonus = _seg_sum(r * k2 * r_k, e128) * v
    return r, k2, v, kk, a, logw, g, bonus


_N_PAR = 14


def _rwkv_prep_body(*refs, tm, chunk, tiles_per_seq):
    rkv_ref, low_ref, prkv_ref, plow_ref, frkv_ref, flow_ref, ltri_ref = refs[:7]
    par = tuple(r[...] for r in refs[7:7 + _N_PAR])
    at_ref, bt_ref, kt_ref, rt_ref, v_ref, g_ref, bon_ref, pt_ref = refs[7 + _N_PAR:]
    i = pl.program_id(0)
    rkv, low = rkv_ref[...], low_ref[...]
    seq_start = i % tiles_per_seq == 0

    def prev_of(x, tail_ref, first_ref):
        first = jnp.where(seq_start, first_ref[...], tail_ref[7:8, :])
        rowi = lax.broadcasted_iota(jnp.int32, x.shape, 0)
        return jnp.where(rowi == 0, first, pltpu.roll(x, 1, 0))

    prev_rkv = prev_of(rkv, prkv_ref, frkv_ref)
    prev_low = prev_of(low, plow_ref, flow_ref)
    r, k2, v, kk, a, logw, g, bonus = _rwkv_features(rkv, low, prev_rkv, prev_low, par, True)
    cum = _dot_exact_lhs(ltri_ref[...], logw)
    e_in = jnp.exp(cum)
    e_out = jnp.exp(-cum)
    at_ref[...] = (-kk * jnp.exp(cum - logw)).astype(at_ref.dtype)
    bt_ref[...] = (kk * a * e_out).astype(bt_ref.dtype)
    kt_ref[...] = (k2 * e_out).astype(kt_ref.dtype)
    rt_ref[...] = (r * e_in).astype(rt_ref.dtype)
    v_ref[...] = v.astype(v_ref.dtype)
    g_ref[...] = g
    bon_ref[...] = bonus
    for c in range(tm // chunk):
        pt_ref[c] = e_in[(c + 1) * chunk - 1:(c + 1) * chunk, :]


def _wkv_chunk_body(at_ref, bt_ref, kt_ref, rt_ref, v_ref, pt_ref, g_ref, bon_ref, lnw_ref, lnb_ref, e_ref,
                    rw_ref, st_ref, s_scr, *, nc, chunk, npairs):
    T = chunk
    hp = LANES // RWKV_HEAD_DIM
    W = hp * T

    @pl.when(pl.program_id(1) == 0)
    def _():
        s_scr[...] = jnp.zeros_like(s_scr)

    lane_head = lax.broadcasted_iota(jnp.int32, (1, 1, LANES), 2) // RWKV_HEAD_DIM
    ri = lax.broadcasted_iota(jnp.int32, (1, W, W), 1)
    ci = lax.broadcasted_iota(jnp.int32, (1, W, W), 2)
    eye = jnp.where(ri == ci, 1.0, 0.0).astype(F32)
    zero = jnp.zeros((), BF16)
    e1 = functools.partial(jnp.einsum, preferred_element_type=F32)
    gram = functools.partial(e1, 'bid,bjd->bij')
    mm = functools.partial(e1, 'bij,bjd->bid')
    mm_nt = functools.partial(e1, 'bwk,bvk->bwv')
    bf = lambda t: t.astype(BF16)
    e128 = e_ref[...]
    lnw, lnb = lnw_ref[...], lnb_ref[...]

    def by_pair(x):
        return jnp.stack([x[:, p * LANES:(p + 1) * LANES] for p in range(npairs)], axis=0)

    def stacked(ref, rows):
        x = by_pair(ref[rows, :])
        return jnp.concatenate([jnp.where(lane_head == h, x, zero) for h in range(hp)], axis=1)

    def chunk_step(c, carry):
        rows = pl.ds(pl.multiple_of(c * T, T), T)
        a2, b2, k2, r2, v2 = (stacked(r, rows) for r in (at_ref, bt_ref, kt_ref, rt_ref, v_ref))
        bk = jnp.concatenate([b2, k2], axis=1)
        ga = gram(a2, bk)
        gr = gram(r2, bk)
        gab = jnp.where(ci < ri, ga[:, :, :W], 0.0)
        gak = bf(jnp.where(ci < ri, ga[:, :, W:], 0.0))
        grbk = bf(jnp.concatenate([jnp.where(ci <= ri, gr[:, :, :W], 0.0),
                                   jnp.where(ci <= ri, gr[:, :, W:], 0.0)], axis=2))
        x = eye + gab
        lp = gab
        span = 2
        while span < T:
            lp = mm(bf(lp), bf(lp))
            x = mm(bf(x), bf(eye + lp))
            span *= 2
        aw = jnp.concatenate([a2, bf(mm(gak, v2))], axis=2)
        xa = mm(bf(x), aw)
        ahat = bf(xa[:, :, :LANES])
        uhat = xa[:, :, LANES:]

        s = s_scr[...]
        sb = bf(s)
        u = mm_nt(ahat, sb) + uhat
        uv = jnp.concatenate([bf(u), v2], axis=1)
        y2 = mm_nt(r2, sb) + mm(grbk, uv)
        s_scr[...] = (s + e1('bwv,bwk->bvk', uv, bk)) * by_pair(pt_ref[c])

        y3 = y2[:, :T]
        for h in range(1, hp):
            y3 = y3 + y2[:, h * T:(h + 1) * T]
        y = jnp.concatenate([y3[p] for p in range(npairs)], axis=1)
        mu = _seg_sum(y, e128) / RWKV_HEAD_DIM
        yc = y - mu
        var = _seg_sum(yc * yc, e128) / RWKV_HEAD_DIM
        yn = yc * lax.rsqrt(var + GN_EPS) * lnw + lnb
        rw_ref[rows, :] = ((yn + bon_ref[rows, :]) * g_ref[rows, :]).astype(rw_ref.dtype)
        return carry

    lax.fori_loop(0, nc, chunk_step, 0)
    st_ref[...] = s_scr[...]


def _wkv_step_body(r_ref, k_ref, v_ref, kk_ref, a_ref, lw_ref, g_ref, bon_ref, lnw_ref, lnb_ref, s0_ref,
                   rw_ref, s1_ref, *, nh):
    d = s0_ref.shape[-1]
    eye = lax.broadcasted_iota(jnp.int32, (d, d), 0) == lax.broadcasted_iota(jnp.int32, (d, d), 1)
    for h in range(nh):
        s0 = s0_ref[h]
        r, k, v, kk, a = r_ref[h], k_ref[h], v_ref[h], kk_ref[h], a_ref[h]
        v_col = jnp.sum(jnp.where(eye, v, 0.0), axis=1, keepdims=True)
        sa = jnp.sum(s0 * (-kk), axis=1, keepdims=True)
        s1 = s0 * jnp.exp(lw_ref[h]) + sa * (kk * a) + v_col * k
        s1_ref[h] = s1
        y_col = jnp.sum(s1 * r, axis=1, keepdims=True)
        y = jnp.sum(jnp.where(eye, y_col, 0.0), axis=0, keepdims=True)
        mu = jnp.mean(y, axis=1, keepdims=True)
        var = jnp.mean(jnp.square(y - mu), axis=1, keepdims=True)
        yn = (y - mu) * lax.rsqrt(var + GN_EPS) * lnw_ref[h] + lnb_ref[h]
        rw_ref[h] = ((yn + bon_ref[h]) * g_ref[h]).astype(rw_ref.dtype)


def _rwkv_feat_body(*refs):
    rkv_ref, low_ref, prkv_ref, plow_ref = refs[:4]
    par = tuple(r[...] for r in refs[4:4 + _N_PAR])
    outs = refs[4 + _N_PAR:]
    vals = _rwkv_features(rkv_ref[...], low_ref[...], prkv_ref[...], plow_ref[...], par, False)
    for o, x in zip(outs, vals):
        o[...] = x


def _rwkv_params(p, rwkv_w):
    def pad_up(wt, r0):
        full = jnp.zeros((LOW_PAD, rwkv_w), F32).at[r0:r0 + wt.shape[0]].set(wt)
        hi = full.astype(BF16)
        return hi, (full - hi.astype(F32)).astype(BF16)

    wd_h, wd_l = pad_up(p['w_decay_up'], 0)
    wi_h, wi_l = pad_up(p['w_iclr_up'], DECAY_RANK)
    wg_h, wg_l = pad_up(p['w_gate_up'], DECAY_RANK + ICLR_RANK)
    hd = np.arange(LANES) // RWKV_HEAD_DIM
    e128 = jnp.asarray(hd[:, None] == hd[None, :], BF16)
    row = lambda t: t.reshape(1, -1).astype(F32)
    mu = p['mu_shift']
    mu_rkv = row(mu[:3 * rwkv_w])
    mu_low = row(jnp.pad(mu[3 * rwkv_w:], (0, LOW_PAD - LOW_W)))
    return (mu_rkv, mu_low, row(p['w0']), row(p['a0']), row(p['k_k']), row(p['k_a']), row(p['r_k']),
            wd_h, wd_l, wi_h, wi_l, wg_h, wg_l, e128)


def _full_spec(a):
    nd = a.ndim
    return pl.BlockSpec(a.shape, lambda *_: (0,) * nd)


def _rwkv_prompt(rkv, low, par, ln_w, ln_b, n, s):
    m = n * s
    w = rkv.shape[1] // 3
    tm = min(256, s)
    chunk = WKV_CHUNK
    cpt = tm // chunk
    hd = np.arange(tm) // chunk
    ltri = jnp.asarray((hd[:, None] == hd[None, :]) & (np.arange(tm)[:, None] >= np.arange(tm)[None, :]), BF16)
    zeros_rkv = jnp.zeros((n, 1, 3 * w), F32)
    zeros_low = jnp.zeros((n, 1, LOW_PAD), F32)
    tiles_per_seq = s // tm
    tail = lambda width: pl.BlockSpec((8, width), lambda i: (jnp.maximum(i * (tm // 8) - 1, 0), 0))
    first = lambda width: pl.BlockSpec((None, 1, width), lambda i: (i // tiles_per_seq, 0, 0))
    big = pl.BlockSpec((tm, w), lambda i: (i, 0))
    outs = pl.pallas_call(
        functools.partial(_rwkv_prep_body, tm=tm, chunk=chunk, tiles_per_seq=tiles_per_seq),
        grid=(m // tm,),
        in_specs=[pl.BlockSpec((tm, 3 * w), lambda i: (i, 0)), pl.BlockSpec((tm, LOW_PAD), lambda i: (i, 0)),
                  tail(3 * w), tail(LOW_PAD), first(3 * w), first(LOW_PAD), _full_spec(ltri),
                  *[_full_spec(t) for t in par]],
        out_specs=[big] * 7 + [pl.BlockSpec((cpt, 1, w), lambda i: (i, 0, 0))],
        out_shape=[jax.ShapeDtypeStruct((m, w), BF16)] * 5 + [jax.ShapeDtypeStruct((m, w), F32)] * 2
        + [jax.ShapeDtypeStruct((m // chunk, 1, w), F32)],
        compiler_params=_cparams(("parallel",)),
        name="rwkv_prep",
    )(rkv, low, rkv, low, zeros_rkv, zeros_low, ltri, *par)
    at, bt, kt, rt, v, g, bon, ptot = outs

    tb = _pick(s, 512)
    nc = tb // chunk
    npairs = w // LANES
    nblk = s // tb
    tok = pl.BlockSpec((tb, w), lambda b, i: (b * nblk + i, 0))
    vec = pl.BlockSpec((1, w), lambda b, i: (0, 0))
    rw, st = pl.pallas_call(
        functools.partial(_wkv_chunk_body, nc=nc, chunk=chunk, npairs=npairs),
        grid=(n, nblk),
        in_specs=[tok, tok, tok, tok, tok,
                  pl.BlockSpec((nc, 1, w), lambda b, i: (b * nblk + i, 0, 0)),
                  tok, tok, vec, vec, _full_spec(par[-1])],
        out_specs=[tok, pl.BlockSpec((None, npairs, LANES, LANES), lambda b, i: (b, 0, 0, 0))],
        out_shape=[jax.ShapeDtypeStruct((m, w), BF16), jax.ShapeDtypeStruct((n, npairs, LANES, LANES), F32)],
        scratch_shapes=[pltpu.VMEM((npairs, LANES, LANES), F32)],
        compiler_params=_cparams(("parallel", "arbitrary")),
        name="wkv_chunks",
    )(at, bt, kt, rt, v, ptot, g, bon, ln_w.reshape(1, w), ln_b.reshape(1, w), par[-1])
    hp = LANES // RWKV_HEAD_DIM
    d = RWKV_HEAD_DIM
    st = jnp.stack([st[:, :, j * d:(j + 1) * d, j * d:(j + 1) * d] for j in range(hp)], axis=2)
    return rw, st.reshape(n, npairs * hp, d, d)


def _rwkv_sample(rkv, low, prev_rkv, prev_low, state0, par, ln_w, ln_b):
    db = rkv.shape[0]
    w = rkv.shape[1] // 3
    d = RWKV_HEAD_DIM
    nh = w // d
    ins = (rkv, low, prev_rkv, prev_low, *par)
    feats = pl.pallas_call(
        _rwkv_feat_body,
        in_specs=[_full_spec(t) for t in ins],
        out_specs=[pl.BlockSpec((db, w), lambda: (0, 0))] * 8,
        out_shape=[jax.ShapeDtypeStruct((db, w), F32)] * 8,
        compiler_params=pltpu.CompilerParams(vmem_limit_bytes=VMEM_LIMIT),
        name="rwkv_feats",
    )(*ins)
    hv = lambda t: t.reshape(db, nh, 1, d)
    row = pl.BlockSpec((None, nh, 1, d), lambda b: (b, 0, 0, 0))
    prow = pl.BlockSpec((nh, 1, d), lambda b: (0, 0, 0))
    mat = pl.BlockSpec((None, nh, d, d), lambda b: (b, 0, 0, 0))
    rw, s1 = pl.pallas_call(
        functools.partial(_wkv_step_body, nh=nh),
        grid=(db,),
        in_specs=[row] * 8 + [prow, prow, mat],
        out_specs=[row, mat],
        out_shape=[jax.ShapeDtypeStruct((db, nh, 1, d), BF16), jax.ShapeDtypeStruct((db, nh, d, d), F32)],
        compiler_params=_cparams(("parallel",)),
        name="wkv_step",
    )(*[hv(t) for t in feats], ln_w.reshape(nh, 1, d), ln_b.reshape(nh, 1, d), state0)
    return rw.reshape(db, w), s1


def _merge_body(attn_ref, rw_ref, ga_ref, gb_ref, wa_ref, wr_ref, o_ref):
    ya = _dg(attn_ref[...], wa_ref[...])
    yb = _dg(rw_ref[...], wr_ref[...])
    o_ref[...] = (ga_ref[...] * ya + gb_ref[...] * yb).astype(o_ref.dtype)


def _merge(attn, rw, gates, wa, wr, tm, tn):
    m, ka = attn.shape
    kr = rw.shape[1]
    d = wa.shape[1]
    nj = d // tn
    return pl.pallas_call(
        _merge_body,
        grid=(m // tm, nj),
        in_specs=[pl.BlockSpec((tm, ka), lambda i, j: (i, 0)), pl.BlockSpec((tm, kr), lambda i, j: (i, 0)),
                  pl.BlockSpec((tm, tn), lambda i, j: (i, j)), pl.BlockSpec((tm, tn), lambda i, j: (i, nj + j)),
                  pl.BlockSpec((ka, tn), lambda i, j: (0, j)), pl.BlockSpec((kr, tn), lambda i, j: (0, j))],
        out_specs=pl.BlockSpec((tm, tn), lambda i, j: (i, j)),
        out_shape=jax.ShapeDtypeStruct((m, d), BF16),
        compiler_params=_cparams(("parallel", "parallel")),
        name="merge",
    )(attn, rw, gates, gates, wa, wr)


def _resid_mm_body(a_ref, w_ref, x_ref, o_ref):
    o_ref[...] = x_ref[...] + _dg(a_ref[...], w_ref[...])


def _resid_matmul(a, w, x, tm, tn):
    m, k = a.shape
    n = w.shape[1]
    return pl.pallas_call(
        _resid_mm_body,
        grid=(m // tm, n // tn),
        in_specs=[pl.BlockSpec((tm, k), lambda i, j: (i, 0)), pl.BlockSpec((k, tn), lambda i, j: (0, j)),
                  pl.BlockSpec((tm, tn), lambda i, j: (i, j))],
        out_specs=pl.BlockSpec((tm, tn), lambda i, j: (i, j)),
        out_shape=jax.ShapeDtypeStruct((m, n), F32),
        compiler_params=_cparams(("parallel", "parallel")),
        name="out_proj",
    )(a, w, x)


def _ffn_up_body(x_ref, g_ref, wg_ref, wu_ref, o_ref, xn_ref):
    @pl.when(pl.program_id(1) == 0)
    def _():
        xn_ref[...] = _rms(x_ref[...], g_ref[...]).astype(BF16)

    xn = xn_ref[...]
    o_ref[...] = (jax.nn.silu(_dg(xn, wg_ref[...])) * _dg(xn, wu_ref[...])).astype(o_ref.dtype)


def _ffn_up(x, g, wg, wu, tm, tn):
    m, d = x.shape
    f = wg.shape[1]
    wspec = pl.BlockSpec((d, tn), lambda i, j: (0, j))
    return pl.pallas_call(
        _ffn_up_body,
        grid=(m // tm, f // tn),
        in_specs=[pl.BlockSpec((tm, d), lambda i, j: (i, 0)), pl.BlockSpec((1, d), lambda i, j: (0, 0)), wspec, wspec],
        out_specs=pl.BlockSpec((tm, tn), lambda i, j: (i, j)),
        out_shape=jax.ShapeDtypeStruct((m, f), BF16),
        scratch_shapes=[pltpu.VMEM((tm, d), BF16)],
        compiler_params=_cparams(("parallel", "arbitrary")),
        name="ffn_up",
    )(x, g, wg, wu)


def _ffn_down_body(h_ref, w_ref, x_ref, g_ref, o_ref, *, nj, tn):
    j = pl.program_id(1)
    val = x_ref[...] + _dg(h_ref[...], w_ref[...])
    for t in range(nj):
        @pl.when(j == t)
        def _(t=t):
            o_ref[:, t * tn:(t + 1) * tn] = val

    @pl.when(j == nj - 1)
    def _():
        o_ref[...] = _rms(o_ref[...], g_ref[...])


def _ffn_down(h, w, x, g, tm, tn):
    m, f = h.shape
    d = w.shape[1]
    nj = d // tn
    return pl.pallas_call(
        functools.partial(_ffn_down_body, nj=nj, tn=tn),
        grid=(m // tm, nj),
        in_specs=[pl.BlockSpec((tm, f), lambda i, j: (i, 0)), pl.BlockSpec((f, tn), lambda i, j: (0, j)),
                  pl.BlockSpec((tm, tn), lambda i, j: (i, j)), pl.BlockSpec((1, d), lambda i, j: (0, 0))],
        out_specs=pl.BlockSpec((tm, d), lambda i, j: (i, 0)),
        out_shape=jax.ShapeDtypeStruct((m, d), F32),
        compiler_params=_cparams(("parallel", "arbitrary")),
        name="ffn_down",
    )(h, w, x, g)


def _pick(total, pref):
    t = min(pref, total)
    while total % t:
        t //= 2
    return t


def _layer(x2, n, s, pos, wts, par, p, attend, rwkv):
    m, d = x2.shape
    att_w = wts['qkv'].shape[1] // 3
    rw_w = wts['rkv'].shape[1] // 3
    tm = _pick(s, 1024) if s % 8 == 0 else _pick(m, 1024)
    assert s % tm == 0 or tm % s == 0
    g_mix = p['g_mix'].reshape(1, d)

    cos_t, sin_a, sin_b = _rope_tables(pos)
    per = max(s // tm, 1)
    if s < tm:
        cos_t, sin_a, sin_b = (jnp.tile(t, (tm // s, 1)) for t in (cos_t, sin_a, sin_b))
    tab = pl.BlockSpec((tm, ATT_HEAD_DIM), lambda i, j: (i % per, 0))
    qkv, xn = _norm_matmul(x2, g_mix, wts['qkv'], tm, att_w, 'proj_qkv', _epi_rope, (cos_t, sin_a, sin_b), (tab, tab, tab))
    rkv = _matmul(xn, wts['rkv'], tm, rw_w, 'proj_rkv')
    low = _matmul(xn, wts['low'], tm, LOW_PAD, 'proj_low')
    tn_g = _pick(2 * d, 1024)
    gates = _matmul(xn, wts['gate'], tm, tn_g, 'proj_gates', _epi_sigmoid_bias, (p['b_gate'].reshape(1, 2 * d),),
                    (pl.BlockSpec((1, tn_g), lambda i, j: (0, j)),), out_dtype=BF16)

    attn = attend(qkv)
    rw, wkv_new = rwkv(rkv, low)

    merged = _merge(attn, rw, gates, wts['proj_attn'], wts['proj_rwkv'], tm, _pick(d, 1024))
    hid = _resid_matmul(merged, wts['out'], x2, tm, _pick(d, 1024))
    f = wts['ffn_gate'].shape[1]
    h = _ffn_up(hid, p['g_ffn'].reshape(1, d), wts['ffn_gate'], wts['ffn_up'], tm, _pick(f, 512))
    y = _ffn_down(h, wts['ffn_down'], hid, p['g_final'].reshape(1, d), _pick(m, 512), _pick(d, 512))

    k = qkv[:, att_w:2 * att_w]
    v = qkv[:, 2 * att_w:]
    sh_last = jnp.concatenate([rkv.reshape(n, s, -1)[:, -1], low.reshape(n, s, -1)[:, -1, :LOW_W]], axis=1)
    return y, k, v, wkv_new, sh_last


def kernel(x_prompt, x_sample, cache_k, cache_v, state_wkv, state_shift, page_table, g_mix, w_in, b_gate, mu_shift, w0, w_decay_up, a0, w_iclr_up, w_gate_up, k_k, k_a, r_k, ln_x_w, ln_x_b, w_proj_attn, w_proj_rwkv, w_out, g_ffn, w_ffn_gate, w_ffn_up, w_ffn_down, g_final):
    depth = w_in.shape[0]
    assert depth == 1, "single-layer trunk"
    n, s, d = x_prompt.shape
    db, ds, _ = x_sample.shape
    assert ds == 1, "one new token per decode sequence"
    _, n_pool, page, nh, dh = cache_k.shape
    assert page == PAGE_SIZE and dh == ATT_HEAD_DIM
    att_w = nh * dh
    rw_w = w0.shape[1]
    n_pages = page_table.shape[1]
    past = n_pages * PAGE_SIZE
    assert s % MOBA_BLOCK == 0 and past % MOBA_BLOCK == 0
    l = 0

    wi = w_in[l]
    o = 3 * att_w
    wts = {
        'qkv': wi[:, :o].astype(BF16),
        'rkv': wi[:, o:o + 3 * rw_w].astype(BF16),
        'low': jnp.pad(wi[:, o + 3 * rw_w:o + 3 * rw_w + LOW_W], ((0, 0), (0, LOW_PAD - LOW_W))).astype(BF16),
        'gate': wi[:, o + 3 * rw_w + LOW_W:].astype(BF16),
        'proj_attn': w_proj_attn[l].astype(BF16), 'proj_rwkv': w_proj_rwkv[l].astype(BF16),
        'out': w_out[l].astype(BF16), 'ffn_gate': w_ffn_gate[l].astype(BF16),
        'ffn_up': w_ffn_up[l].astype(BF16), 'ffn_down': w_ffn_down[l].astype(BF16),
    }
    p = {'g_mix': g_mix[l], 'b_gate': b_gate[l], 'g_ffn': g_ffn[l], 'g_final': g_final,
         'mu_shift': mu_shift[l], 'w0': w0[l], 'w_decay_up': w_decay_up[l], 'a0': a0[l],
         'w_iclr_up': w_iclr_up[l], 'w_gate_up': w_gate_up[l], 'k_k': k_k[l], 'k_a': k_a[l], 'r_k': r_k[l]}
    par = _rwkv_params(p, rw_w)
    ln_w, ln_b = ln_x_w[l], ln_x_b[l]

    blk_sums = []

    def attend_prompt(qkv):
        attn, bs = _moba_prompt(qkv, n, s, page_table, cache_k, l)
        blk_sums.append(bs)
        return attn

    yp, kp, vp, wp, sp = _layer(
        x_prompt.reshape(n * s, d), n, s, jnp.arange(s), wts, par, p, attend_prompt,
        lambda rkv, low: _rwkv_prompt(rkv, low, par, ln_w, ln_b, n, s))

    q_blk = past // MOBA_BLOCK

    def attend_sample(qkv):
        q, k_new, v_new = qkv[:, :att_w], qkv[:, att_w:2 * att_w], qkv[:, 2 * att_w:]
        blk_sum = blk_sums[0].reshape(db, q_blk, att_w)
        sel = _decode_select(blk_sum, q, q_blk)
        return _decode_attn(page_table, sel, q, k_new, v_new, cache_k, cache_v, l)

    sh_prev = state_shift[l]
    prev_rkv = sh_prev[:, :3 * rw_w]
    prev_low = jnp.pad(sh_prev[:, 3 * rw_w:], ((0, 0), (0, LOW_PAD - LOW_W)))
    ys, ks_, vs_, ws_, ss_ = _layer(
        x_sample.reshape(db * ds, d), db, ds, past + jnp.arange(ds), wts, par, p,
        attend_sample,
        lambda rkv, low: _rwkv_sample(rkv, low, prev_rkv, prev_low, state_wkv[l], par, ln_w, ln_b))

    return (yp.reshape(n, s, d), ys.reshape(db, ds, d),
            kp.reshape(1, n, s, nh, dh), vp.reshape(1, n, s, nh, dh),
            wp.astype(state_wkv.dtype)[None], sp.astype(state_shift.dtype)[None],
            ks_.reshape(1, db, ds, nh, dh), vs_.reshape(1, db, ds, nh, dh),
            ws_.astype(state_wkv.dtype)[None], ss_.astype(state_shift.dtype)[None])
```

```python
import functools
import math
from typing import NamedTuple

import numpy as np
import jax
import jax.numpy as jnp
from jax import lax
from jax.experimental import pallas as pl
from jax.experimental.pallas import tpu as pltpu

F32 = jnp.float32
BF16 = jnp.bfloat16

ATT_HEAD_DIM = 128
ROT_DIM = ATT_HEAD_DIM // 4
ROPE_THETA = 500000.0
MOBA_BLOCK = 256
MOBA_TOP_K = 3
PAGE_SIZE = 128
RWKV_HEAD_DIM = 64
DECAY_RANK = 64
ICLR_RANK = 64
GATE_RANK = 160
RMS_EPS = 1e-6
GN_EPS = 64e-5
L2_EPS = 1e-12
LOG2E = math.log2(math.e)

LANES = 128
LOW_W = DECAY_RANK + ICLR_RANK + GATE_RANK
LOW_PAD = -(-LOW_W // LANES) * LANES
WKV_CHUNK = 64
VMEM_LIMIT = 56 * 1024 * 1024

NN = (((1,), (0,)), ((), ()))
NT = (((1,), (1,)), ((), ()))
TN = (((0,), (0,)), ((), ()))


def _cparams(sem):
    return pltpu.CompilerParams(dimension_semantics=sem, vmem_limit_bytes=VMEM_LIMIT)


def _dg(a, b, dims=NN):
    return lax.dot_general(a, b, dims, preferred_element_type=F32)


def _split2(x):
    hi = x.astype(BF16)
    lo = (x - hi.astype(F32)).astype(BF16)
    return hi, lo


def _dot3(a, b, dims=NN):
    ah, al = _split2(a)
    bh, bl = _split2(b)
    return _dg(ah, bh, dims) + (_dg(ah, bl, dims) + _dg(al, bh, dims))


def _dot3_pre(a, bh, bl, dims=NN):
    ah, al = _split2(a)
    return _dg(ah, bh, dims) + (_dg(ah, bl, dims) + _dg(al, bh, dims))


def _dot_exact_rhs(a, b_bf16, dims=NN):
    hi, lo = _split2(a)
    return _dg(hi, b_bf16, dims) + _dg(lo, b_bf16, dims)


def _dot_exact_lhs(a_bf16, b, dims=NN):
    hi, lo = _split2(b)
    return _dg(a_bf16, hi, dims) + _dg(a_bf16, lo, dims)


def _rms(x, g):
    return x * lax.rsqrt(jnp.mean(x * x, axis=-1, keepdims=True) + RMS_EPS) * g


def _norm_mm_body(*refs, epilogue, n_aux):
    x_ref, g_ref, w_ref = refs[:3]
    aux = refs[3:3 + n_aux]
    o_ref, xn_ref = refs[3 + n_aux:]
    j = pl.program_id(1)

    @pl.when(j == 0)
    def _():
        xn_ref[...] = _rms(x_ref[...], g_ref[...]).astype(BF16)

    acc = _dg(xn_ref[...], w_ref[...])
    o_ref[...] = epilogue(acc, j, *aux).astype(o_ref.dtype)


def _mm_body(*refs, epilogue, n_aux):
    a_ref, w_ref = refs[:2]
    aux = refs[2:2 + n_aux]
    o_ref, = refs[2 + n_aux:]
    o_ref[...] = epilogue(_dg(a_ref[...], w_ref[...]), pl.program_id(1), *aux).astype(o_ref.dtype)


def _epi_none(acc, j):
    return acc


def _epi_rope(acc, j, cos_ref, sa_ref, sb_ref):
    c, sa, sb = cos_ref[...], sa_ref[...], sb_ref[...]
    half = ROT_DIM // 2
    outs = []
    for h in range(acc.shape[1] // ATT_HEAD_DIM):
        xh = acc[:, h * ATT_HEAD_DIM:(h + 1) * ATT_HEAD_DIM]
        outs.append(xh * c + pltpu.roll(xh, ATT_HEAD_DIM - half, 1) * sa + pltpu.roll(xh, half, 1) * sb)
    rot = jnp.concatenate(outs, axis=1)
    rot = rot * jnp.where(j == 0, ATT_HEAD_DIM ** -0.5, 1.0).astype(F32)
    return jnp.where(j < 2, rot, acc)


def _epi_sigmoid_bias(acc, j, b_ref):
    return jax.nn.sigmoid(acc + b_ref[...])


def _norm_matmul(x, g, w, tm, tn, name, epilogue=_epi_none, aux=(), aux_specs=(), out_dtype=F32):
    m, d = x.shape
    n = w.shape[1]
    body = functools.partial(_norm_mm_body, epilogue=epilogue, n_aux=len(aux))
    return pl.pallas_call(
        body,
        grid=(m // tm, n // tn),
        in_specs=[pl.BlockSpec((tm, d), lambda i, j: (i, 0)),
                  pl.BlockSpec((1, d), lambda i, j: (0, 0)),
                  pl.BlockSpec((d, tn), lambda i, j: (0, j)), *aux_specs],
        out_specs=[pl.BlockSpec((tm, tn), lambda i, j: (i, j)), pl.BlockSpec((tm, d), lambda i, j: (i, 0))],
        out_shape=[jax.ShapeDtypeStruct((m, n), out_dtype), jax.ShapeDtypeStruct((m, d), BF16)],
        compiler_params=_cparams(("parallel", "arbitrary")),
        name=name,
    )(x, g, w, *aux)


def _matmul(a, w, tm, tn, name, epilogue=_epi_none, aux=(), aux_specs=(), out_dtype=F32):
    m, k = a.shape
    n = w.shape[1]
    return pl.pallas_call(
        functools.partial(_mm_body, epilogue=epilogue, n_aux=len(aux)),
        grid=(m // tm, n // tn),
        in_specs=[pl.BlockSpec((tm, k), lambda i, j: (i, 0)), pl.BlockSpec((k, tn), lambda i, j: (0, j)), *aux_specs],
        out_specs=pl.BlockSpec((tm, tn), lambda i, j: (i, j)),
        out_shape=jax.ShapeDtypeStruct((m, n), out_dtype),
        compiler_params=_cparams(("parallel", "parallel")),
        name=name,
    )(a, w, *aux)


def _rope_tables(pos):
    half = ROT_DIM // 2
    inv = jnp.exp(jnp.arange(half, dtype=F32) * (-2.0 * math.log(ROPE_THETA) / ROT_DIM))
    ang = pos.astype(F32)[:, None] * inv[None, :]
    cos, sin = jnp.cos(ang), jnp.sin(ang)
    s = pos.shape[0]
    z = lambda w: jnp.zeros((s, w), F32)
    cos_t = jnp.concatenate([cos, cos, jnp.ones((s, ATT_HEAD_DIM - ROT_DIM), F32)], axis=1)
    sin_a = jnp.concatenate([-sin, z(ATT_HEAD_DIM - half)], axis=1)
    sin_b = jnp.concatenate([z(half), sin, z(ATT_HEAD_DIM - ROT_DIM)], axis=1)
    return cos_t, sin_a, sin_b


def _moba_prompt_body(pt_ref, q_ref, k_ref, v_ref, ck_ref, o_ref, bs_ref, km_ref, kb_ref, vt_ref, s_ref,
                      pbuf, psum, psem, osem, *, nb, nh, pg):
    qb = pl.program_id(2)
    blk = MOBA_BLOCK
    sub = 8
    g = (pl.program_id(0) * nh + pl.program_id(1)) * nb + qb
    slot = g % 2

    def page_copies(step, slot_):
        b = step // pg.per_seq
        p0 = (step % pg.per_seq) * pg.pps
        return [pltpu.make_async_copy(ck_ref.at[pg.layer, pt_ref[b, p0 + j]], pbuf.at[slot_, j], psem.at[slot_])
                for j in range(pg.pps)]

    @pl.when(g == 0)
    def _():
        for c in page_copies(0, 0):
            c.start()

    @pl.when(g + 1 < pg.groups)
    def _():
        for c in page_copies(g + 1, 1 - slot):
            c.start()

    @pl.when(qb == 0)
    def _():
        km_ref[...] = jnp.zeros_like(km_ref)
        for j in range(nb):
            kj = k_ref[j * blk:(j + 1) * blk, :]
            km_ref[j:j + 1, :] = jnp.mean(kj, axis=0, keepdims=True)
            kb_ref[j] = kj.astype(BF16)
            vt_ref[j] = v_ref[j * blk:(j + 1) * blk, :].T.astype(BF16)

    q = q_ref[...]
    gate = _dot3(km_ref[...], q, NT)
    rowi = lax.broadcasted_iota(jnp.int32, gate.shape, 0)
    valid = rowi < qb
    gm = jnp.where(valid, gate, -jnp.inf)
    cnt = jnp.zeros(gate.shape, F32)
    for i in range(nb):
        gi = gm[i:i + 1, :]
        cnt = cnt + jnp.where(gi > gm, 1.0, 0.0) + jnp.where(gi == gm, (rowi > i).astype(F32), 0.0)
    bias = jnp.where(valid, jnp.where(cnt < MOBA_TOP_K, 0.0, -jnp.inf), -jnp.inf)

    qh = (q * LOG2E).astype(BF16)
    keyi = lax.broadcasted_iota(jnp.int32, (blk, blk), 0)
    qi = lax.broadcasted_iota(jnp.int32, (blk, blk), 1)
    fold = lambda t, op: op(t.reshape(blk // sub, sub, blk), axis=0)

    def sweep(qbv):
        pm = None
        for j in range(qbv + 1):
            s = _dg(kb_ref[j], qh, NT)
            s = jnp.where(keyi <= qi, s, -jnp.inf) if j == qbv else s + bias[j:j + 1, :]
            s_ref[j] = s
            t = fold(s, jnp.max)
            pm = t if pm is None else jnp.maximum(pm, t)
        m = jnp.max(pm, axis=0, keepdims=True)
        lp = jnp.zeros((sub, blk), F32)
        acc = jnp.zeros((ATT_HEAD_DIM, blk), F32)
        for j in range(qbv + 1):
            p = jnp.exp2(s_ref[j] - m)
            lp = lp + fold(p, jnp.sum)
            acc = acc + _dg(vt_ref[j], p.astype(BF16))
        l = jnp.sum(lp, axis=0, keepdims=True)
        o_ref[...] = (acc / l).T.astype(o_ref.dtype)

    for qbv in range(nb):
        pl.when(qb == qbv)(functools.partial(sweep, qbv))

    nbs = pg.pps // pg.ppb

    def sums_out(step, slot_):
        dst = bs_ref.at[step // pg.per_seq, pl.ds((step % pg.per_seq) * nbs, nbs)]
        return pltpu.make_async_copy(psum.at[slot_], dst, osem.at[slot_])

    @pl.when(g < pg.groups)
    def _():
        for c in page_copies(g, slot):
            c.wait()

        @pl.when(g >= 2)
        def _():
            sums_out(g - 2, slot).wait()

        for j in range(nbs):
            acc = jnp.sum(pbuf[slot, j * pg.ppb], axis=0)
            for t in range(1, pg.ppb):
                acc = acc + jnp.sum(pbuf[slot, j * pg.ppb + t], axis=0)
            psum[slot, j] = acc
        sums_out(g, slot).start()

        @pl.when(g == pg.groups - 1)
        def _():
            sums_out(g, slot).wait()

            @pl.when(g >= 1)
            def _():
                sums_out(g - 1, 1 - slot).wait()


class _PageGroups(NamedTuple):
    pps: int
    ppb: int
    per_seq: int
    groups: int
    layer: int


def _moba_prompt(qkv, n, s, page_table, cache_k, layer):
    att_w = qkv.shape[1] // 3
    nh = att_w // ATT_HEAD_DIM
    nb = s // MOBA_BLOCK
    nbp = -(-nb // 8) * 8
    db, n_pages = page_table.shape
    _, _, page, ch, dh = cache_k.shape
    ppb = MOBA_BLOCK // PAGE_SIZE
    nsteps = n * nh * nb
    pps = ppb
    while (db * n_pages // pps > nsteps or pps * 2 <= 8) and n_pages % (pps * 2) == 0:
        pps *= 2
    pg = _PageGroups(pps, ppb, n_pages // pps, db * n_pages // pps, layer)
    assert pg.groups <= nsteps, "more page groups than attention grid steps"
    gs = pltpu.PrefetchScalarGridSpec(
        num_scalar_prefetch=1, grid=(n, nh, nb),
        in_specs=[pl.BlockSpec((MOBA_BLOCK, ATT_HEAD_DIM), lambda b, h, i, pt: (b * nb + i, h)),
                  pl.BlockSpec((s, ATT_HEAD_DIM), lambda b, h, i, pt: (b, nh + h)),
                  pl.BlockSpec((s, ATT_HEAD_DIM), lambda b, h, i, pt: (b, 2 * nh + h)),
                  pl.BlockSpec(memory_space=pl.ANY)],
        out_specs=[pl.BlockSpec((MOBA_BLOCK, ATT_HEAD_DIM), lambda b, h, i, pt: (b * nb + i, h)),
                   pl.BlockSpec(memory_space=pl.ANY)],
        scratch_shapes=[pltpu.VMEM((nbp, ATT_HEAD_DIM), F32),
                        pltpu.VMEM((nb, MOBA_BLOCK, ATT_HEAD_DIM), BF16),
                        pltpu.VMEM((nb, ATT_HEAD_DIM, MOBA_BLOCK), BF16),
                        pltpu.VMEM((nb, MOBA_BLOCK, MOBA_BLOCK), F32),
                        pltpu.VMEM((2, pps, page, ch, dh), F32),
                        pltpu.VMEM((2, pps // ppb, ch, dh), F32),
                        pltpu.SemaphoreType.DMA((2,)), pltpu.SemaphoreType.DMA((2,))])
    return pl.pallas_call(
        functools.partial(_moba_prompt_body, nb=nb, nh=nh, pg=pg),
        grid_spec=gs,
        out_shape=[jax.ShapeDtypeStruct((n * s, att_w), BF16),
                   jax.ShapeDtypeStruct((db, n_pages // ppb, ch, dh), F32)],
        compiler_params=_cparams(("arbitrary", "arbitrary", "arbitrary")),
        name="moba_prompt",
    )(page_table, qkv, qkv, qkv, cache_k)


def _ring_step(copies, g, nsteps):
    slot = g % 2

    @pl.when(g == 0)
    def _():
        for c in copies(0, 0):
            c.start()

    @pl.when(g + 1 < nsteps)
    def _():
        for c in copies(g + 1, 1 - slot):
            c.start()

    for c in copies(g, slot):
        c.wait()
    return slot


def _decode_select_body(bs_ref, q_ref, sel_ref, *, nbp, q_blk, nh):
    bs = bs_ref[...]
    q = q_ref[...]
    it = lax.broadcasted_iota(jnp.int32, (nbp, 1), 0)
    for h in range(nh):
        sl = slice(h * ATT_HEAD_DIM, (h + 1) * ATT_HEAD_DIM)
        g = jnp.sum(q[:, sl] * (bs[:, sl] / MOBA_BLOCK), axis=1, keepdims=True)
        g = jnp.where(it < q_blk, g, -jnp.inf)
        for t in range(MOBA_TOP_K):
            mx = jnp.max(g, axis=0, keepdims=True)
            idx = jnp.min(jnp.where(g == mx, it, nbp), axis=0, keepdims=True)
            sel_ref[t:t + 1, h:h + 1] = jnp.where(mx > -jnp.inf, idx, -1)
            g = jnp.where(it == idx, -jnp.inf, g)


def _decode_select(blk_sum, q, q_blk):
    db, nbp, att_w = blk_sum.shape
    nh = att_w // ATT_HEAD_DIM
    return pl.pallas_call(
        functools.partial(_decode_select_body, nbp=nbp, q_blk=q_blk, nh=nh),
        grid=(db,),
        in_specs=[pl.BlockSpec((None, nbp, att_w), lambda b: (b, 0, 0)),
                  pl.BlockSpec((None, 1, att_w), lambda b: (b, 0, 0))],
        out_specs=pl.BlockSpec((None, MOBA_TOP_K, nh), lambda b: (b, 0, 0)),
        out_shape=jax.ShapeDtypeStruct((db, MOBA_TOP_K, nh), jnp.int32),
        compiler_params=_cparams(("parallel",)),
        name="decode_select",
    )(blk_sum, q.reshape(db, 1, att_w))


def _decode_attn_body(pt_ref, sel_ref, q_ref, kn_ref, vn_ref, ck_ref, cv_ref, o_ref, kbuf, vbuf, sem,
                      *, ppb, nh, nsteps, layer):
    g = pl.program_id(0)
    npg = MOBA_TOP_K * ppb

    def copies(step, slot):
        b, h = step // nh, step % nh
        cps = []
        for t in range(npg):
            blk = jnp.maximum(sel_ref[(b * MOBA_TOP_K + t // ppb) * nh + h], 0)
            page = pt_ref[b, blk * ppb + t % ppb]
            cps.append(pltpu.make_async_copy(ck_ref.at[layer, page, :, h, :], kbuf.at[slot, t], sem.at[0, slot]))
            cps.append(pltpu.make_async_copy(cv_ref.at[layer, page, :, h, :], vbuf.at[slot, t], sem.at[1, slot]))
        return cps

    slot = _ring_step(copies, g, nsteps)
    b, h = g // nh, g % nh
    q = q_ref[...]
    s_own = jnp.sum(q * kn_ref[...], axis=1, keepdims=True)
    m = s_own
    ss = []
    for t in range(npg):
        ok = sel_ref[(b * MOBA_TOP_K + t // ppb) * nh + h] >= 0
        st = jnp.where(ok, jnp.sum(kbuf[slot, t] * q, axis=1, keepdims=True), -jnp.inf)
        ss.append(st)
        m = jnp.maximum(m, jnp.max(st, axis=0, keepdims=True))
    p_own = jnp.exp(s_own - m)
    l = p_own
    acc = p_own * vn_ref[...]
    for t in range(npg):
        p = jnp.exp(ss[t] - m)
        l = l + jnp.sum(p, axis=0, keepdims=True)
        acc = acc + jnp.sum(p * vbuf[slot, t], axis=0, keepdims=True)
    o_ref[...] = (acc / l).astype(o_ref.dtype)


def _decode_attn(page_table, sel, q, k_new, v_new, cache_k, cache_v, layer):
    db, att_w = q.shape
    nh = att_w // ATT_HEAD_DIM
    ppb = MOBA_BLOCK // PAGE_SIZE
    npg = MOBA_TOP_K * ppb
    nsteps = db * nh
    row = pl.BlockSpec((None, 1, ATT_HEAD_DIM), lambda g, pt, sl: (g // nh, 0, g % nh))
    hbm = pl.BlockSpec(memory_space=pl.ANY)
    gs = pltpu.PrefetchScalarGridSpec(
        num_scalar_prefetch=2, grid=(nsteps,),
        in_specs=[row, row, row, hbm, hbm],
        out_specs=row,
        scratch_shapes=[pltpu.VMEM((2, npg, PAGE_SIZE, ATT_HEAD_DIM), F32),
                        pltpu.VMEM((2, npg, PAGE_SIZE, ATT_HEAD_DIM), F32),
                        pltpu.SemaphoreType.DMA((2, 2))])
    r3 = lambda a: a.reshape(db, 1, att_w)
    out = pl.pallas_call(
        functools.partial(_decode_attn_body, ppb=ppb, nh=nh, nsteps=nsteps, layer=layer), grid_spec=gs,
        out_shape=jax.ShapeDtypeStruct((db, 1, att_w), BF16),
        compiler_params=_cparams(("arbitrary",)),
        name="decode_attn",
    )(page_table, sel.reshape(-1), r3(q), r3(k_new), r3(v_new), cache_k, cache_v)
    return out.reshape(db, att_w)


def _seg_sum(x, e_bf16):
    outs = [_dot_exact_rhs(x[:, c:c + LANES], e_bf16) for c in range(0, x.shape[1], LANES)]
    return outs[0] if len(outs) == 1 else jnp.concatenate(outs, axis=1)


def _rwkv_features(rkv, low, prev_rkv, prev_low, par, single_pass):
    (mu_rkv, mu_low, w0, a0, k_k, k_a, r_k, wd_h, wd_l, wi_h, wi_l, wg_h, wg_l, e128) = par
    w = w0.shape[1]
    mix = rkv + mu_rkv * (prev_rkv - rkv)
    mlow = low + mu_low * (prev_low - low)
    r, k, v = mix[:, :w], mix[:, w:2 * w], mix[:, 2 * w:]
    lane = lax.broadcasted_iota(jnp.int32, mlow.shape, 1)
    feat = jnp.where(lane < DECAY_RANK, jnp.tanh(mlow),
                     jnp.where(lane < DECAY_RANK + ICLR_RANK, mlow,
                               jnp.where(lane < LOW_W, jax.nn.sigmoid(mlow), 0.0)))
    z = w0 + _dot3_pre(feat, wd_h, wd_l)
    nz = -z
    w_log = -(jnp.maximum(nz, 0.0) + jnp.log1p(jnp.exp(-jnp.abs(nz)))) - 0.5
    logw = -jnp.exp(w_log)
    if single_pass:
        fb = feat.astype(BF16)
        up_i, up_g = _dg(fb, wi_h), _dg(fb, wg_h)
    else:
        up_i, up_g = _dot3_pre(feat, wi_h, wi_l), _dot3_pre(feat, wg_h, wg_l)
    a = jax.nn.sigmoid(a0 + up_i)
    g = up_g
    kk = k * k_k
    kk = kk / jnp.maximum(jnp.sqrt(_seg_sum(kk * kk, e128)), L2_EPS)
    k2 = k * (1.0 + (a - 1.0) * k_a)
    bonus = _seg_sum(r * k2 * r_k, e128) * v
    return r, k2, v, kk, a, logw, g, bonus


_N_PAR = 14


def _rwkv_prep_body(*refs, tm, chunk, tiles_per_seq):
    rkv_ref, low_ref, prkv_ref, plow_ref, frkv_ref, flow_ref, ltri_ref = refs[:7]
    par = tuple(r[...] for r in refs[7:7 + _N_PAR])
    at_ref, bt_ref, kt_ref, rt_ref, v_ref, g_ref, bon_ref, pt_ref = refs[7 + _N_PAR:]
    i = pl.program_id(0)
    rkv, low = rkv_ref[...], low_ref[...]
    seq_start = i % tiles_per_seq == 0

    def prev_of(x, tail_ref, first_ref):
        first = jnp.where(seq_start, first_ref[...], tail_ref[7:8, :])
        rowi = lax.broadcasted_iota(jnp.int32, x.shape, 0)
        return jnp.where(rowi == 0, first, pltpu.roll(x, 1, 0))

    prev_rkv = prev_of(rkv, prkv_ref, frkv_ref)
    prev_low = prev_of(low, plow_ref, flow_ref)
    r, k2, v, kk, a, logw, g, bonus = _rwkv_features(rkv, low, prev_rkv, prev_low, par, True)
    cum = _dot_exact_lhs(ltri_ref[...], logw)
    e_in = jnp.exp(cum)
    e_out = jnp.exp(-cum)
    at_ref[...] = (-kk * jnp.exp(cum - logw)).astype(at_ref.dtype)
    bt_ref[...] = (kk * a * e_out).astype(bt_ref.dtype)
    kt_ref[...] = (k2 * e_out).astype(kt_ref.dtype)
    rt_ref[...] = (r * e_in).astype(rt_ref.dtype)
    v_ref[...] = v.astype(v_ref.dtype)
    g_ref[...] = g
    bon_ref[...] = bonus
    for c in range(tm // chunk):
        pt_ref[c] = e_in[(c + 1) * chunk - 1:(c + 1) * chunk, :]


def _wkv_chunk_body(at_ref, bt_ref, kt_ref, rt_ref, v_ref, pt_ref, g_ref, bon_ref, lnw_ref, lnb_ref, e_ref,
                    rw_ref, st_ref, s_scr, *, nc, chunk, npairs):
    T = chunk
    hp = LANES // RWKV_HEAD_DIM
    W = hp * T

    @pl.when(pl.program_id(1) == 0)
    def _():
        s_scr[...] = jnp.zeros_like(s_scr)

    lane_head = lax.broadcasted_iota(jnp.int32, (1, 1, LANES), 2) // RWKV_HEAD_DIM
    ri = lax.broadcasted_iota(jnp.int32, (1, W, W), 1)
    ci = lax.broadcasted_iota(jnp.int32, (1, W, W), 2)
    eye = jnp.where(ri == ci, 1.0, 0.0).astype(F32)
    zero = jnp.zeros((), BF16)
    e1 = functools.partial(jnp.einsum, preferred_element_type=F32)
    gram = functools.partial(e1, 'bid,bjd->bij')
    mm = functools.partial(e1, 'bij,bjd->bid')
    mm_nt = functools.partial(e1, 'bwk,bvk->bwv')
    bf = lambda t: t.astype(BF16)
    e128 = e_ref[...]
    lnw, lnb = lnw_ref[...], lnb_ref[...]

    def by_pair(x):
        return jnp.stack([x[:, p * LANES:(p + 1) * LANES] for p in range(npairs)], axis=0)

    def stacked(ref, rows):
        x = by_pair(ref[rows, :])
        return jnp.concatenate([jnp.where(lane_head == h, x, zero) for h in range(hp)], axis=1)

    def chunk_step(c, carry):
        rows = pl.ds(pl.multiple_of(c * T, T), T)
        a2, b2, k2, r2, v2 = (stacked(r, rows) for r in (at_ref, bt_ref, kt_ref, rt_ref, v_ref))
        bk = jnp.concatenate([b2, k2], axis=1)
        ga = gram(a2, bk)
        gr = gram(r2, bk)
        gab = jnp.where(ci < ri, ga[:, :, :W], 0.0)
        gak = bf(jnp.where(ci < ri, ga[:, :, W:], 0.0))
        grbk = bf(jnp.concatenate([jnp.where(ci <= ri, gr[:, :, :W], 0.0),
                                   jnp.where(ci <= ri, gr[:, :, W:], 0.0)], axis=2))
        x = eye + gab
        lp = gab
        span = 2
        while span < T:
            lp = mm(bf(lp), bf(lp))
            x = mm(bf(x), bf(eye + lp))
            span *= 2
        aw = jnp.concatenate([a2, bf(mm(gak, v2))], axis=2)
        xa = mm(bf(x), aw)
        ahat = bf(xa[:, :, :LANES])
        uhat = xa[:, :, LANES:]

        s = s_scr[...]
        sb = bf(s)
        u = mm_nt(ahat, sb) + uhat
        uv = jnp.concatenate([bf(u), v2], axis=1)
        y2 = mm_nt(r2, sb) + mm(grbk, uv)
        s_scr[...] = (s + e1('bwv,bwk->bvk', uv, bk)) * by_pair(pt_ref[c])

        y3 = y2[:, :T]
        for h in range(1, hp):
            y3 = y3 + y2[:, h * T:(h + 1) * T]
        y = jnp.concatenate([y3[p] for p in range(npairs)], axis=1)
        mu = _seg_sum(y, e128) / RWKV_HEAD_DIM
        yc = y - mu
        var = _seg_sum(yc * yc, e128) / RWKV_HEAD_DIM
        yn = yc * lax.rsqrt(var + GN_EPS) * lnw + lnb
        rw_ref[rows, :] = ((yn + bon_ref[rows, :]) * g_ref[rows, :]).astype(rw_ref.dtype)
        return carry

    lax.fori_loop(0, nc, chunk_step, 0)
    st_ref[...] = s_scr[...]


def _wkv_step_body(r_ref, k_ref, v_ref, kk_ref, a_ref, lw_ref, g_ref, bon_ref, lnw_ref, lnb_ref, s0_ref,
                   rw_ref, s1_ref, *, nh):
    d = s0_ref.shape[-1]
    eye = lax.broadcasted_iota(jnp.int32, (d, d), 0) == lax.broadcasted_iota(jnp.int32, (d, d), 1)
    for h in range(nh):
        s0 = s0_ref[h]
        r, k, v, kk, a = r_ref[h], k_ref[h], v_ref[h], kk_ref[h], a_ref[h]
        v_col = jnp.sum(jnp.where(eye, v, 0.0), axis=1, keepdims=True)
        sa = jnp.sum(s0 * (-kk), axis=1, keepdims=True)
        s1 = s0 * jnp.exp(lw_ref[h]) + sa * (kk * a) + v_col * k
        s1_ref[h] = s1
        y_col = jnp.sum(s1 * r, axis=1, keepdims=True)
        y = jnp.sum(jnp.where(eye, y_col, 0.0), axis=0, keepdims=True)
        mu = jnp.mean(y, axis=1, keepdims=True)
        var = jnp.mean(jnp.square(y - mu), axis=1, keepdims=True)
        yn = (y - mu) * lax.rsqrt(var + GN_EPS) * lnw_ref[h] + lnb_ref[h]
        rw_ref[h] = ((yn + bon_ref[h]) * g_ref[h]).astype(rw_ref.dtype)


def _rwkv_feat_body(*refs):
    rkv_ref, low_ref, prkv_ref, plow_ref = refs[:4]
    par = tuple(r[...] for r in refs[4:4 + _N_PAR])
    outs = refs[4 + _N_PAR:]
    vals = _rwkv_features(rkv_ref[...], low_ref[...], prkv_ref[...], plow_ref[...], par, False)
    for o, x in zip(outs, vals):
        o[...] = x


def _rwkv_params(p, rwkv_w):
    def pad_up(wt, r0):
        full = jnp.zeros((LOW_PAD, rwkv_w), F32).at[r0:r0 + wt.shape[0]].set(wt)
        hi = full.astype(BF16)
        return hi, (full - hi.astype(F32)).astype(BF16)

    wd_h, wd_l = pad_up(p['w_decay_up'], 0)
    wi_h, wi_l = pad_up(p['w_iclr_up'], DECAY_RANK)
    wg_h, wg_l = pad_up(p['w_gate_up'], DECAY_RANK + ICLR_RANK)
    hd = np.arange(LANES) // RWKV_HEAD_DIM
    e128 = jnp.asarray(hd[:, None] == hd[None, :], BF16)
    row = lambda t: t.reshape(1, -1).astype(F32)
    mu = p['mu_shift']
    mu_rkv = row(mu[:3 * rwkv_w])
    mu_low = row(jnp.pad(mu[3 * rwkv_w:], (0, LOW_PAD - LOW_W)))
    return (mu_rkv, mu_low, row(p['w0']), row(p['a0']), row(p['k_k']), row(p['k_a']), row(p['r_k']),
            wd_h, wd_l, wi_h, wi_l, wg_h, wg_l, e128)


def _full_spec(a):
    nd = a.ndim
    return pl.BlockSpec(a.shape, lambda *_: (0,) * nd)


def _rwkv_prompt(rkv, low, par, ln_w, ln_b, n, s):
    m = n * s
    w = rkv.shape[1] // 3
    tm = min(256, s)
    chunk = WKV_CHUNK
    cpt = tm // chunk
    hd = np.arange(tm) // chunk
    ltri = jnp.asarray((hd[:, None] == hd[None, :]) & (np.arange(tm)[:, None] >= np.arange(tm)[None, :]), BF16)
    zeros_rkv = jnp.zeros((n, 1, 3 * w), F32)
    zeros_low = jnp.zeros((n, 1, LOW_PAD), F32)
    tiles_per_seq = s // tm
    tail = lambda width: pl.BlockSpec((8, width), lambda i: (jnp.maximum(i * (tm // 8) - 1, 0), 0))
    first = lambda width: pl.BlockSpec((None, 1, width), lambda i: (i // tiles_per_seq, 0, 0))
    big = pl.BlockSpec((tm, w), lambda i: (i, 0))
    outs = pl.pallas_call(
        functools.partial(_rwkv_prep_body, tm=tm, chunk=chunk, tiles_per_seq=tiles_per_seq),
        grid=(m // tm,),
        in_specs=[pl.BlockSpec((tm, 3 * w), lambda i: (i, 0)), pl.BlockSpec((tm, LOW_PAD), lambda i: (i, 0)),
                  tail(3 * w), tail(LOW_PAD), first(3 * w), first(LOW_PAD), _full_spec(ltri),
                  *[_full_spec(t) for t in par]],
        out_specs=[big] * 7 + [pl.BlockSpec((cpt, 1, w), lambda i: (i, 0, 0))],
        out_shape=[jax.ShapeDtypeStruct((m, w), BF16)] * 5 + [jax.ShapeDtypeStruct((m, w), F32)] * 2
        + [jax.ShapeDtypeStruct((m // chunk, 1, w), F32)],
        compiler_params=_cparams(("parallel",)),
        name="rwkv_prep",
    )(rkv, low, rkv, low, zeros_rkv, zeros_low, ltri, *par)
    at, bt, kt, rt, v, g, bon, ptot = outs

    tb = _pick(s, 512)
    nc = tb // chunk
    npairs = w // LANES
    nblk = s // tb
    tok = pl.BlockSpec((tb, w), lambda b, i: (b * nblk + i, 0))
    vec = pl.BlockSpec((1, w), lambda b, i: (0, 0))
    rw, st = pl.pallas_call(
        functools.partial(_wkv_chunk_body, nc=nc, chunk=chunk, npairs=npairs),
        grid=(n, nblk),
        in_specs=[tok, tok, tok, tok, tok,
                  pl.BlockSpec((nc, 1, w), lambda b, i: (b * nblk + i, 0, 0)),
                  tok, tok, vec, vec, _full_spec(par[-1])],
        out_specs=[tok, pl.BlockSpec((None, npairs, LANES, LANES), lambda b, i: (b, 0, 0, 0))],
        out_shape=[jax.ShapeDtypeStruct((m, w), BF16), jax.ShapeDtypeStruct((n, npairs, LANES, LANES), F32)],
        scratch_shapes=[pltpu.VMEM((npairs, LANES, LANES), F32)],
        compiler_params=_cparams(("parallel", "arbitrary")),
        name="wkv_chunks",
    )(at, bt, kt, rt, v, ptot, g, bon, ln_w.reshape(1, w), ln_b.reshape(1, w), par[-1])
    hp = LANES // RWKV_HEAD_DIM
    d = RWKV_HEAD_DIM
    st = jnp.stack([st[:, :, j * d:(j + 1) * d, j * d:(j + 1) * d] for j in range(hp)], axis=2)
    return rw, st.reshape(n, npairs * hp, d, d)


def _rwkv_sample(rkv, low, prev_rkv, prev_low, state0, par, ln_w, ln_b):
    db = rkv.shape[0]
    w = rkv.shape[1] // 3
    d = RWKV_HEAD_DIM
    nh = w // d
    ins = (rkv, low, prev_rkv, prev_low, *par)
    feats = pl.pallas_call(
        _rwkv_feat_body,
        in_specs=[_full_spec(t) for t in ins],
        out_specs=[pl.BlockSpec((db, w), lambda: (0, 0))] * 8,
        out_shape=[jax.ShapeDtypeStruct((db, w), F32)] * 8,
        compiler_params=pltpu.CompilerParams(vmem_limit_bytes=VMEM_LIMIT),
        name="rwkv_feats",
    )(*ins)
    hv = lambda t: t.reshape(db, nh, 1, d)
    row = pl.BlockSpec((None, nh, 1, d), lambda b: (b, 0, 0, 0))
    prow = pl.BlockSpec((nh, 1, d), lambda b: (0, 0, 0))
    mat = pl.BlockSpec((None, nh, d, d), lambda b: (b, 0, 0, 0))
    rw, s1 = pl.pallas_call(
        functools.partial(_wkv_step_body, nh=nh),
        grid=(db,),
        in_specs=[row] * 8 + [prow, prow, mat],
        out_specs=[row, mat],
        out_shape=[jax.ShapeDtypeStruct((db, nh, 1, d), BF16), jax.ShapeDtypeStruct((db, nh, d, d), F32)],
        compiler_params=_cparams(("parallel",)),
        name="wkv_step",
    )(*[hv(t) for t in feats], ln_w.reshape(nh, 1, d), ln_b.reshape(nh, 1, d), state0)
    return rw.reshape(db, w), s1


def _merge_body(attn_ref, rw_ref, ga_ref, gb_ref, wa_ref, wr_ref, o_ref):
    ya = _dg(attn_ref[...], wa_ref[...])
    yb = _dg(rw_ref[...], wr_ref[...])
    o_ref[...] = (ga_ref[...] * ya + gb_ref[...] * yb).astype(o_ref.dtype)


def _merge(attn, rw, gates, wa, wr, tm, tn):
    m, ka = attn.shape
    kr = rw.shape[1]
    d = wa.shape[1]
    nj = d // tn
    return pl.pallas_call(
        _merge_body,
        grid=(m // tm, nj),
        in_specs=[pl.BlockSpec((tm, ka), lambda i, j: (i, 0)), pl.BlockSpec((tm, kr), lambda i, j: (i, 0)),
                  pl.BlockSpec((tm, tn), lambda i, j: (i, j)), pl.BlockSpec((tm, tn), lambda i, j: (i, nj + j)),
                  pl.BlockSpec((ka, tn), lambda i, j: (0, j)), pl.BlockSpec((kr, tn), lambda i, j: (0, j))],
        out_specs=pl.BlockSpec((tm, tn), lambda i, j: (i, j)),
        out_shape=jax.ShapeDtypeStruct((m, d), BF16),
        compiler_params=_cparams(("parallel", "parallel")),
        name="merge",
    )(attn, rw, gates, gates, wa, wr)


def _resid_mm_body(a_ref, w_ref, x_ref, o_ref):
    o_ref[...] = x_ref[...] + _dg(a_ref[...], w_ref[...])


def _resid_matmul(a, w, x, tm, tn):
    m, k = a.shape
    n = w.shape[1]
    return pl.pallas_call(
        _resid_mm_body,
        grid=(m // tm, n // tn),
        in_specs=[pl.BlockSpec((tm, k), lambda i, j: (i, 0)), pl.BlockSpec((k, tn), lambda i, j: (0, j)),
                  pl.BlockSpec((tm, tn), lambda i, j: (i, j))],
        out_specs=pl.BlockSpec((tm, tn), lambda i, j: (i, j)),
        out_shape=jax.ShapeDtypeStruct((m, n), F32),
        compiler_params=_cparams(("parallel", "parallel")),
        name="out_proj",
    )(a, w, x)


def _ffn_up_body(x_ref, g_ref, wg_ref, wu_ref, o_ref, xn_ref):
    @pl.when(pl.program_id(1) == 0)
    def _():
        xn_ref[...] = _rms(x_ref[...], g_ref[...]).astype(BF16)

    xn = xn_ref[...]
    o_ref[...] = (jax.nn.silu(_dg(xn, wg_ref[...])) * _dg(xn, wu_ref[...])).astype(o_ref.dtype)


def _ffn_up(x, g, wg, wu, tm, tn):
    m, d = x.shape
    f = wg.shape[1]
    wspec = pl.BlockSpec((d, tn), lambda i, j: (0, j))
    return pl.pallas_call(
        _ffn_up_body,
        grid=(m // tm, f // tn),
        in_specs=[pl.BlockSpec((tm, d), lambda i, j: (i, 0)), pl.BlockSpec((1, d), lambda i, j: (0, 0)), wspec, wspec],
        out_specs=pl.BlockSpec((tm, tn), lambda i, j: (i, j)),
        out_shape=jax.ShapeDtypeStruct((m, f), BF16),
        scratch_shapes=[pltpu.VMEM((tm, d), BF16)],
        compiler_params=_cparams(("parallel", "arbitrary")),
        name="ffn_up",
    )(x, g, wg, wu)


def _ffn_down_body(h_ref, w_ref, x_ref, g_ref, o_ref, *, nj, tn):
    j = pl.program_id(1)
    val = x_ref[...] + _dg(h_ref[...], w_ref[...])
    for t in range(nj):
        @pl.when(j == t)
        def _(t=t):
            o_ref[:, t * tn:(t + 1) * tn] = val

    @pl.when(j == nj - 1)
    def _():
        o_ref[...] = _rms(o_ref[...], g_ref[...])


def _ffn_down(h, w, x, g, tm, tn):
    m, f = h.shape
    d = w.shape[1]
    nj = d // tn
    return pl.pallas_call(
        functools.partial(_ffn_down_body, nj=nj, tn=tn),
        grid=(m // tm, nj),
        in_specs=[pl.BlockSpec((tm, f), lambda i, j: (i, 0)), pl.BlockSpec((f, tn), lambda i, j: (0, j)),
                  pl.BlockSpec((tm, tn), lambda i, j: (i, j)), pl.BlockSpec((1, d), lambda i, j: (0, 0))],
        out_specs=pl.BlockSpec((tm, d), lambda i, j: (i, 0)),
        out_shape=jax.ShapeDtypeStruct((m, d), F32),
        compiler_params=_cparams(("parallel", "arbitrary")),
        name="ffn_down",
    )(h, w, x, g)


def _pick(total, pref):
    t = min(pref, total)
    while total % t:
        t //= 2
    return t


def _layer(x2, n, s, pos, wts, par, p, attend, rwkv):
    m, d = x2.shape
    att_w = wts['qkv'].shape[1] // 3
    rw_w = wts['rkv'].shape[1] // 3
    tm = _pick(s, 1024) if s % 8 == 0 else _pick(m, 1024)
    assert s % tm == 0 or tm % s == 0
    g_mix = p['g_mix'].reshape(1, d)

    cos_t, sin_a, sin_b = _rope_tables(pos)
    per = max(s // tm, 1)
    if s < tm:
        cos_t, sin_a, sin_b = (jnp.tile(t, (tm // s, 1)) for t in (cos_t, sin_a, sin_b))
    tab = pl.BlockSpec((tm, ATT_HEAD_DIM), lambda i, j: (i % per, 0))
    qkv, xn = _norm_matmul(x2, g_mix, wts['qkv'], tm, att_w, 'proj_qkv', _epi_rope, (cos_t, sin_a, sin_b), (tab, tab, tab))
    rkv = _matmul(xn, wts['rkv'], tm, rw_w, 'proj_rkv')
    low = _matmul(xn, wts['low'], tm, LOW_PAD, 'proj_low')
    tn_g = _pick(2 * d, 1024)
    gates = _matmul(xn, wts['gate'], tm, tn_g, 'proj_gates', _epi_sigmoid_bias, (p['b_gate'].reshape(1, 2 * d),),
                    (pl.BlockSpec((1, tn_g), lambda i, j: (0, j)),), out_dtype=BF16)

    attn = attend(qkv)
    rw, wkv_new = rwkv(rkv, low)

    merged = _merge(attn, rw, gates, wts['proj_attn'], wts['proj_rwkv'], tm, _pick(d, 1024))
    hid = _resid_matmul(merged, wts['out'], x2, tm, _pick(d, 1024))
    f = wts['ffn_gate'].shape[1]
    h = _ffn_up(hid, p['g_ffn'].reshape(1, d), wts['ffn_gate'], wts['ffn_up'], tm, _pick(f, 512))
    y = _ffn_down(h, wts['ffn_down'], hid, p['g_final'].reshape(1, d), _pick(m, 512), _pick(d, 512))

    k = qkv[:, att_w:2 * att_w]
    v = qkv[:, 2 * att_w:]
    sh_last = jnp.concatenate([rkv.reshape(n, s, -1)[:, -1], low.reshape(n, s, -1)[:, -1, :LOW_W]], axis=1)
    return y, k, v, wkv_new, sh_last


def kernel(x_prompt, x_sample, cache_k, cache_v, state_wkv, state_shift, page_table, g_mix, w_in, b_gate, mu_shift, w0, w_decay_up, a0, w_iclr_up, w_gate_up, k_k, k_a, r_k, ln_x_w, ln_x_b, w_proj_attn, w_proj_rwkv, w_out, g_ffn, w_ffn_gate, w_ffn_up, w_ffn_down, g_final):
    depth = w_in.shape[0]
    assert depth == 1, "single-layer trunk"
    n, s, d = x_prompt.shape
    db, ds, _ = x_sample.shape
    assert ds == 1, "one new token per decode sequence"
    _, n_pool, page, nh, dh = cache_k.shape
    assert page == PAGE_SIZE and dh == ATT_HEAD_DIM
    att_w = nh * dh
    rw_w = w0.shape[1]
    n_pages = page_table.shape[1]
    past = n_pages * PAGE_SIZE
    assert s % MOBA_BLOCK == 0 and past % MOBA_BLOCK == 0
    l = 0

    wi = w_in[l]
    o = 3 * att_w
    wts = {
        'qkv': wi[:, :o].astype(BF16),
        'rkv': wi[:, o:o + 3 * rw_w].astype(BF16),
        'low': jnp.pad(wi[:, o + 3 * rw_w:o + 3 * rw_w + LOW_W], ((0, 0), (0, LOW_PAD - LOW_W))).astype(BF16),
        'gate': wi[:, o + 3 * rw_w + LOW_W:].astype(BF16),
        'proj_attn': w_proj_attn[l].astype(BF16), 'proj_rwkv': w_proj_rwkv[l].astype(BF16),
        'out': w_out[l].astype(BF16), 'ffn_gate': w_ffn_gate[l].astype(BF16),
        'ffn_up': w_ffn_up[l].astype(BF16), 'ffn_down': w_ffn_down[l].astype(BF16),
    }
    p = {'g_mix': g_mix[l], 'b_gate': b_gate[l], 'g_ffn': g_ffn[l], 'g_final': g_final,
         'mu_shift': mu_shift[l], 'w0': w0[l], 'w_decay_up': w_decay_up[l], 'a0': a0[l],
         'w_iclr_up': w_iclr_up[l], 'w_gate_up': w_gate_up[l], 'k_k': k_k[l], 'k_a': k_a[l], 'r_k': r_k[l]}
    par = _rwkv_params(p, rw_w)
    ln_w, ln_b = ln_x_w[l], ln_x_b[l]

    blk_sums = []

    def attend_prompt(qkv):
        attn, bs = _moba_prompt(qkv, n, s, page_table, cache_k, l)
        blk_sums.append(bs)
        return attn

    yp, kp, vp, wp, sp = _layer(
        x_prompt.reshape(n * s, d), n, s, jnp.arange(s), wts, par, p, attend_prompt,
        lambda rkv, low: _rwkv_prompt(rkv, low, par, ln_w, ln_b, n, s))

    q_blk = past // MOBA_BLOCK

    def attend_sample(qkv):
        q, k_new, v_new = qkv[:, :att_w], qkv[:, att_w:2 * att_w], qkv[:, 2 * att_w:]
        blk_sum = blk_sums[0].reshape(db, q_blk, att_w)
        sel = _decode_select(blk_sum, q, q_blk)
        return _decode_attn(page_table, sel, q, k_new, v_new, cache_k, cache_v, l)

    sh_prev = state_shift[l]
    prev_rkv = sh_prev[:, :3 * rw_w]
    prev_low = jnp.pad(sh_prev[:, 3 * rw_w:], ((0, 0), (0, LOW_PAD - LOW_W)))
    ys, ks_, vs_, ws_, ss_ = _layer(
        x_sample.reshape(db * ds, d), db, ds, past + jnp.arange(ds), wts, par, p,
        attend_sample,
        lambda rkv, low: _rwkv_sample(rkv, low, prev_rkv, prev_low, state_wkv[l], par, ln_w, ln_b))

    return (yp.reshape(n, s, d), ys.reshape(db, ds, d),
            kp.reshape(1, n, s, nh, dh), vp.reshape(1, n, s, nh, dh),
            wp.astype(state_wkv.dtype)[None], sp.astype(state_shift.dtype)[None],
            ks_.reshape(1, db, ds, nh, dh), vs_.reshape(1, db, ds, nh, dh),
            ws_.astype(state_wkv.dtype)[None], ss_.astype(state_shift.dtype)[None])
```

```python
import functools
import math
from typing import NamedTuple

import numpy as np
import jax
import jax.numpy as jnp
from jax import lax
from jax.experimental import pallas as pl
from jax.experimental.pallas import tpu as pltpu

F32 = jnp.float32
BF16 = jnp.bfloat16

ATT_HEAD_DIM = 128
ROT_DIM = ATT_HEAD_DIM // 4
ROPE_THETA = 500000.0
MOBA_BLOCK = 256
MOBA_TOP_K = 3
PAGE_SIZE = 128
RWKV_HEAD_DIM = 64
DECAY_RANK = 64
ICLR_RANK = 64
GATE_RANK = 160
RMS_EPS = 1e-6
GN_EPS = 64e-5
L2_EPS = 1e-12
LOG2E = math.log2(math.e)

LANES = 128
LOW_W = DECAY_RANK + ICLR_RANK + GATE_RANK
LOW_PAD = -(-LOW_W // LANES) * LANES
WKV_CHUNK = 64
VMEM_LIMIT = 56 * 1024 * 1024

NN = (((1,), (0,)), ((), ()))
NT = (((1,), (1,)), ((), ()))
TN = (((0,), (0,)), ((), ()))


def _cparams(sem):
    return pltpu.CompilerParams(dimension_semantics=sem, vmem_limit_bytes=VMEM_LIMIT)


def _dg(a, b, dims=NN):
    return lax.dot_general(a, b, dims, preferred_element_type=F32)


def _split2(x):
    hi = x.astype(BF16)
    lo = (x - hi.astype(F32)).astype(BF16)
    return hi, lo


def _dot3(a, b, dims=NN):
    ah, al = _split2(a)
    bh, bl = _split2(b)
    return _dg(ah, bh, dims) + (_dg(ah, bl, dims) + _dg(al, bh, dims))


def _dot3_pre(a, bh, bl, dims=NN):
    ah, al = _split2(a)
    return _dg(ah, bh, dims) + (_dg(ah, bl, dims) + _dg(al, bh, dims))


def _dot_exact_rhs(a, b_bf16, dims=NN):
    hi, lo = _split2(a)
    return _dg(hi, b_bf16, dims) + _dg(lo, b_bf16, dims)


def _dot_exact_lhs(a_bf16, b, dims=NN):
    hi, lo = _split2(b)
    return _dg(a_bf16, hi, dims) + _dg(a_bf16, lo, dims)


def _rms(x, g):
    return x * lax.rsqrt(jnp.mean(x * x, axis=-1, keepdims=True) + RMS_EPS) * g


def _norm_mm_body(*refs, epilogue, n_aux):
    x_ref, g_ref, w_ref = refs[:3]
    aux = refs[3:3 + n_aux]
    o_ref, xn_ref = refs[3 + n_aux:]
    j = pl.program_id(1)

    @pl.when(j == 0)
    def _():
        xn_ref[...] = _rms(x_ref[...], g_ref[...]).astype(BF16)

    acc = _dg(xn_ref[...], w_ref[...])
    o_ref[...] = epilogue(acc, j, *aux).astype(o_ref.dtype)


def _mm_body(*refs, epilogue, n_aux):
    a_ref, w_ref = refs[:2]
    aux = refs[2:2 + n_aux]
    o_ref, = refs[2 + n_aux:]
    o_ref[...] = epilogue(_dg(a_ref[...], w_ref[...]), pl.program_id(1), *aux).astype(o_ref.dtype)


def _epi_none(acc, j):
    return acc


def _epi_rope(acc, j, cos_ref, sa_ref, sb_ref):
    c, sa, sb = cos_ref[...], sa_ref[...], sb_ref[...]
    half = ROT_DIM // 2
    outs = []
    for h in range(acc.shape[1] // ATT_HEAD_DIM):
        xh = acc[:, h * ATT_HEAD_DIM:(h + 1) * ATT_HEAD_DIM]
        outs.append(xh * c + pltpu.roll(xh, ATT_HEAD_DIM - half, 1) * sa + pltpu.roll(xh, half, 1) * sb)
    rot = jnp.concatenate(outs, axis=1)
    rot = rot * jnp.where(j == 0, ATT_HEAD_DIM ** -0.5, 1.0).astype(F32)
    return jnp.where(j < 2, rot, acc)


def _epi_sigmoid_bias(acc, j, b_ref):
    return jax.nn.sigmoid(acc + b_ref[...])


def _norm_matmul(x, g, w, tm, tn, name, epilogue=_epi_none, aux=(), aux_specs=(), out_dtype=F32):
    m, d = x.shape
    n = w.shape[1]
    body = functools.partial(_norm_mm_body, epilogue=epilogue, n_aux=len(aux))
    return pl.pallas_call(
        body,
        grid=(m // tm, n // tn),
        in_specs=[pl.BlockSpec((tm, d), lambda i, j: (i, 0)),
                  pl.BlockSpec((1, d), lambda i, j: (0, 0)),
                  pl.BlockSpec((d, tn), lambda i, j: (0, j)), *aux_specs],
        out_specs=[pl.BlockSpec((tm, tn), lambda i, j: (i, j)), pl.BlockSpec((tm, d), lambda i, j: (i, 0))],
        out_shape=[jax.ShapeDtypeStruct((m, n), out_dtype), jax.ShapeDtypeStruct((m, d), BF16)],
        compiler_params=_cparams(("parallel", "arbitrary")),
        name=name,
    )(x, g, w, *aux)


def _matmul(a, w, tm, tn, name, epilogue=_epi_none, aux=(), aux_specs=(), out_dtype=F32):
    m, k = a.shape
    n = w.shape[1]
    return pl.pallas_call(
        functools.partial(_mm_body, epilogue=epilogue, n_aux=len(aux)),
        grid=(m // tm, n // tn),
        in_specs=[pl.BlockSpec((tm, k), lambda i, j: (i, 0)), pl.BlockSpec((k, tn), lambda i, j: (0, j)), *aux_specs],
        out_specs=pl.BlockSpec((tm, tn), lambda i, j: (i, j)),
        out_shape=jax.ShapeDtypeStruct((m, n), out_dtype),
        compiler_params=_cparams(("parallel", "parallel")),
        name=name,
    )(a, w, *aux)


def _rope_tables(pos):
    half = ROT_DIM // 2
    inv = jnp.exp(jnp.arange(half, dtype=F32) * (-2.0 * math.log(ROPE_THETA) / ROT_DIM))
    ang = pos.astype(F32)[:, None] * inv[None, :]
    cos, sin = jnp.cos(ang), jnp.sin(ang)
    s = pos.shape[0]
    z = lambda w: jnp.zeros((s, w), F32)
    cos_t = jnp.concatenate([cos, cos, jnp.ones((s, ATT_HEAD_DIM - ROT_DIM), F32)], axis=1)
    sin_a = jnp.concatenate([-sin, z(ATT_HEAD_DIM - half)], axis=1)
    sin_b = jnp.concatenate([z(half), sin, z(ATT_HEAD_DIM - ROT_DIM)], axis=1)
    return cos_t, sin_a, sin_b


def _moba_prompt_body(pt_ref, q_ref, k_ref, v_ref, ck_ref, o_ref, bs_ref, km_ref, kb_ref, vt_ref, s_ref,
                      pbuf, psum, psem, osem, *, nb, nh, qpb, pg):
    qp = pl.program_id(2)
    blk = MOBA_BLOCK
    sub = 8
    g = (pl.program_id(0) * nh + pl.program_id(1)) * (nb // qpb) + qp
    slot = g % 2

    def page_copies(step, slot_):
        b = step // pg.per_seq
        p0 = (step % pg.per_seq) * pg.pps
        return [pltpu.make_async_copy(ck_ref.at[pg.layer, pt_ref[b, p0 + j]], pbuf.at[slot_, j], psem.at[slot_])
                for j in range(pg.pps)]

    @pl.when(g == 0)
    def _():
        for c in page_copies(0, 0):
            c.start()

    @pl.when(g + 1 < pg.groups)
    def _():
        for c in page_copies(g + 1, 1 - slot):
            c.start()

    @pl.when(qp == 0)
    def _():
        km_ref[...] = jnp.zeros_like(km_ref)
        for j in range(nb):
            kj = k_ref[j * blk:(j + 1) * blk, :]
            km_ref[j:j + 1, :] = jnp.mean(kj, axis=0, keepdims=True)
            kb_ref[j] = kj.astype(BF16)
            vt_ref[j] = v_ref[j * blk:(j + 1) * blk, :].T.astype(BF16)

    q = q_ref[...]
    gate = _dot3(km_ref[...], q, NT)
    rowi = lax.broadcasted_iota(jnp.int32, gate.shape, 0)
    qblk = qp * qpb + lax.broadcasted_iota(jnp.int32, gate.shape, 1) // blk
    valid = rowi < qblk
    gm = jnp.where(valid, gate, -jnp.inf)
    cnt = jnp.zeros(gate.shape, F32)
    for i in range(nb):
        gi = gm[i:i + 1, :]
        cnt = cnt + jnp.where(gi > gm, 1.0, 0.0) + jnp.where(gi == gm, (rowi > i).astype(F32), 0.0)
    bias = jnp.where(valid, jnp.where(cnt < MOBA_TOP_K, 0.0, -jnp.inf), -jnp.inf)

    qh = (q * LOG2E).astype(BF16)
    keyi = lax.broadcasted_iota(jnp.int32, (blk, blk), 0)
    qi = lax.broadcasted_iota(jnp.int32, (blk, blk), 1)
    fold = lambda t, op: op(t.reshape(blk // sub, sub, blk), axis=0)

    def sweep(qbv, u):
        cols = slice(u * blk, (u + 1) * blk)
        qu = qh[cols, :]
        pm = None
        for j in range(qbv + 1):
            s = _dg(kb_ref[j], qu, NT)
            s = jnp.where(keyi <= qi, s, -jnp.inf) if j == qbv else s + bias[j:j + 1, cols]
            s_ref[u, j] = s
            t = fold(s, jnp.max)
            pm = t if pm is None else jnp.maximum(pm, t)
        m = jnp.max(pm, axis=0, keepdims=True)
        lp = jnp.zeros((sub, blk), F32)
        acc = jnp.zeros((ATT_HEAD_DIM, blk), F32)
        for j in range(qbv + 1):
            p = jnp.exp2(s_ref[u, j] - m)
            lp = lp + fold(p, jnp.sum)
            acc = acc + _dg(vt_ref[j], p.astype(BF16))
        l = jnp.sum(lp, axis=0, keepdims=True)
        o_ref[cols, :] = (acc / l).T.astype(o_ref.dtype)

    def group(gv):
        for u in range(qpb):
            sweep(gv * qpb + u, u)

    for gv in range(nb // qpb):
        pl.when(qp == gv)(functools.partial(group, gv))

    nbs = pg.pps // pg.ppb

    def sums_out(step, slot_):
        dst = bs_ref.at[step // pg.per_seq, pl.ds((step % pg.per_seq) * nbs, nbs)]
        return pltpu.make_async_copy(psum.at[slot_], dst, osem.at[slot_])

    @pl.when(g < pg.groups)
    def _():
        for c in page_copies(g, slot):
            c.wait()

        @pl.when(g >= 2)
        def _():
            sums_out(g - 2, slot).wait()

        for j in range(nbs):
            acc = jnp.sum(pbuf[slot, j * pg.ppb], axis=0)
            for t in range(1, pg.ppb):
                acc = acc + jnp.sum(pbuf[slot, j * pg.ppb + t], axis=0)
            psum[slot, j] = acc
        sums_out(g, slot).start()

        @pl.when(g == pg.groups - 1)
        def _():
            sums_out(g, slot).wait()

            @pl.when(g >= 1)
            def _():
                sums_out(g - 1, 1 - slot).wait()


class _PageGroups(NamedTuple):
    pps: int
    ppb: int
    per_seq: int
    groups: int
    layer: int


def _moba_prompt(qkv, n, s, page_table, cache_k, layer):
    att_w = qkv.shape[1] // 3
    nh = att_w // ATT_HEAD_DIM
    nb = s // MOBA_BLOCK
    nbp = -(-nb // 8) * 8
    db, n_pages = page_table.shape
    _, _, page, ch, dh = cache_k.shape
    ppb = MOBA_BLOCK // PAGE_SIZE
    qpb = 2 if nb % 2 == 0 else 1
    nq = nb // qpb
    nsteps = n * nh * nq
    pps = ppb
    while (db * n_pages // pps > nsteps or pps * 2 <= 8) and n_pages % (pps * 2) == 0:
        pps *= 2
    pg = _PageGroups(pps, ppb, n_pages // pps, db * n_pages // pps, layer)
    assert pg.groups <= nsteps, "more page groups than attention grid steps"
    gs = pltpu.PrefetchScalarGridSpec(
        num_scalar_prefetch=1, grid=(n, nh, nq),
        in_specs=[pl.BlockSpec((qpb * MOBA_BLOCK, ATT_HEAD_DIM), lambda b, h, i, pt: (b * nq + i, h)),
                  pl.BlockSpec((s, ATT_HEAD_DIM), lambda b, h, i, pt: (b, nh + h)),
                  pl.BlockSpec((s, ATT_HEAD_DIM), lambda b, h, i, pt: (b, 2 * nh + h)),
                  pl.BlockSpec(memory_space=pl.ANY)],
        out_specs=[pl.BlockSpec((qpb * MOBA_BLOCK, ATT_HEAD_DIM), lambda b, h, i, pt: (b * nq + i, h)),
                   pl.BlockSpec(memory_space=pl.ANY)],
        scratch_shapes=[pltpu.VMEM((nbp, ATT_HEAD_DIM), F32),
                        pltpu.VMEM((nb, MOBA_BLOCK, ATT_HEAD_DIM), BF16),
                        pltpu.VMEM((nb, ATT_HEAD_DIM, MOBA_BLOCK), BF16),
                        pltpu.VMEM((qpb, nb, MOBA_BLOCK, MOBA_BLOCK), F32),
                        pltpu.VMEM((2, pps, page, ch, dh), F32),
                        pltpu.VMEM((2, pps // ppb, ch, dh), F32),
                        pltpu.SemaphoreType.DMA((2,)), pltpu.SemaphoreType.DMA((2,))])
    return pl.pallas_call(
        functools.partial(_moba_prompt_body, nb=nb, nh=nh, qpb=qpb, pg=pg),
        grid_spec=gs,
        out_shape=[jax.ShapeDtypeStruct((n * s, att_w), BF16),
                   jax.ShapeDtypeStruct((db, n_pages // ppb, ch, dh), F32)],
        compiler_params=_cparams(("arbitrary", "arbitrary", "arbitrary")),
        name="moba_prompt",
    )(page_table, qkv, qkv, qkv, cache_k)


def _ring_step(copies, g, nsteps):
    slot = g % 2

    @pl.when(g == 0)
    def _():
        for c in copies(0, 0):
            c.start()

    @pl.when(g + 1 < nsteps)
    def _():
        for c in copies(g + 1, 1 - slot):
            c.start()

    for c in copies(g, slot):
        c.wait()
    return slot


def _decode_select_body(bs_ref, q_ref, sel_ref, *, nbp, q_blk, nh):
    bs = bs_ref[...]
    q = q_ref[...]
    it = lax.broadcasted_iota(jnp.int32, (nbp, 1), 0)
    for h in range(nh):
        sl = slice(h * ATT_HEAD_DIM, (h + 1) * ATT_HEAD_DIM)
        g = jnp.sum(q[:, sl] * (bs[:, sl] / MOBA_BLOCK), axis=1, keepdims=True)
        g = jnp.where(it < q_blk, g, -jnp.inf)
        for t in range(MOBA_TOP_K):
            mx = jnp.max(g, axis=0, keepdims=True)
            idx = jnp.min(jnp.where(g == mx, it, nbp), axis=0, keepdims=True)
            sel_ref[t:t + 1, h:h + 1] = jnp.where(mx > -jnp.inf, idx, -1)
            g = jnp.where(it == idx, -jnp.inf, g)


def _decode_select(blk_sum, q, q_blk):
    db, nbp, att_w = blk_sum.shape
    nh = att_w // ATT_HEAD_DIM
    return pl.pallas_call(
        functools.partial(_decode_select_body, nbp=nbp, q_blk=q_blk, nh=nh),
        grid=(db,),
        in_specs=[pl.BlockSpec((None, nbp, att_w), lambda b: (b, 0, 0)),
                  pl.BlockSpec((None, 1, att_w), lambda b: (b, 0, 0))],
        out_specs=pl.BlockSpec((None, MOBA_TOP_K, nh), lambda b: (b, 0, 0)),
        out_shape=jax.ShapeDtypeStruct((db, MOBA_TOP_K, nh), jnp.int32),
        compiler_params=_cparams(("parallel",)),
        name="decode_select",
    )(blk_sum, q.reshape(db, 1, att_w))


def _decode_attn_body(pt_ref, sel_ref, q_ref, kn_ref, vn_ref, ck_ref, cv_ref, o_ref, kbuf, vbuf, sem,
                      *, ppb, nh, hps, nsteps, layer):
    g = pl.program_id(0)
    npg = MOBA_TOP_K * ppb
    hgroups = nh // hps

    def copies(step, slot):
        b, h0 = step // hgroups, (step % hgroups) * hps
        cps = []
        for u in range(hps):
            for t in range(npg):
                blk = jnp.maximum(sel_ref[(b * MOBA_TOP_K + t // ppb) * nh + h0 + u], 0)
                page = pt_ref[b, blk * ppb + t % ppb]
                cps.append(pltpu.make_async_copy(ck_ref.at[layer, page, :, h0 + u, :], kbuf.at[slot, u, t], sem.at[0, slot]))
                cps.append(pltpu.make_async_copy(cv_ref.at[layer, page, :, h0 + u, :], vbuf.at[slot, u, t], sem.at[1, slot]))
        return cps

    slot = _ring_step(copies, g, nsteps)
    b, h0 = g // hgroups, (g % hgroups) * hps
    for u in range(hps):
        lanes = slice(u * ATT_HEAD_DIM, (u + 1) * ATT_HEAD_DIM)
        q = q_ref[:, lanes]
        s_own = jnp.sum(q * kn_ref[:, lanes], axis=1, keepdims=True)
        m = s_own
        ss = []
        for t in range(npg):
            ok = sel_ref[(b * MOBA_TOP_K + t // ppb) * nh + h0 + u] >= 0
            st = jnp.where(ok, jnp.sum(kbuf[slot, u, t] * q, axis=1, keepdims=True), -jnp.inf)
            ss.append(st)
            m = jnp.maximum(m, jnp.max(st, axis=0, keepdims=True))
        p_own = jnp.exp(s_own - m)
        l = p_own
        acc = p_own * vn_ref[:, lanes]
        for t in range(npg):
            p = jnp.exp(ss[t] - m)
            l = l + jnp.sum(p, axis=0, keepdims=True)
            acc = acc + jnp.sum(p * vbuf[slot, u, t], axis=0, keepdims=True)
        o_ref[:, lanes] = (acc / l).astype(o_ref.dtype)


def _decode_attn(page_table, sel, q, k_new, v_new, cache_k, cache_v, layer):
    db, att_w = q.shape
    nh = att_w // ATT_HEAD_DIM
    ppb = MOBA_BLOCK // PAGE_SIZE
    npg = MOBA_TOP_K * ppb
    hps = 2 if nh % 2 == 0 else 1
    hgroups = nh // hps
    nsteps = db * hgroups
    row = pl.BlockSpec((None, 1, hps * ATT_HEAD_DIM), lambda g, pt, sl: (g // hgroups, 0, g % hgroups))
    hbm = pl.BlockSpec(memory_space=pl.ANY)
    gs = pltpu.PrefetchScalarGridSpec(
        num_scalar_prefetch=2, grid=(nsteps,),
        in_specs=[row, row, row, hbm, hbm],
        out_specs=row,
        scratch_shapes=[pltpu.VMEM((2, hps, npg, PAGE_SIZE, ATT_HEAD_DIM), F32),
                        pltpu.VMEM((2, hps, npg, PAGE_SIZE, ATT_HEAD_DIM), F32),
                        pltpu.SemaphoreType.DMA((2, 2))])
    r3 = lambda a: a.reshape(db, 1, att_w)
    out = pl.pallas_call(
        functools.partial(_decode_attn_body, ppb=ppb, nh=nh, hps=hps, nsteps=nsteps, layer=layer), grid_spec=gs,
        out_shape=jax.ShapeDtypeStruct((db, 1, att_w), BF16),
        compiler_params=_cparams(("arbitrary",)),
        name="decode_attn",
    )(page_table, sel.reshape(-1), r3(q), r3(k_new), r3(v_new), cache_k, cache_v)
    return out.reshape(db, att_w)


def _seg_sum(x, e_bf16):
    outs = [_dot_exact_rhs(x[:, c:c + LANES], e_bf16) for c in range(0, x.shape[1], LANES)]
    return outs[0] if len(outs) == 1 else jnp.concatenate(outs, axis=1)


def _rwkv_features(rkv, low, prev_rkv, prev_low, par, single_pass):
    (mu_rkv, mu_low, w0, a0, k_k, k_a, r_k, wd_h, wd_l, wi_h, wi_l, wg_h, wg_l, e128) = par
    w = w0.shape[1]
    mix = rkv + mu_rkv * (prev_rkv - rkv)
    mlow = low + mu_low * (prev_low - low)
    r, k, v = mix[:, :w], mix[:, w:2 * w], mix[:, 2 * w:]
    lane = lax.broadcasted_iota(jnp.int32, mlow.shape, 1)
    feat = jnp.where(lane < DECAY_RANK, jnp.tanh(mlow),
                     jnp.where(lane < DECAY_RANK + ICLR_RANK, mlow,
                               jnp.where(lane < LOW_W, jax.nn.sigmoid(mlow), 0.0)))
    z = w0 + _dot3_pre(feat, wd_h, wd_l)
    nz = -z
    w_log = -(jnp.maximum(nz, 0.0) + jnp.log1p(jnp.exp(-jnp.abs(nz)))) - 0.5
    logw = -jnp.exp(w_log)
    if single_pass:
        fb = feat.astype(BF16)
        up_i, up_g = _dg(fb, wi_h), _dg(fb, wg_h)
    else:
        up_i, up_g = _dot3_pre(feat, wi_h, wi_l), _dot3_pre(feat, wg_h, wg_l)
    a = jax.nn.sigmoid(a0 + up_i)
    g = up_g
    kk = k * k_k
    kk = kk / jnp.maximum(jnp.sqrt(_seg_sum(kk * kk, e128)), L2_EPS)
    k2 = k * (1.0 + (a - 1.0) * k_a)
    bonus = _seg_sum(r * k2 * r_k, e128) * v
    return r, k2, v, kk, a, logw, g, bonus


_N_PAR = 14


def _rwkv_prep_body(*refs, tm, chunk, tiles_per_seq):
    rkv_ref, low_ref, prkv_ref, plow_ref, frkv_ref, flow_ref, ltri_ref = refs[:7]
    par = tuple(r[...] for r in refs[7:7 + _N_PAR])
    at_ref, bt_ref, kt_ref, rt_ref, v_ref, g_ref, bon_ref, pt_ref = refs[7 + _N_PAR:]
    i = pl.program_id(0)
    rkv, low = rkv_ref[...], low_ref[...]
    seq_start = i % tiles_per_seq == 0

    def prev_of(x, tail_ref, first_ref):
        first = jnp.where(seq_start, first_ref[...], tail_ref[7:8, :])
        rowi = lax.broadcasted_iota(jnp.int32, x.shape, 0)
        return jnp.where(rowi == 0, first, pltpu.roll(x, 1, 0))

    prev_rkv = prev_of(rkv, prkv_ref, frkv_ref)
    prev_low = prev_of(low, plow_ref, flow_ref)
    r, k2, v, kk, a, logw, g, bonus = _rwkv_features(rkv, low, prev_rkv, prev_low, par, True)
    cum = _dot_exact_lhs(ltri_ref[...], logw)
    e_in = jnp.exp(cum)
    e_out = jnp.exp(-cum)
    at_ref[...] = (-kk * jnp.exp(cum - logw)).astype(at_ref.dtype)
    bt_ref[...] = (kk * a * e_out).astype(bt_ref.dtype)
    kt_ref[...] = (k2 * e_out).astype(kt_ref.dtype)
    rt_ref[...] = (r * e_in).astype(rt_ref.dtype)
    v_ref[...] = v.astype(v_ref.dtype)
    g_ref[...] = g
    bon_ref[...] = bonus
    for c in range(tm // chunk):
        pt_ref[c] = e_in[(c + 1) * chunk - 1:(c + 1) * chunk, :]


def _wkv_chunk_body(at_ref, bt_ref, kt_ref, rt_ref, v_ref, pt_ref, g_ref, bon_ref, lnw_ref, lnb_ref, e_ref,
                    rw_ref, st_ref, s_scr, *, nc, chunk, npairs):
    T = chunk
    hp = LANES // RWKV_HEAD_DIM
    W = hp * T

    @pl.when(pl.program_id(1) == 0)
    def _():
        s_scr[...] = jnp.zeros_like(s_scr)

    lane_head = lax.broadcasted_iota(jnp.int32, (1, 1, LANES), 2) // RWKV_HEAD_DIM
    ri = lax.broadcasted_iota(jnp.int32, (1, W, W), 1)
    ci = lax.broadcasted_iota(jnp.int32, (1, W, W), 2)
    eye = jnp.where(ri == ci, 1.0, 0.0).astype(F32)
    zero = jnp.zeros((), BF16)
    e1 = functools.partial(jnp.einsum, preferred_element_type=F32)
    gram = functools.partial(e1, 'bid,bjd->bij')
    mm = functools.partial(e1, 'bij,bjd->bid')
    mm_nt = functools.partial(e1, 'bwk,bvk->bwv')
    bf = lambda t: t.astype(BF16)
    e128 = e_ref[...]
    lnw, lnb = lnw_ref[...], lnb_ref[...]

    def by_pair(x):
        return jnp.stack([x[:, p * LANES:(p + 1) * LANES] for p in range(npairs)], axis=0)

    def stacked(ref, rows):
        x = by_pair(ref[rows, :])
        return jnp.concatenate([jnp.where(lane_head == h, x, zero) for h in range(hp)], axis=1)

    def chunk_step(c, carry):
        rows = pl.ds(pl.multiple_of(c * T, T), T)
        a2, b2, k2, r2, v2 = (stacked(r, rows) for r in (at_ref, bt_ref, kt_ref, rt_ref, v_ref))
        bk = jnp.concatenate([b2, k2], axis=1)
        ga = gram(a2, bk)
        gr = gram(r2, bk)
        gab = jnp.where(ci < ri, ga[:, :, :W], 0.0)
        gak = bf(jnp.where(ci < ri, ga[:, :, W:], 0.0))
        grbk = bf(jnp.concatenate([jnp.where(ci <= ri, gr[:, :, :W], 0.0),
                                   jnp.where(ci <= ri, gr[:, :, W:], 0.0)], axis=2))
        x = eye + gab
        lp = gab
        span = 2
        while span < T:
            lp = mm(bf(lp), bf(lp))
            x = mm(bf(x), bf(eye + lp))
            span *= 2
        aw = jnp.concatenate([a2, bf(mm(gak, v2))], axis=2)
        xa = mm(bf(x), aw)
        ahat = bf(xa[:, :, :LANES])
        uhat = xa[:, :, LANES:]

        s = s_scr[...]
        sb = bf(s)
        u = mm_nt(ahat, sb) + uhat
        uv = jnp.concatenate([bf(u), v2], axis=1)
        y2 = mm_nt(r2, sb) + mm(grbk, uv)
        s_scr[...] = (s + e1('bwv,bwk->bvk', uv, bk)) * by_pair(pt_ref[c])

        y3 = y2[:, :T]
        for h in range(1, hp):
            y3 = y3 + y2[:, h * T:(h + 1) * T]
        y = jnp.concatenate([y3[p] for p in range(npairs)], axis=1)
        mu = _seg_sum(y, e128) / RWKV_HEAD_DIM
        yc = y - mu
        var = _seg_sum(yc * yc, e128) / RWKV_HEAD_DIM
        yn = yc * lax.rsqrt(var + GN_EPS) * lnw + lnb
        rw_ref[rows, :] = ((yn + bon_ref[rows, :]) * g_ref[rows, :]).astype(rw_ref.dtype)
        return carry

    lax.fori_loop(0, nc, chunk_step, 0)
    st_ref[...] = s_scr[...]


def _wkv_step_body(r_ref, k_ref, v_ref, kk_ref, a_ref, lw_ref, g_ref, bon_ref, lnw_ref, lnb_ref, s0_ref,
                   rw_ref, s1_ref, *, nh):
    d = s0_ref.shape[-1]
    eye = lax.broadcasted_iota(jnp.int32, (d, d), 0) == lax.broadcasted_iota(jnp.int32, (d, d), 1)
    for h in range(nh):
        s0 = s0_ref[h]
        r, k, v, kk, a = r_ref[h], k_ref[h], v_ref[h], kk_ref[h], a_ref[h]
        v_col = jnp.sum(jnp.where(eye, v, 0.0), axis=1, keepdims=True)
        sa = jnp.sum(s0 * (-kk), axis=1, keepdims=True)
        s1 = s0 * jnp.exp(lw_ref[h]) + sa * (kk * a) + v_col * k
        s1_ref[h] = s1
        y_col = jnp.sum(s1 * r, axis=1, keepdims=True)
        y = jnp.sum(jnp.where(eye, y_col, 0.0), axis=0, keepdims=True)
        mu = jnp.mean(y, axis=1, keepdims=True)
        var = jnp.mean(jnp.square(y - mu), axis=1, keepdims=True)
        yn = (y - mu) * lax.rsqrt(var + GN_EPS) * lnw_ref[h] + lnb_ref[h]
        rw_ref[h] = ((yn + bon_ref[h]) * g_ref[h]).astype(rw_ref.dtype)


def _rwkv_feat_body(*refs):
    rkv_ref, low_ref, prkv_ref, plow_ref = refs[:4]
    par = tuple(r[...] for r in refs[4:4 + _N_PAR])
    outs = refs[4 + _N_PAR:]
    vals = _rwkv_features(rkv_ref[...], low_ref[...], prkv_ref[...], plow_ref[...], par, False)
    for o, x in zip(outs, vals):
        o[...] = x


def _rwkv_params(p, rwkv_w):
    def pad_up(wt, r0):
        full = jnp.zeros((LOW_PAD, rwkv_w), F32).at[r0:r0 + wt.shape[0]].set(wt)
        hi = full.astype(BF16)
        return hi, (full - hi.astype(F32)).astype(BF16)

    wd_h, wd_l = pad_up(p['w_decay_up'], 0)
    wi_h, wi_l = pad_up(p['w_iclr_up'], DECAY_RANK)
    wg_h, wg_l = pad_up(p['w_gate_up'], DECAY_RANK + ICLR_RANK)
    hd = np.arange(LANES) // RWKV_HEAD_DIM
    e128 = jnp.asarray(hd[:, None] == hd[None, :], BF16)
    row = lambda t: t.reshape(1, -1).astype(F32)
    mu = p['mu_shift']
    mu_rkv = row(mu[:3 * rwkv_w])
    mu_low = row(jnp.pad(mu[3 * rwkv_w:], (0, LOW_PAD - LOW_W)))
    return (mu_rkv, mu_low, row(p['w0']), row(p['a0']), row(p['k_k']), row(p['k_a']), row(p['r_k']),
            wd_h, wd_l, wi_h, wi_l, wg_h, wg_l, e128)


def _full_spec(a):
    nd = a.ndim
    return pl.BlockSpec(a.shape, lambda *_: (0,) * nd)


def _rwkv_prompt(rkv, low, par, ln_w, ln_b, n, s):
    m = n * s
    w = rkv.shape[1] // 3
    tm = min(256, s)
    chunk = WKV_CHUNK
    cpt = tm // chunk
    hd = np.arange(tm) // chunk
    ltri = jnp.asarray((hd[:, None] == hd[None, :]) & (np.arange(tm)[:, None] >= np.arange(tm)[None, :]), BF16)
    zeros_rkv = jnp.zeros((n, 1, 3 * w), F32)
    zeros_low = jnp.zeros((n, 1, LOW_PAD), F32)
    tiles_per_seq = s // tm
    tail = lambda width: pl.BlockSpec((8, width), lambda i: (jnp.maximum(i * (tm // 8) - 1, 0), 0))
    first = lambda width: pl.BlockSpec((None, 1, width), lambda i: (i // tiles_per_seq, 0, 0))
    big = pl.BlockSpec((tm, w), lambda i: (i, 0))
    outs = pl.pallas_call(
        functools.partial(_rwkv_prep_body, tm=tm, chunk=chunk, tiles_per_seq=tiles_per_seq),
        grid=(m // tm,),
        in_specs=[pl.BlockSpec((tm, 3 * w), lambda i: (i, 0)), pl.BlockSpec((tm, LOW_PAD), lambda i: (i, 0)),
                  tail(3 * w), tail(LOW_PAD), first(3 * w), first(LOW_PAD), _full_spec(ltri),
                  *[_full_spec(t) for t in par]],
        out_specs=[big] * 7 + [pl.BlockSpec((cpt, 1, w), lambda i: (i, 0, 0))],
        out_shape=[jax.ShapeDtypeStruct((m, w), BF16)] * 5 + [jax.ShapeDtypeStruct((m, w), F32)] * 2
        + [jax.ShapeDtypeStruct((m // chunk, 1, w), F32)],
        compiler_params=_cparams(("parallel",)),
        name="rwkv_prep",
    )(rkv, low, rkv, low, zeros_rkv, zeros_low, ltri, *par)
    at, bt, kt, rt, v, g, bon, ptot = outs

    tb = _pick(s, 512)
    nc = tb // chunk
    npairs = w // LANES
    nblk = s // tb
    tok = pl.BlockSpec((tb, w), lambda b, i: (b * nblk + i, 0))
    vec = pl.BlockSpec((1, w), lambda b, i: (0, 0))
    rw, st = pl.pallas_call(
        functools.partial(_wkv_chunk_body, nc=nc, chunk=chunk, npairs=npairs),
        grid=(n, nblk),
        in_specs=[tok, tok, tok, tok, tok,
                  pl.BlockSpec((nc, 1, w), lambda b, i: (b * nblk + i, 0, 0)),
                  tok, tok, vec, vec, _full_spec(par[-1])],
        out_specs=[tok, pl.BlockSpec((None, npairs, LANES, LANES), lambda b, i: (b, 0, 0, 0))],
        out_shape=[jax.ShapeDtypeStruct((m, w), BF16), jax.ShapeDtypeStruct((n, npairs, LANES, LANES), F32)],
        scratch_shapes=[pltpu.VMEM((npairs, LANES, LANES), F32)],
        compiler_params=_cparams(("parallel", "arbitrary")),
        name="wkv_chunks",
    )(at, bt, kt, rt, v, ptot, g, bon, ln_w.reshape(1, w), ln_b.reshape(1, w), par[-1])
    hp = LANES // RWKV_HEAD_DIM
    d = RWKV_HEAD_DIM
    st = jnp.stack([st[:, :, j * d:(j + 1) * d, j * d:(j + 1) * d] for j in range(hp)], axis=2)
    return rw, st.reshape(n, npairs * hp, d, d)


def _rwkv_sample(rkv, low, prev_rkv, prev_low, state0, par, ln_w, ln_b):
    db = rkv.shape[0]
    w = rkv.shape[1] // 3
    d = RWKV_HEAD_DIM
    nh = w // d
    ins = (rkv, low, prev_rkv, prev_low, *par)
    feats = pl.pallas_call(
        _rwkv_feat_body,
        in_specs=[_full_spec(t) for t in ins],
        out_specs=[pl.BlockSpec((db, w), lambda: (0, 0))] * 8,
        out_shape=[jax.ShapeDtypeStruct((db, w), F32)] * 8,
        compiler_params=pltpu.CompilerParams(vmem_limit_bytes=VMEM_LIMIT),
        name="rwkv_feats",
    )(*ins)
    hv = lambda t: t.reshape(db, nh, 1, d)
    row = pl.BlockSpec((None, nh, 1, d), lambda b: (b, 0, 0, 0))
    prow = pl.BlockSpec((nh, 1, d), lambda b: (0, 0, 0))
    mat = pl.BlockSpec((None, nh, d, d), lambda b: (b, 0, 0, 0))
    rw, s1 = pl.pallas_call(
        functools.partial(_wkv_step_body, nh=nh),
        grid=(db,),
        in_specs=[row] * 8 + [prow, prow, mat],
        out_specs=[row, mat],
        out_shape=[jax.ShapeDtypeStruct((db, nh, 1, d), BF16), jax.ShapeDtypeStruct((db, nh, d, d), F32)],
        compiler_params=_cparams(("parallel",)),
        name="wkv_step",
    )(*[hv(t) for t in feats], ln_w.reshape(nh, 1, d), ln_b.reshape(nh, 1, d), state0)
    return rw.reshape(db, w), s1


def _merge_body(attn_ref, rw_ref, ga_ref, gb_ref, wa_ref, wr_ref, o_ref):
    ya = _dg(attn_ref[...], wa_ref[...])
    yb = _dg(rw_ref[...], wr_ref[...])
    o_ref[...] = (ga_ref[...] * ya + gb_ref[...] * yb).astype(o_ref.dtype)


def _merge(attn, rw, gates, wa, wr, tm, tn):
    m, ka = attn.shape
    kr = rw.shape[1]
    d = wa.shape[1]
    nj = d // tn
    return pl.pallas_call(
        _merge_body,
        grid=(m // tm, nj),
        in_specs=[pl.BlockSpec((tm, ka), lambda i, j: (i, 0)), pl.BlockSpec((tm, kr), lambda i, j: (i, 0)),
                  pl.BlockSpec((tm, tn), lambda i, j: (i, j)), pl.BlockSpec((tm, tn), lambda i, j: (i, nj + j)),
                  pl.BlockSpec((ka, tn), lambda i, j: (0, j)), pl.BlockSpec((kr, tn), lambda i, j: (0, j))],
        out_specs=pl.BlockSpec((tm, tn), lambda i, j: (i, j)),
        out_shape=jax.ShapeDtypeStruct((m, d), BF16),
        compiler_params=_cparams(("parallel", "parallel")),
        name="merge",
    )(attn, rw, gates, gates, wa, wr)


def _resid_mm_body(a_ref, w_ref, x_ref, o_ref):
    o_ref[...] = x_ref[...] + _dg(a_ref[...], w_ref[...])


def _resid_matmul(a, w, x, tm, tn):
    m, k = a.shape
    n = w.shape[1]
    return pl.pallas_call(
        _resid_mm_body,
        grid=(m // tm, n // tn),
        in_specs=[pl.BlockSpec((tm, k), lambda i, j: (i, 0)), pl.BlockSpec((k, tn), lambda i, j: (0, j)),
                  pl.BlockSpec((tm, tn), lambda i, j: (i, j))],
        out_specs=pl.BlockSpec((tm, tn), lambda i, j: (i, j)),
        out_shape=jax.ShapeDtypeStruct((m, n), F32),
        compiler_params=_cparams(("parallel", "parallel")),
        name="out_proj",
    )(a, w, x)


def _ffn_up_body(x_ref, g_ref, wg_ref, wu_ref, o_ref, xn_ref):
    @pl.when(pl.program_id(1) == 0)
    def _():
        xn_ref[...] = _rms(x_ref[...], g_ref[...]).astype(BF16)

    xn = xn_ref[...]
    o_ref[...] = (jax.nn.silu(_dg(xn, wg_ref[...])) * _dg(xn, wu_ref[...])).astype(o_ref.dtype)


def _ffn_up(x, g, wg, wu, tm, tn):
    m, d = x.shape
    f = wg.shape[1]
    wspec = pl.BlockSpec((d, tn), lambda i, j: (0, j))
    return pl.pallas_call(
        _ffn_up_body,
        grid=(m // tm, f // tn),
        in_specs=[pl.BlockSpec((tm, d), lambda i, j: (i, 0)), pl.BlockSpec((1, d), lambda i, j: (0, 0)), wspec, wspec],
        out_specs=pl.BlockSpec((tm, tn), lambda i, j: (i, j)),
        out_shape=jax.ShapeDtypeStruct((m, f), BF16),
        scratch_shapes=[pltpu.VMEM((tm, d), BF16)],
        compiler_params=_cparams(("parallel", "arbitrary")),
        name="ffn_up",
    )(x, g, wg, wu)


def _ffn_down_body(h_ref, w_ref, x_ref, g_ref, o_ref, *, nj, tn):
    j = pl.program_id(1)
    val = x_ref[...] + _dg(h_ref[...], w_ref[...])
    for t in range(nj):
        @pl.when(j == t)
        def _(t=t):
            o_ref[:, t * tn:(t + 1) * tn] = val

    @pl.when(j == nj - 1)
    def _():
        o_ref[...] = _rms(o_ref[...], g_ref[...])


def _ffn_down(h, w, x, g, tm, tn):
    m, f = h.shape
    d = w.shape[1]
    nj = d // tn
    return pl.pallas_call(
        functools.partial(_ffn_down_body, nj=nj, tn=tn),
        grid=(m // tm, nj),
        in_specs=[pl.BlockSpec((tm, f), lambda i, j: (i, 0)), pl.BlockSpec((f, tn), lambda i, j: (0, j)),
                  pl.BlockSpec((tm, tn), lambda i, j: (i, j)), pl.BlockSpec((1, d), lambda i, j: (0, 0))],
        out_specs=pl.BlockSpec((tm, d), lambda i, j: (i, 0)),
        out_shape=jax.ShapeDtypeStruct((m, d), F32),
        compiler_params=_cparams(("parallel", "arbitrary")),
        name="ffn_down",
    )(h, w, x, g)


def _pick(total, pref):
    t = min(pref, total)
    while total % t:
        t //= 2
    return t


def _layer(x2, n, s, pos, wts, par, p, attend, rwkv):
    m, d = x2.shape
    att_w = wts['qkv'].shape[1] // 3
    rw_w = wts['rkv'].shape[1] // 3
    tm = _pick(s, 1024) if s % 8 == 0 else _pick(m, 1024)
    assert s % tm == 0 or tm % s == 0
    g_mix = p['g_mix'].reshape(1, d)

    cos_t, sin_a, sin_b = _rope_tables(pos)
    per = max(s // tm, 1)
    if s < tm:
        cos_t, sin_a, sin_b = (jnp.tile(t, (tm // s, 1)) for t in (cos_t, sin_a, sin_b))
    tab = pl.BlockSpec((tm, ATT_HEAD_DIM), lambda i, j: (i % per, 0))
    qkv, xn = _norm_matmul(x2, g_mix, wts['qkv'], tm, att_w, 'proj_qkv', _epi_rope, (cos_t, sin_a, sin_b), (tab, tab, tab))
    rkv = _matmul(xn, wts['rkv'], tm, rw_w, 'proj_rkv')
    low = _matmul(xn, wts['low'], tm, LOW_PAD, 'proj_low')
    tn_g = _pick(2 * d, 1024)
    gates = _matmul(xn, wts['gate'], tm, tn_g, 'proj_gates', _epi_sigmoid_bias, (p['b_gate'].reshape(1, 2 * d),),
                    (pl.BlockSpec((1, tn_g), lambda i, j: (0, j)),), out_dtype=BF16)

    attn = attend(qkv)
    rw, wkv_new = rwkv(rkv, low)

    merged = _merge(attn, rw, gates, wts['proj_attn'], wts['proj_rwkv'], tm, _pick(d, 1024))
    hid = _resid_matmul(merged, wts['out'], x2, tm, _pick(d, 1024))
    f = wts['ffn_gate'].shape[1]
    h = _ffn_up(hid, p['g_ffn'].reshape(1, d), wts['ffn_gate'], wts['ffn_up'], tm, _pick(f, 512))
    y = _ffn_down(h, wts['ffn_down'], hid, p['g_final'].reshape(1, d), _pick(m, 512), _pick(d, 1024))

    k = qkv[:, att_w:2 * att_w]
    v = qkv[:, 2 * att_w:]
    sh_last = jnp.concatenate([rkv.reshape(n, s, -1)[:, -1], low.reshape(n, s, -1)[:, -1, :LOW_W]], axis=1)
    return y, k, v, wkv_new, sh_last


def kernel(x_prompt, x_sample, cache_k, cache_v, state_wkv, state_shift, page_table, g_mix, w_in, b_gate, mu_shift, w0, w_decay_up, a0, w_iclr_up, w_gate_up, k_k, k_a, r_k, ln_x_w, ln_x_b, w_proj_attn, w_proj_rwkv, w_out, g_ffn, w_ffn_gate, w_ffn_up, w_ffn_down, g_final):
    depth = w_in.shape[0]
    assert depth == 1, "single-layer trunk"
    n, s, d = x_prompt.shape
    db, ds, _ = x_sample.shape
    assert ds == 1, "one new token per decode sequence"
    _, n_pool, page, nh, dh = cache_k.shape
    assert page == PAGE_SIZE and dh == ATT_HEAD_DIM
    att_w = nh * dh
    rw_w = w0.shape[1]
    n_pages = page_table.shape[1]
    past = n_pages * PAGE_SIZE
    assert s % MOBA_BLOCK == 0 and past % MOBA_BLOCK == 0
    l = 0

    wi = w_in[l]
    o = 3 * att_w
    wts = {
        'qkv': wi[:, :o].astype(BF16),
        'rkv': wi[:, o:o + 3 * rw_w].astype(BF16),
        'low': jnp.pad(wi[:, o + 3 * rw_w:o + 3 * rw_w + LOW_W], ((0, 0), (0, LOW_PAD - LOW_W))).astype(BF16),
        'gate': wi[:, o + 3 * rw_w + LOW_W:].astype(BF16),
        'proj_attn': w_proj_attn[l].astype(BF16), 'proj_rwkv': w_proj_rwkv[l].astype(BF16),
        'out': w_out[l].astype(BF16), 'ffn_gate': w_ffn_gate[l].astype(BF16),
        'ffn_up': w_ffn_up[l].astype(BF16), 'ffn_down': w_ffn_down[l].astype(BF16),
    }
    p = {'g_mix': g_mix[l], 'b_gate': b_gate[l], 'g_ffn': g_ffn[l], 'g_final': g_final,
         'mu_shift': mu_shift[l], 'w0': w0[l], 'w_decay_up': w_decay_up[l], 'a0': a0[l],
         'w_iclr_up': w_iclr_up[l], 'w_gate_up': w_gate_up[l], 'k_k': k_k[l], 'k_a': k_a[l], 'r_k': r_k[l]}
    par = _rwkv_params(p, rw_w)
    ln_w, ln_b = ln_x_w[l], ln_x_b[l]

    blk_sums = []

    def attend_prompt(qkv):
        attn, bs = _moba_prompt(qkv, n, s, page_table, cache_k, l)
        blk_sums.append(bs)
        return attn

    yp, kp, vp, wp, sp = _layer(
        x_prompt.reshape(n * s, d), n, s, jnp.arange(s), wts, par, p, attend_prompt,
        lambda rkv, low: _rwkv_prompt(rkv, low, par, ln_w, ln_b, n, s))

    q_blk = past // MOBA_BLOCK

    def attend_sample(qkv):
        q, k_new, v_new = qkv[:, :att_w], qkv[:, att_w:2 * att_w], qkv[:, 2 * att_w:]
        blk_sum = blk_sums[0].reshape(db, q_blk, att_w)
        sel = _decode_select(blk_sum, q, q_blk)
        return _decode_attn(page_table, sel, q, k_new, v_new, cache_k, cache_v, l)

    sh_prev = state_shift[l]
    prev_rkv = sh_prev[:, :3 * rw_w]
    prev_low = jnp.pad(sh_prev[:, 3 * rw_w:], ((0, 0), (0, LOW_PAD - LOW_W)))
    ys, ks_, vs_, ws_, ss_ = _layer(
        x_sample.reshape(db * ds, d), db, ds, past + jnp.arange(ds), wts, par, p,
        attend_sample,
        lambda rkv, low: _rwkv_sample(rkv, low, prev_rkv, prev_low, state_wkv[l], par, ln_w, ln_b))

    return (yp.reshape(n, s, d), ys.reshape(db, ds, d),
            kp.reshape(1, n, s, nh, dh), vp.reshape(1, n, s, nh, dh),
            wp.astype(state_wkv.dtype)[None], sp.astype(state_shift.dtype)[None],
            ks_.reshape(1, db, ds, nh, dh), vs_.reshape(1, db, ds, nh, dh),
            ws_.astype(state_wkv.dtype)[None], ss_.astype(state_shift.dtype)[None])
```

```python
import functools
import math
from typing import NamedTuple

import numpy as np
import jax
import jax.numpy as jnp
from jax import lax
from jax.experimental import pallas as pl
from jax.experimental.pallas import tpu as pltpu

F32 = jnp.float32
BF16 = jnp.bfloat16

ATT_HEAD_DIM = 128
ROT_DIM = ATT_HEAD_DIM // 4
ROPE_THETA = 500000.0
MOBA_BLOCK = 256
MOBA_TOP_K = 3
PAGE_SIZE = 128
RWKV_HEAD_DIM = 64
DECAY_RANK = 64
ICLR_RANK = 64
GATE_RANK = 160
RMS_EPS = 1e-6
GN_EPS = 64e-5
L2_EPS = 1e-12
LOG2E = math.log2(math.e)

LANES = 128
LOW_W = DECAY_RANK + ICLR_RANK + GATE_RANK
LOW_PAD = -(-LOW_W // LANES) * LANES
WKV_CHUNK = 64
VMEM_LIMIT = 56 * 1024 * 1024

NN = (((1,), (0,)), ((), ()))
NT = (((1,), (1,)), ((), ()))
TN = (((0,), (0,)), ((), ()))


def _cparams(sem):
    return pltpu.CompilerParams(dimension_semantics=sem, vmem_limit_bytes=VMEM_LIMIT)


def _dg(a, b, dims=NN):
    return lax.dot_general(a, b, dims, preferred_element_type=F32)


def _split2(x):
    hi = x.astype(BF16)
    lo = (x - hi.astype(F32)).astype(BF16)
    return hi, lo


def _dot3(a, b, dims=NN):
    ah, al = _split2(a)
    bh, bl = _split2(b)
    return _dg(ah, bh, dims) + (_dg(ah, bl, dims) + _dg(al, bh, dims))


def _dot3_pre(a, bh, bl, dims=NN):
    ah, al = _split2(a)
    return _dg(ah, bh, dims) + (_dg(ah, bl, dims) + _dg(al, bh, dims))


def _dot_exact_rhs(a, b_bf16, dims=NN):
    hi, lo = _split2(a)
    return _dg(hi, b_bf16, dims) + _dg(lo, b_bf16, dims)


def _dot_exact_lhs(a_bf16, b, dims=NN):
    hi, lo = _split2(b)
    return _dg(a_bf16, hi, dims) + _dg(a_bf16, lo, dims)


def _rms(x, g):
    return x * lax.rsqrt(jnp.mean(x * x, axis=-1, keepdims=True) + RMS_EPS) * g


def _proj_qkv_body(x_ref, g_ref, w_ref, cos_ref, sa_ref, sb_ref, o_ref, xn_ref, k3_ref, v3_ref):
    j = pl.program_id(1)

    @pl.when(j == 0)
    def _():
        xn_ref[...] = _rms(x_ref[...], g_ref[...]).astype(BF16)

    val = _epi_rope(_dg(xn_ref[...], w_ref[...]), j, cos_ref, sa_ref, sb_ref)
    o_ref[...] = val

    @pl.when(j == 1)
    def _():
        k3_ref[...] = val.reshape(k3_ref.shape)

    @pl.when(j == 2)
    def _():
        v3_ref[...] = val.reshape(v3_ref.shape)


def _mm_body(*refs, epilogue, n_aux):
    a_ref, w_ref = refs[:2]
    aux = refs[2:2 + n_aux]
    o_ref, = refs[2 + n_aux:]
    o_ref[...] = epilogue(_dg(a_ref[...], w_ref[...]), pl.program_id(1), *aux).astype(o_ref.dtype)


def _epi_none(acc, j):
    return acc


def _epi_rope(acc, j, cos_ref, sa_ref, sb_ref):
    c, sa, sb = cos_ref[...], sa_ref[...], sb_ref[...]
    half = ROT_DIM // 2
    outs = []
    for h in range(acc.shape[1] // ATT_HEAD_DIM):
        xh = acc[:, h * ATT_HEAD_DIM:(h + 1) * ATT_HEAD_DIM]
        outs.append(xh * c + pltpu.roll(xh, ATT_HEAD_DIM - half, 1) * sa + pltpu.roll(xh, half, 1) * sb)
    rot = jnp.concatenate(outs, axis=1)
    rot = rot * jnp.where(j == 0, ATT_HEAD_DIM ** -0.5, 1.0).astype(F32)
    return jnp.where(j < 2, rot, acc)


def _epi_sigmoid_bias(acc, j, b_ref):
    return jax.nn.sigmoid(acc + b_ref[...])


def _proj_qkv(x, g, w, tabs, tm, period):
    m, d = x.shape
    att_w = w.shape[1] // 3
    nh = att_w // ATT_HEAD_DIM
    tab = pl.BlockSpec((tm, ATT_HEAD_DIM), lambda i, j: (i % period, 0))
    heads = pl.BlockSpec((tm, nh, ATT_HEAD_DIM), lambda i, j: (i, 0, 0))
    return pl.pallas_call(
        _proj_qkv_body,
        grid=(m // tm, 3),
        in_specs=[pl.BlockSpec((tm, d), lambda i, j: (i, 0)),
                  pl.BlockSpec((1, d), lambda i, j: (0, 0)),
                  pl.BlockSpec((d, att_w), lambda i, j: (0, j)), tab, tab, tab],
        out_specs=[pl.BlockSpec((tm, att_w), lambda i, j: (i, j)), pl.BlockSpec((tm, d), lambda i, j: (i, 0)),
                   heads, heads],
        out_shape=[jax.ShapeDtypeStruct((m, 3 * att_w), F32), jax.ShapeDtypeStruct((m, d), BF16),
                   jax.ShapeDtypeStruct((m, nh, ATT_HEAD_DIM), F32), jax.ShapeDtypeStruct((m, nh, ATT_HEAD_DIM), F32)],
        compiler_params=_cparams(("parallel", "arbitrary")),
        name="proj_qkv",
    )(x, g, w, *tabs)


def _matmul(a, w, tm, tn, name, epilogue=_epi_none, aux=(), aux_specs=(), out_dtype=F32):
    m, k = a.shape
    n = w.shape[1]
    return pl.pallas_call(
        functools.partial(_mm_body, epilogue=epilogue, n_aux=len(aux)),
        grid=(m // tm, n // tn),
        in_specs=[pl.BlockSpec((tm, k), lambda i, j: (i, 0)), pl.BlockSpec((k, tn), lambda i, j: (0, j)), *aux_specs],
        out_specs=pl.BlockSpec((tm, tn), lambda i, j: (i, j)),
        out_shape=jax.ShapeDtypeStruct((m, n), out_dtype),
        compiler_params=_cparams(("parallel", "parallel")),
        name=name,
    )(a, w, *aux)


def _rope_tables(pos):
    half = ROT_DIM // 2
    inv = jnp.exp(jnp.arange(half, dtype=F32) * (-2.0 * math.log(ROPE_THETA) / ROT_DIM))
    ang = pos.astype(F32)[:, None] * inv[None, :]
    cos, sin = jnp.cos(ang), jnp.sin(ang)
    s = pos.shape[0]
    z = lambda w: jnp.zeros((s, w), F32)
    cos_t = jnp.concatenate([cos, cos, jnp.ones((s, ATT_HEAD_DIM - ROT_DIM), F32)], axis=1)
    sin_a = jnp.concatenate([-sin, z(ATT_HEAD_DIM - half)], axis=1)
    sin_b = jnp.concatenate([z(half), sin, z(ATT_HEAD_DIM - ROT_DIM)], axis=1)
    return cos_t, sin_a, sin_b


def _moba_prompt_body(pt_ref, q_ref, k_ref, v_ref, ck_ref, o_ref, bs_ref, km_ref, kb_ref, vt_ref, s_ref,
                      pbuf, psum, psem, osem, *, nb, nh, qpb, pg):
    qp = pl.program_id(2)
    blk = MOBA_BLOCK
    sub = 8
    g = (pl.program_id(0) * nh + pl.program_id(1)) * (nb // qpb) + qp
    slot = g % 2

    def page_copies(step, slot_):
        b = step // pg.per_seq
        p0 = (step % pg.per_seq) * pg.pps
        return [pltpu.make_async_copy(ck_ref.at[pg.layer, pt_ref[b, p0 + j]], pbuf.at[slot_, j], psem.at[slot_])
                for j in range(pg.pps)]

    @pl.when(g == 0)
    def _():
        for c in page_copies(0, 0):
            c.start()

    @pl.when(g + 1 < pg.groups)
    def _():
        for c in page_copies(g + 1, 1 - slot):
            c.start()

    @pl.when(qp == 0)
    def _():
        km_ref[...] = jnp.zeros_like(km_ref)
        for j in range(nb):
            kj = k_ref[j * blk:(j + 1) * blk, :]
            km_ref[j:j + 1, :] = jnp.mean(kj, axis=0, keepdims=True)
            kb_ref[j] = kj.astype(BF16)
            vt_ref[j] = v_ref[j * blk:(j + 1) * blk, :].T.astype(BF16)

    q = q_ref[...]
    gate = _dot3(km_ref[...], q, NT)
    rowi = lax.broadcasted_iota(jnp.int32, gate.shape, 0)
    qblk = qp * qpb + lax.broadcasted_iota(jnp.int32, gate.shape, 1) // blk
    valid = rowi < qblk
    gm = jnp.where(valid, gate, -jnp.inf)
    cnt = jnp.zeros(gate.shape, F32)
    for i in range(nb):
        gi = gm[i:i + 1, :]
        cnt = cnt + jnp.where(gi > gm, 1.0, 0.0) + jnp.where(gi == gm, (rowi > i).astype(F32), 0.0)
    bias = jnp.where(valid, jnp.where(cnt < MOBA_TOP_K, 0.0, -jnp.inf), -jnp.inf)

    qh = (q * LOG2E).astype(BF16)
    keyi = lax.broadcasted_iota(jnp.int32, (blk, blk), 0)
    qi = lax.broadcasted_iota(jnp.int32, (blk, blk), 1)
    fold = lambda t, op: op(t.reshape(blk // sub, sub, blk), axis=0)

    def sweep(qbv, u):
        cols = slice(u * blk, (u + 1) * blk)
        qu = qh[cols, :]
        pm = None
        for j in range(qbv + 1):
            s = _dg(kb_ref[j], qu, NT)
            s = jnp.where(keyi <= qi, s, -jnp.inf) if j == qbv else s + bias[j:j + 1, cols]
            s_ref[u, j] = s
            t = fold(s, jnp.max)
            pm = t if pm is None else jnp.maximum(pm, t)
        m = jnp.max(pm, axis=0, keepdims=True)
        lp = jnp.zeros((sub, blk), F32)
        acc = jnp.zeros((ATT_HEAD_DIM, blk), F32)
        for j in range(qbv + 1):
            p = jnp.exp2(s_ref[u, j] - m)
            lp = lp + fold(p, jnp.sum)
            acc = acc + _dg(vt_ref[j], p.astype(BF16))
        l = jnp.sum(lp, axis=0, keepdims=True)
        o_ref[cols, :] = (acc / l).T.astype(o_ref.dtype)

    def group(gv):
        for u in range(qpb):
            sweep(gv * qpb + u, u)

    for gv in range(nb // qpb):
        pl.when(qp == gv)(functools.partial(group, gv))

    nbs = pg.pps // pg.ppb

    def sums_out(step, slot_):
        dst = bs_ref.at[step // pg.per_seq, pl.ds((step % pg.per_seq) * nbs, nbs)]
        return pltpu.make_async_copy(psum.at[slot_], dst, osem.at[slot_])

    @pl.when(g < pg.groups)
    def _():
        for c in page_copies(g, slot):
            c.wait()

        @pl.when(g >= 2)
        def _():
            sums_out(g - 2, slot).wait()

        for j in range(nbs):
            acc = jnp.sum(pbuf[slot, j * pg.ppb], axis=0)
            for t in range(1, pg.ppb):
                acc = acc + jnp.sum(pbuf[slot, j * pg.ppb + t], axis=0)
            psum[slot, j] = acc
        sums_out(g, slot).start()

        @pl.when(g == pg.groups - 1)
        def _():
            sums_out(g, slot).wait()

            @pl.when(g >= 1)
            def _():
                sums_out(g - 1, 1 - slot).wait()


class _PageGroups(NamedTuple):
    pps: int
    ppb: int
    per_seq: int
    groups: int
    layer: int


def _moba_prompt(qkv, n, s, page_table, cache_k, layer):
    att_w = qkv.shape[1] // 3
    nh = att_w // ATT_HEAD_DIM
    nb = s // MOBA_BLOCK
    nbp = -(-nb // 8) * 8
    db, n_pages = page_table.shape
    _, _, page, ch, dh = cache_k.shape
    ppb = MOBA_BLOCK // PAGE_SIZE
    qpb = 2 if nb % 2 == 0 else 1
    nq = nb // qpb
    nsteps = n * nh * nq
    pps = ppb
    while (db * n_pages // pps > nsteps or pps * 2 <= 8) and n_pages % (pps * 2) == 0:
        pps *= 2
    pg = _PageGroups(pps, ppb, n_pages // pps, db * n_pages // pps, layer)
    assert pg.groups <= nsteps, "more page groups than attention grid steps"
    gs = pltpu.PrefetchScalarGridSpec(
        num_scalar_prefetch=1, grid=(n, nh, nq),
        in_specs=[pl.BlockSpec((qpb * MOBA_BLOCK, ATT_HEAD_DIM), lambda b, h, i, pt: (b * nq + i, h)),
                  pl.BlockSpec((s, ATT_HEAD_DIM), lambda b, h, i, pt: (b, nh + h)),
                  pl.BlockSpec((s, ATT_HEAD_DIM), lambda b, h, i, pt: (b, 2 * nh + h)),
                  pl.BlockSpec(memory_space=pl.ANY)],
        out_specs=[pl.BlockSpec((qpb * MOBA_BLOCK, ATT_HEAD_DIM), lambda b, h, i, pt: (b * nq + i, h)),
                   pl.BlockSpec(memory_space=pl.ANY)],
        scratch_shapes=[pltpu.VMEM((nbp, ATT_HEAD_DIM), F32),
                        pltpu.VMEM((nb, MOBA_BLOCK, ATT_HEAD_DIM), BF16),
                        pltpu.VMEM((nb, ATT_HEAD_DIM, MOBA_BLOCK), BF16),
                        pltpu.VMEM((qpb, nb, MOBA_BLOCK, MOBA_BLOCK), F32),
                        pltpu.VMEM((2, pps, page, ch, dh), F32),
                        pltpu.VMEM((2, pps // ppb, ch, dh), F32),
                        pltpu.SemaphoreType.DMA((2,)), pltpu.SemaphoreType.DMA((2,))])
    return pl.pallas_call(
        functools.partial(_moba_prompt_body, nb=nb, nh=nh, qpb=qpb, pg=pg),
        grid_spec=gs,
        out_shape=[jax.ShapeDtypeStruct((n * s, att_w), BF16),
                   jax.ShapeDtypeStruct((db, n_pages // ppb, ch, dh), F32)],
        compiler_params=_cparams(("arbitrary", "arbitrary", "arbitrary")),
        name="moba_prompt",
    )(page_table, qkv, qkv, qkv, cache_k)


def _ring_step(copies, g, nsteps):
    slot = g % 2

    @pl.when(g == 0)
    def _():
        for c in copies(0, 0):
            c.start()

    @pl.when(g + 1 < nsteps)
    def _():
        for c in copies(g + 1, 1 - slot):
            c.start()

    for c in copies(g, slot):
        c.wait()
    return slot


def _decode_select_body(bs_ref, q_ref, sel_ref, *, nbp, q_blk, nh):
    bs = bs_ref[...]
    q = q_ref[...]
    it = lax.broadcasted_iota(jnp.int32, (nbp, 1), 0)
    for h in range(nh):
        sl = slice(h * ATT_HEAD_DIM, (h + 1) * ATT_HEAD_DIM)
        g = jnp.sum(q[:, sl] * (bs[:, sl] / MOBA_BLOCK), axis=1, keepdims=True)
        g = jnp.where(it < q_blk, g, -jnp.inf)
        for t in range(MOBA_TOP_K):
            mx = jnp.max(g, axis=0, keepdims=True)
            idx = jnp.min(jnp.where(g == mx, it, nbp), axis=0, keepdims=True)
            sel_ref[t:t + 1, h:h + 1] = jnp.where(mx > -jnp.inf, idx, -1)
            g = jnp.where(it == idx, -jnp.inf, g)


def _decode_select(blk_sum, q, q_blk):
    db, nbp, att_w = blk_sum.shape
    nh = att_w // ATT_HEAD_DIM
    return pl.pallas_call(
        functools.partial(_decode_select_body, nbp=nbp, q_blk=q_blk, nh=nh),
        grid=(db,),
        in_specs=[pl.BlockSpec((None, nbp, att_w), lambda b: (b, 0, 0)),
                  pl.BlockSpec((None, 1, att_w), lambda b: (b, 0, 0))],
        out_specs=pl.BlockSpec((None, MOBA_TOP_K, nh), lambda b: (b, 0, 0)),
        out_shape=jax.ShapeDtypeStruct((db, MOBA_TOP_K, nh), jnp.int32),
        compiler_params=_cparams(("parallel",)),
        name="decode_select",
    )(blk_sum, q.reshape(db, 1, att_w))


def _decode_attn_body(pt_ref, sel_ref, q_ref, kn_ref, vn_ref, ck_ref, cv_ref, o_ref, kbuf, vbuf, sem,
                      *, ppb, nh, hps, nsteps, layer):
    g = pl.program_id(0)
    npg = MOBA_TOP_K * ppb
    hgroups = nh // hps

    def copies(step, slot):
        b, h0 = step // hgroups, (step % hgroups) * hps
        cps = []
        for u in range(hps):
            for t in range(npg):
                blk = jnp.maximum(sel_ref[(b * MOBA_TOP_K + t // ppb) * nh + h0 + u], 0)
                page = pt_ref[b, blk * ppb + t % ppb]
                cps.append(pltpu.make_async_copy(ck_ref.at[layer, page, :, h0 + u, :], kbuf.at[slot, u, t], sem.at[0, slot]))
                cps.append(pltpu.make_async_copy(cv_ref.at[layer, page, :, h0 + u, :], vbuf.at[slot, u, t], sem.at[1, slot]))
        return cps

    slot = _ring_step(copies, g, nsteps)
    b, h0 = g // hgroups, (g % hgroups) * hps
    for u in range(hps):
        lanes = slice(u * ATT_HEAD_DIM, (u + 1) * ATT_HEAD_DIM)
        q = q_ref[:, lanes]
        s_own = jnp.sum(q * kn_ref[:, lanes], axis=1, keepdims=True)
        m = s_own
        ss = []
        for t in range(npg):
            ok = sel_ref[(b * MOBA_TOP_K + t // ppb) * nh + h0 + u] >= 0
            st = jnp.where(ok, jnp.sum(kbuf[slot, u, t] * q, axis=1, keepdims=True), -jnp.inf)
            ss.append(st)
            m = jnp.maximum(m, jnp.max(st, axis=0, keepdims=True))
        p_own = jnp.exp(s_own - m)
        l = p_own
        acc = p_own * vn_ref[:, lanes]
        for t in range(npg):
            p = jnp.exp(ss[t] - m)
            l = l + jnp.sum(p, axis=0, keepdims=True)
            acc = acc + jnp.sum(p * vbuf[slot, u, t], axis=0, keepdims=True)
        o_ref[:, lanes] = (acc / l).astype(o_ref.dtype)


def _decode_attn(page_table, sel, q, k_new, v_new, cache_k, cache_v, layer):
    db, att_w = q.shape
    nh = att_w // ATT_HEAD_DIM
    ppb = MOBA_BLOCK // PAGE_SIZE
    npg = MOBA_TOP_K * ppb
    hps = next(c for c in (4, 2, 1) if nh % c == 0)
    hgroups = nh // hps
    nsteps = db * hgroups
    row = pl.BlockSpec((None, 1, hps * ATT_HEAD_DIM), lambda g, pt, sl: (g // hgroups, 0, g % hgroups))
    hbm = pl.BlockSpec(memory_space=pl.ANY)
    gs = pltpu.PrefetchScalarGridSpec(
        num_scalar_prefetch=2, grid=(nsteps,),
        in_specs=[row, row, row, hbm, hbm],
        out_specs=row,
        scratch_shapes=[pltpu.VMEM((2, hps, npg, PAGE_SIZE, ATT_HEAD_DIM), F32),
                        pltpu.VMEM((2, hps, npg, PAGE_SIZE, ATT_HEAD_DIM), F32),
                        pltpu.SemaphoreType.DMA((2, 2))])
    r3 = lambda a: a.reshape(db, 1, att_w)
    out = pl.pallas_call(
        functools.partial(_decode_attn_body, ppb=ppb, nh=nh, hps=hps, nsteps=nsteps, layer=layer), grid_spec=gs,
        out_shape=jax.ShapeDtypeStruct((db, 1, att_w), BF16),
        compiler_params=_cparams(("arbitrary",)),
        name="decode_attn",
    )(page_table, sel.reshape(-1), r3(q), r3(k_new), r3(v_new), cache_k, cache_v)
    return out.reshape(db, att_w)


def _seg_sum(x, e_bf16):
    outs = [_dot_exact_rhs(x[:, c:c + LANES], e_bf16) for c in range(0, x.shape[1], LANES)]
    return outs[0] if len(outs) == 1 else jnp.concatenate(outs, axis=1)


def _rwkv_features(rkv, low, prev_rkv, prev_low, par, single_pass):
    (mu_rkv, mu_low, w0, a0, k_k, k_a, r_k, wd_h, wd_l, wi_h, wi_l, wg_h, wg_l, e128) = par
    w = w0.shape[1]
    mix = rkv + mu_rkv * (prev_rkv - rkv)
    mlow = low + mu_low * (prev_low - low)
    r, k, v = mix[:, :w], mix[:, w:2 * w], mix[:, 2 * w:]
    lane = lax.broadcasted_iota(jnp.int32, mlow.shape, 1)
    feat = jnp.where(lane < DECAY_RANK, jnp.tanh(mlow),
                     jnp.where(lane < DECAY_RANK + ICLR_RANK, mlow,
                               jnp.where(lane < LOW_W, jax.nn.sigmoid(mlow), 0.0)))
    z = w0 + _dot3_pre(feat, wd_h, wd_l)
    nz = -z
    w_log = -(jnp.maximum(nz, 0.0) + jnp.log1p(jnp.exp(-jnp.abs(nz)))) - 0.5
    logw = -jnp.exp(w_log)
    if single_pass:
        fb = feat.astype(BF16)
        up_i, up_g = _dg(fb, wi_h), _dg(fb, wg_h)
    else:
        up_i, up_g = _dot3_pre(feat, wi_h, wi_l), _dot3_pre(feat, wg_h, wg_l)
    a = jax.nn.sigmoid(a0 + up_i)
    g = up_g
    kk = k * k_k
    kk = kk / jnp.maximum(jnp.sqrt(_seg_sum(kk * kk, e128)), L2_EPS)
    k2 = k * (1.0 + (a - 1.0) * k_a)
    bonus = _seg_sum(r * k2 * r_k, e128) * v
    return r, k2, v, kk, a, logw, g, bonus


_N_PAR = 14


def _rwkv_prep_body(*refs, tm, chunk, tiles_per_seq):
    rkv_ref, low_ref, prkv_ref, plow_ref, frkv_ref, flow_ref, ltri_ref = refs[:7]
    par = tuple(r[...] for r in refs[7:7 + _N_PAR])
    at_ref, bt_ref, kt_ref, rt_ref, v_ref, g_ref, bon_ref, pt_ref = refs[7 + _N_PAR:]
    i = pl.program_id(0)
    rkv, low = rkv_ref[...], low_ref[...]
    seq_start = i % tiles_per_seq == 0

    def prev_of(x, tail_ref, first_ref):
        first = jnp.where(seq_start, first_ref[...], tail_ref[7:8, :])
        rowi = lax.broadcasted_iota(jnp.int32, x.shape, 0)
        return jnp.where(rowi == 0, first, pltpu.roll(x, 1, 0))

    prev_rkv = prev_of(rkv, prkv_ref, frkv_ref)
    prev_low = prev_of(low, plow_ref, flow_ref)
    r, k2, v, kk, a, logw, g, bonus = _rwkv_features(rkv, low, prev_rkv, prev_low, par, True)
    cum = _dot_exact_lhs(ltri_ref[...], logw)
    e_in = jnp.exp(cum)
    e_out = jnp.exp(-cum)
    at_ref[...] = (-kk * jnp.exp(cum - logw)).astype(at_ref.dtype)
    bt_ref[...] = (kk * a * e_out).astype(bt_ref.dtype)
    kt_ref[...] = (k2 * e_out).astype(kt_ref.dtype)
    rt_ref[...] = (r * e_in).astype(rt_ref.dtype)
    v_ref[...] = v.astype(v_ref.dtype)
    g_ref[...] = g
    bon_ref[...] = bonus
    for c in range(tm // chunk):
        pt_ref[c] = e_in[(c + 1) * chunk - 1:(c + 1) * chunk, :]


def _wkv_chunk_body(at_ref, bt_ref, kt_ref, rt_ref, v_ref, pt_ref, g_ref, bon_ref, lnw_ref, lnb_ref, e_ref,
                    rw_ref, st_ref, s_scr, *, nc, chunk, npairs):
    T = chunk
    hp = LANES // RWKV_HEAD_DIM
    W = hp * T

    @pl.when(pl.program_id(1) == 0)
    def _():
        s_scr[...] = jnp.zeros_like(s_scr)

    lane_head = lax.broadcasted_iota(jnp.int32, (1, 1, LANES), 2) // RWKV_HEAD_DIM
    ri = lax.broadcasted_iota(jnp.int32, (1, W, W), 1)
    ci = lax.broadcasted_iota(jnp.int32, (1, W, W), 2)
    eye = jnp.where(ri == ci, 1.0, 0.0).astype(F32)
    zero = jnp.zeros((), BF16)
    e1 = functools.partial(jnp.einsum, preferred_element_type=F32)
    gram = functools.partial(e1, 'bid,bjd->bij')
    mm = functools.partial(e1, 'bij,bjd->bid')
    mm_nt = functools.partial(e1, 'bwk,bvk->bwv')
    bf = lambda t: t.astype(BF16)
    e128 = e_ref[...]
    lnw, lnb = lnw_ref[...], lnb_ref[...]

    def by_pair(x):
        return jnp.stack([x[:, p * LANES:(p + 1) * LANES] for p in range(npairs)], axis=0)

    def stacked(ref, rows):
        x = by_pair(ref[rows, :])
        return jnp.concatenate([jnp.where(lane_head == h, x, zero) for h in range(hp)], axis=1)

    def chunk_step(c, carry):
        rows = pl.ds(pl.multiple_of(c * T, T), T)
        a2, b2, k2, r2, v2 = (stacked(r, rows) for r in (at_ref, bt_ref, kt_ref, rt_ref, v_ref))
        bk = jnp.concatenate([b2, k2], axis=1)
        ga = gram(a2, bk)
        gr = gram(r2, bk)
        gab = jnp.where(ci < ri, ga[:, :, :W], 0.0)
        gak = bf(jnp.where(ci < ri, ga[:, :, W:], 0.0))
        grbk = bf(jnp.concatenate([jnp.where(ci <= ri, gr[:, :, :W], 0.0),
                                   jnp.where(ci <= ri, gr[:, :, W:], 0.0)], axis=2))
        x = eye + gab
        lp = gab
        span = 2
        while span < T:
            lp = mm(bf(lp), bf(lp))
            x = mm(bf(x), bf(eye + lp))
            span *= 2
        aw = jnp.concatenate([a2, bf(mm(gak, v2))], axis=2)
        xa = mm(bf(x), aw)
        ahat = bf(xa[:, :, :LANES])
        uhat = xa[:, :, LANES:]

        s = s_scr[...]
        sb = bf(s)
        u = mm_nt(ahat, sb) + uhat
        uv = jnp.concatenate([bf(u), v2], axis=1)
        y2 = mm_nt(r2, sb) + mm(grbk, uv)
        s_scr[...] = (s + e1('bwv,bwk->bvk', uv, bk)) * by_pair(pt_ref[c])

        y3 = y2[:, :T]
        for h in range(1, hp):
            y3 = y3 + y2[:, h * T:(h + 1) * T]
        y = jnp.concatenate([y3[p] for p in range(npairs)], axis=1)
        mu = _seg_sum(y, e128) / RWKV_HEAD_DIM
        yc = y - mu
        var = _seg_sum(yc * yc, e128) / RWKV_HEAD_DIM
        yn = yc * lax.rsqrt(var + GN_EPS) * lnw + lnb
        rw_ref[rows, :] = ((yn + bon_ref[rows, :]) * g_ref[rows, :]).astype(rw_ref.dtype)
        return carry

    lax.fori_loop(0, nc, chunk_step, 0)
    st_ref[...] = s_scr[...]


def _wkv_step_body(r_ref, k_ref, v_ref, kk_ref, a_ref, lw_ref, g_ref, bon_ref, lnw_ref, lnb_ref, s0_ref,
                   rw_ref, s1_ref, *, nh):
    d = s0_ref.shape[-1]
    eye = lax.broadcasted_iota(jnp.int32, (d, d), 0) == lax.broadcasted_iota(jnp.int32, (d, d), 1)
    for h in range(nh):
        s0 = s0_ref[h]
        r, k, v, kk, a = r_ref[h], k_ref[h], v_ref[h], kk_ref[h], a_ref[h]
        v_col = jnp.sum(jnp.where(eye, v, 0.0), axis=1, keepdims=True)
        sa = jnp.sum(s0 * (-kk), axis=1, keepdims=True)
        s1 = s0 * jnp.exp(lw_ref[h]) + sa * (kk * a) + v_col * k
        s1_ref[h] = s1
        y_col = jnp.sum(s1 * r, axis=1, keepdims=True)
        y = jnp.sum(jnp.where(eye, y_col, 0.0), axis=0, keepdims=True)
        mu = jnp.mean(y, axis=1, keepdims=True)
        var = jnp.mean(jnp.square(y - mu), axis=1, keepdims=True)
        yn = (y - mu) * lax.rsqrt(var + GN_EPS) * lnw_ref[h] + lnb_ref[h]
        rw_ref[h] = ((yn + bon_ref[h]) * g_ref[h]).astype(rw_ref.dtype)


def _rwkv_feat_body(*refs):
    rkv_ref, low_ref, prkv_ref, plow_ref = refs[:4]
    par = tuple(r[...] for r in refs[4:4 + _N_PAR])
    outs = refs[4 + _N_PAR:]
    vals = _rwkv_features(rkv_ref[...], low_ref[...], prkv_ref[...], plow_ref[...], par, False)
    for o, x in zip(outs, vals):
        o[...] = x


def _rwkv_params(p, rwkv_w):
    def pad_up(wt, r0):
        full = jnp.zeros((LOW_PAD, rwkv_w), F32).at[r0:r0 + wt.shape[0]].set(wt)
        hi = full.astype(BF16)
        return hi, (full - hi.astype(F32)).astype(BF16)

    wd_h, wd_l = pad_up(p['w_decay_up'], 0)
    wi_h, wi_l = pad_up(p['w_iclr_up'], DECAY_RANK)
    wg_h, wg_l = pad_up(p['w_gate_up'], DECAY_RANK + ICLR_RANK)
    hd = np.arange(LANES) // RWKV_HEAD_DIM
    e128 = jnp.asarray(hd[:, None] == hd[None, :], BF16)
    row = lambda t: t.reshape(1, -1).astype(F32)
    mu = p['mu_shift']
    mu_rkv = row(mu[:3 * rwkv_w])
    mu_low = row(jnp.pad(mu[3 * rwkv_w:], (0, LOW_PAD - LOW_W)))
    return (mu_rkv, mu_low, row(p['w0']), row(p['a0']), row(p['k_k']), row(p['k_a']), row(p['r_k']),
            wd_h, wd_l, wi_h, wi_l, wg_h, wg_l, e128)


def _full_spec(a):
    nd = a.ndim
    return pl.BlockSpec(a.shape, lambda *_: (0,) * nd)


def _rwkv_prompt(rkv, low, par, ln_w, ln_b, n, s):
    m = n * s
    w = rkv.shape[1] // 3
    tm = min(256, s)
    chunk = WKV_CHUNK
    cpt = tm // chunk
    hd = np.arange(tm) // chunk
    ltri = jnp.asarray((hd[:, None] == hd[None, :]) & (np.arange(tm)[:, None] >= np.arange(tm)[None, :]), BF16)
    zeros_rkv = jnp.zeros((n, 1, 3 * w), F32)
    zeros_low = jnp.zeros((n, 1, LOW_PAD), F32)
    tiles_per_seq = s // tm
    tail = lambda width: pl.BlockSpec((8, width), lambda i: (jnp.maximum(i * (tm // 8) - 1, 0), 0))
    first = lambda width: pl.BlockSpec((None, 1, width), lambda i: (i // tiles_per_seq, 0, 0))
    big = pl.BlockSpec((tm, w), lambda i: (i, 0))
    outs = pl.pallas_call(
        functools.partial(_rwkv_prep_body, tm=tm, chunk=chunk, tiles_per_seq=tiles_per_seq),
        grid=(m // tm,),
        in_specs=[pl.BlockSpec((tm, 3 * w), lambda i: (i, 0)), pl.BlockSpec((tm, LOW_PAD), lambda i: (i, 0)),
                  tail(3 * w), tail(LOW_PAD), first(3 * w), first(LOW_PAD), _full_spec(ltri),
                  *[_full_spec(t) for t in par]],
        out_specs=[big] * 7 + [pl.BlockSpec((cpt, 1, w), lambda i: (i, 0, 0))],
        out_shape=[jax.ShapeDtypeStruct((m, w), BF16)] * 5 + [jax.ShapeDtypeStruct((m, w), F32)] * 2
        + [jax.ShapeDtypeStruct((m // chunk, 1, w), F32)],
        compiler_params=_cparams(("parallel",)),
        name="rwkv_prep",
    )(rkv, low, rkv, low, zeros_rkv, zeros_low, ltri, *par)
    at, bt, kt, rt, v, g, bon, ptot = outs

    tb = _pick(s, 512)
    nc = tb // chunk
    npairs = w // LANES
    nblk = s // tb
    tok = pl.BlockSpec((tb, w), lambda b, i: (b * nblk + i, 0))
    vec = pl.BlockSpec((1, w), lambda b, i: (0, 0))
    rw, st = pl.pallas_call(
        functools.partial(_wkv_chunk_body, nc=nc, chunk=chunk, npairs=npairs),
        grid=(n, nblk),
        in_specs=[tok, tok, tok, tok, tok,
                  pl.BlockSpec((nc, 1, w), lambda b, i: (b * nblk + i, 0, 0)),
                  tok, tok, vec, vec, _full_spec(par[-1])],
        out_specs=[tok, pl.BlockSpec((None, npairs, LANES, LANES), lambda b, i: (b, 0, 0, 0))],
        out_shape=[jax.ShapeDtypeStruct((m, w), BF16), jax.ShapeDtypeStruct((n, npairs, LANES, LANES), F32)],
        scratch_shapes=[pltpu.VMEM((npairs, LANES, LANES), F32)],
        compiler_params=_cparams(("parallel", "arbitrary")),
        name="wkv_chunks",
    )(at, bt, kt, rt, v, ptot, g, bon, ln_w.reshape(1, w), ln_b.reshape(1, w), par[-1])
    hp = LANES // RWKV_HEAD_DIM
    d = RWKV_HEAD_DIM
    st = jnp.stack([st[:, :, j * d:(j + 1) * d, j * d:(j + 1) * d] for j in range(hp)], axis=2)
    return rw, st.reshape(n, npairs * hp, d, d)


def _rwkv_sample(rkv, low, prev_rkv, prev_low, state0, par, ln_w, ln_b):
    db = rkv.shape[0]
    w = rkv.shape[1] // 3
    d = RWKV_HEAD_DIM
    nh = w // d
    ins = (rkv, low, prev_rkv, prev_low, *par)
    feats = pl.pallas_call(
        _rwkv_feat_body,
        in_specs=[_full_spec(t) for t in ins],
        out_specs=[pl.BlockSpec((db, w), lambda: (0, 0))] * 8,
        out_shape=[jax.ShapeDtypeStruct((db, w), F32)] * 8,
        compiler_params=pltpu.CompilerParams(vmem_limit_bytes=VMEM_LIMIT),
        name="rwkv_feats",
    )(*ins)
    hv = lambda t: t.reshape(db, nh, 1, d)
    row = pl.BlockSpec((None, nh, 1, d), lambda b: (b, 0, 0, 0))
    prow = pl.BlockSpec((nh, 1, d), lambda b: (0, 0, 0))
    mat = pl.BlockSpec((None, nh, d, d), lambda b: (b, 0, 0, 0))
    rw, s1 = pl.pallas_call(
        functools.partial(_wkv_step_body, nh=nh),
        grid=(db,),
        in_specs=[row] * 8 + [prow, prow, mat],
        out_specs=[row, mat],
        out_shape=[jax.ShapeDtypeStruct((db, nh, 1, d), BF16), jax.ShapeDtypeStruct((db, nh, d, d), F32)],
        compiler_params=_cparams(("parallel",)),
        name="wkv_step",
    )(*[hv(t) for t in feats], ln_w.reshape(nh, 1, d), ln_b.reshape(nh, 1, d), state0)
    return rw.reshape(db, w), s1


def _merge_body(attn_ref, rw_ref, ga_ref, gb_ref, wa_ref, wr_ref, o_ref):
    ya = _dg(attn_ref[...], wa_ref[...])
    yb = _dg(rw_ref[...], wr_ref[...])
    o_ref[...] = (ga_ref[...] * ya + gb_ref[...] * yb).astype(o_ref.dtype)


def _merge(attn, rw, gates, wa, wr, tm, tn):
    m, ka = attn.shape
    kr = rw.shape[1]
    d = wa.shape[1]
    nj = d // tn
    return pl.pallas_call(
        _merge_body,
        grid=(m // tm, nj),
        in_specs=[pl.BlockSpec((tm, ka), lambda i, j: (i, 0)), pl.BlockSpec((tm, kr), lambda i, j: (i, 0)),
                  pl.BlockSpec((tm, tn), lambda i, j: (i, j)), pl.BlockSpec((tm, tn), lambda i, j: (i, nj + j)),
                  pl.BlockSpec((ka, tn), lambda i, j: (0, j)), pl.BlockSpec((kr, tn), lambda i, j: (0, j))],
        out_specs=pl.BlockSpec((tm, tn), lambda i, j: (i, j)),
        out_shape=jax.ShapeDtypeStruct((m, d), BF16),
        compiler_params=_cparams(("parallel", "parallel")),
        name="merge",
    )(attn, rw, gates, gates, wa, wr)


def _resid_mm_body(a_ref, w_ref, x_ref, o_ref):
    o_ref[...] = x_ref[...] + _dg(a_ref[...], w_ref[...])


def _resid_matmul(a, w, x, tm, tn):
    m, k = a.shape
    n = w.shape[1]
    return pl.pallas_call(
        _resid_mm_body,
        grid=(m // tm, n // tn),
        in_specs=[pl.BlockSpec((tm, k), lambda i, j: (i, 0)), pl.BlockSpec((k, tn), lambda i, j: (0, j)),
                  pl.BlockSpec((tm, tn), lambda i, j: (i, j))],
        out_specs=pl.BlockSpec((tm, tn), lambda i, j: (i, j)),
        out_shape=jax.ShapeDtypeStruct((m, n), F32),
        compiler_params=_cparams(("parallel", "parallel")),
        name="out_proj",
    )(a, w, x)


def _ffn_up_body(x_ref, g_ref, wg_ref, wu_ref, o_ref, xn_ref):
    @pl.when(pl.program_id(1) == 0)
    def _():
        xn_ref[...] = _rms(x_ref[...], g_ref[...]).astype(BF16)

    xn = xn_ref[...]
    o_ref[...] = (jax.nn.silu(_dg(xn, wg_ref[...])) * _dg(xn, wu_ref[...])).astype(o_ref.dtype)


def _ffn_up(x, g, wg, wu, tm, tn):
    m, d = x.shape
    f = wg.shape[1]
    wspec = pl.BlockSpec((d, tn), lambda i, j: (0, j))
    return pl.pallas_call(
        _ffn_up_body,
        grid=(m // tm, f // tn),
        in_specs=[pl.BlockSpec((tm, d), lambda i, j: (i, 0)), pl.BlockSpec((1, d), lambda i, j: (0, 0)), wspec, wspec],
        out_specs=pl.BlockSpec((tm, tn), lambda i, j: (i, j)),
        out_shape=jax.ShapeDtypeStruct((m, f), BF16),
        scratch_shapes=[pltpu.VMEM((tm, d), BF16)],
        compiler_params=_cparams(("parallel", "arbitrary")),
        name="ffn_up",
    )(x, g, wg, wu)


def _ffn_down_body(h_ref, w_ref, x_ref, g_ref, o_ref, *, nj, tn):
    j = pl.program_id(1)
    val = x_ref[...] + _dg(h_ref[...], w_ref[...])
    for t in range(nj):
        @pl.when(j == t)
        def _(t=t):
            o_ref[:, t * tn:(t + 1) * tn] = val

    @pl.when(j == nj - 1)
    def _():
        o_ref[...] = _rms(o_ref[...], g_ref[...])


def _ffn_down(h, w, x, g, tm, tn):
    m, f = h.shape
    d = w.shape[1]
    nj = d // tn
    return pl.pallas_call(
        functools.partial(_ffn_down_body, nj=nj, tn=tn),
        grid=(m // tm, nj),
        in_specs=[pl.BlockSpec((tm, f), lambda i, j: (i, 0)), pl.BlockSpec((f, tn), lambda i, j: (0, j)),
                  pl.BlockSpec((tm, tn), lambda i, j: (i, j)), pl.BlockSpec((1, d), lambda i, j: (0, 0))],
        out_specs=pl.BlockSpec((tm, d), lambda i, j: (i, 0)),
        out_shape=jax.ShapeDtypeStruct((m, d), F32),
        compiler_params=_cparams(("parallel", "arbitrary")),
        name="ffn_down",
    )(h, w, x, g)


def _pick(total, pref):
    t = min(pref, total)
    while total % t:
        t //= 2
    return t


def _layer(x2, n, s, pos, wts, par, p, attend, rwkv):
    m, d = x2.shape
    att_w = wts['qkv'].shape[1] // 3
    rw_w = wts['rkv'].shape[1] // 3
    tm = _pick(s, 1024) if s % 8 == 0 else _pick(m, 1024)
    assert s % tm == 0 or tm % s == 0
    g_mix = p['g_mix'].reshape(1, d)

    tq = _pick(s, 512) if s % 8 == 0 else _pick(m, 512)
    assert s % tq == 0 or tq % s == 0
    tabs = _rope_tables(pos)
    if s < tq:
        tabs = tuple(jnp.tile(t, (tq // s, 1)) for t in tabs)
    qkv, xn, k3, v3 = _proj_qkv(x2, g_mix, wts['qkv'], tabs, tq, max(s // tq, 1))
    rkv = _matmul(xn, wts['rkv'], tm, rw_w, 'proj_rkv')
    low = _matmul(xn, wts['low'], tm, LOW_PAD, 'proj_low')
    tn_g = _pick(2 * d, 1024)
    gates = _matmul(xn, wts['gate'], tm, tn_g, 'proj_gates', _epi_sigmoid_bias, (p['b_gate'].reshape(1, 2 * d),),
                    (pl.BlockSpec((1, tn_g), lambda i, j: (0, j)),), out_dtype=BF16)

    attn = attend(qkv)
    rw, wkv_new = rwkv(rkv, low)

    merged = _merge(attn, rw, gates, wts['proj_attn'], wts['proj_rwkv'], tm, _pick(d, 1024))
    hid = _resid_matmul(merged, wts['out'], x2, tm, _pick(d, 1024))
    f = wts['ffn_gate'].shape[1]
    h = _ffn_up(hid, p['g_ffn'].reshape(1, d), wts['ffn_gate'], wts['ffn_up'], tm, _pick(f, 512))
    y = _ffn_down(h, wts['ffn_down'], hid, p['g_final'].reshape(1, d), _pick(m, 512), _pick(d, 1024))

    sh_last = jnp.concatenate([rkv.reshape(n, s, -1)[:, -1], low.reshape(n, s, -1)[:, -1, :LOW_W]], axis=1)
    return y, k3, v3, wkv_new, sh_last


def kernel(x_prompt, x_sample, cache_k, cache_v, state_wkv, state_shift, page_table, g_mix, w_in, b_gate, mu_shift, w0, w_decay_up, a0, w_iclr_up, w_gate_up, k_k, k_a, r_k, ln_x_w, ln_x_b, w_proj_attn, w_proj_rwkv, w_out, g_ffn, w_ffn_gate, w_ffn_up, w_ffn_down, g_final):
    depth = w_in.shape[0]
    assert depth == 1, "single-layer trunk"
    n, s, d = x_prompt.shape
    db, ds, _ = x_sample.shape
    assert ds == 1, "one new token per decode sequence"
    _, n_pool, page, nh, dh = cache_k.shape
    assert page == PAGE_SIZE and dh == ATT_HEAD_DIM
    att_w = nh * dh
    rw_w = w0.shape[1]
    n_pages = page_table.shape[1]
    past = n_pages * PAGE_SIZE
    assert s % MOBA_BLOCK == 0 and past % MOBA_BLOCK == 0
    l = 0

    wi = w_in[l]
    o = 3 * att_w
    wts = {
        'qkv': wi[:, :o].astype(BF16),
        'rkv': wi[:, o:o + 3 * rw_w].astype(BF16),
        'low': jnp.pad(wi[:, o + 3 * rw_w:o + 3 * rw_w + LOW_W], ((0, 0), (0, LOW_PAD - LOW_W))).astype(BF16),
        'gate': wi[:, o + 3 * rw_w + LOW_W:].astype(BF16),
        'proj_attn': w_proj_attn[l].astype(BF16), 'proj_rwkv': w_proj_rwkv[l].astype(BF16),
        'out': w_out[l].astype(BF16), 'ffn_gate': w_ffn_gate[l].astype(BF16),
        'ffn_up': w_ffn_up[l].astype(BF16), 'ffn_down': w_ffn_down[l].astype(BF16),
    }
    p = {'g_mix': g_mix[l], 'b_gate': b_gate[l], 'g_ffn': g_ffn[l], 'g_final': g_final,
         'mu_shift': mu_shift[l], 'w0': w0[l], 'w_decay_up': w_decay_up[l], 'a0': a0[l],
         'w_iclr_up': w_iclr_up[l], 'w_gate_up': w_gate_up[l], 'k_k': k_k[l], 'k_a': k_a[l], 'r_k': r_k[l]}
    par = _rwkv_params(p, rw_w)
    ln_w, ln_b = ln_x_w[l], ln_x_b[l]

    blk_sums = []

    def attend_prompt(qkv):
        attn, bs = _moba_prompt(qkv, n, s, page_table, cache_k, l)
        blk_sums.append(bs)
        return attn

    yp, kp, vp, wp, sp = _layer(
        x_prompt.reshape(n * s, d), n, s, jnp.arange(s), wts, par, p, attend_prompt,
        lambda rkv, low: _rwkv_prompt(rkv, low, par, ln_w, ln_b, n, s))

    q_blk = past // MOBA_BLOCK

    def attend_sample(qkv):
        q, k_new, v_new = qkv[:, :att_w], qkv[:, att_w:2 * att_w], qkv[:, 2 * att_w:]
        blk_sum = blk_sums[0].reshape(db, q_blk, att_w)
        sel = _decode_select(blk_sum, q, q_blk)
        return _decode_attn(page_table, sel, q, k_new, v_new, cache_k, cache_v, l)

    sh_prev = state_shift[l]
    prev_rkv = sh_prev[:, :3 * rw_w]
    prev_low = jnp.pad(sh_prev[:, 3 * rw_w:], ((0, 0), (0, LOW_PAD - LOW_W)))
    ys, ks_, vs_, ws_, ss_ = _layer(
        x_sample.reshape(db * ds, d), db, ds, past + jnp.arange(ds), wts, par, p,
        attend_sample,
        lambda rkv, low: _rwkv_sample(rkv, low, prev_rkv, prev_low, state_wkv[l], par, ln_w, ln_b))

    return (yp.reshape(n, s, d), ys.reshape(db, ds, d),
            kp.reshape(1, n, s, nh, dh), vp.reshape(1, n, s, nh, dh),
            wp.astype(state_wkv.dtype)[None], sp.astype(state_shift.dtype)[None],
            ks_.reshape(1, db, ds, nh, dh), vs_.reshape(1, db, ds, nh, dh),
            ws_.astype(state_wkv.dtype)[None], ss_.astype(state_shift.dtype)[None])
```

```python
import functools
import math
from typing import NamedTuple

import numpy as np
import jax
import jax.numpy as jnp
from jax import lax
from jax.experimental import pallas as pl
from jax.experimental.pallas import tpu as pltpu

F32 = jnp.float32
BF16 = jnp.bfloat16

ATT_HEAD_DIM = 128
ROT_DIM = ATT_HEAD_DIM // 4
ROPE_THETA = 500000.0
MOBA_BLOCK = 256
MOBA_TOP_K = 3
PAGE_SIZE = 128
RWKV_HEAD_DIM = 64
DECAY_RANK = 64
ICLR_RANK = 64
GATE_RANK = 160
RMS_EPS = 1e-6
GN_EPS = 64e-5
L2_EPS = 1e-12
LOG2E = math.log2(math.e)

LANES = 128
LOW_W = DECAY_RANK + ICLR_RANK + GATE_RANK
LOW_PAD = -(-LOW_W // LANES) * LANES
WKV_CHUNK = 64
VMEM_LIMIT = 56 * 1024 * 1024

NN = (((1,), (0,)), ((), ()))
NT = (((1,), (1,)), ((), ()))
TN = (((0,), (0,)), ((), ()))


def _cparams(sem):
    return pltpu.CompilerParams(dimension_semantics=sem, vmem_limit_bytes=VMEM_LIMIT)


def _dg(a, b, dims=NN):
    return lax.dot_general(a, b, dims, preferred_element_type=F32)


def _split2(x):
    hi = x.astype(BF16)
    lo = (x - hi.astype(F32)).astype(BF16)
    return hi, lo


def _dot3(a, b, dims=NN):
    ah, al = _split2(a)
    bh, bl = _split2(b)
    return _dg(ah, bh, dims) + (_dg(ah, bl, dims) + _dg(al, bh, dims))


def _dot3_pre(a, bh, bl, dims=NN):
    ah, al = _split2(a)
    return _dg(ah, bh, dims) + (_dg(ah, bl, dims) + _dg(al, bh, dims))


def _dot_exact_rhs(a, b_bf16, dims=NN):
    hi, lo = _split2(a)
    return _dg(hi, b_bf16, dims) + _dg(lo, b_bf16, dims)


def _dot_exact_lhs(a_bf16, b, dims=NN):
    hi, lo = _split2(b)
    return _dg(a_bf16, hi, dims) + _dg(a_bf16, lo, dims)


def _rms(x, g):
    return x * lax.rsqrt(jnp.mean(x * x, axis=-1, keepdims=True) + RMS_EPS) * g


def _proj_qkv_body(x_ref, g_ref, w_ref, cos_ref, sa_ref, sb_ref, o_ref, xn_ref, k3_ref, v3_ref):
    j = pl.program_id(1)

    @pl.when(j == 0)
    def _():
        xn_ref[...] = _rms(x_ref[...], g_ref[...]).astype(BF16)

    val = _epi_rope(_dg(xn_ref[...], w_ref[...]), j, cos_ref, sa_ref, sb_ref)
    o_ref[...] = val

    @pl.when(j == 1)
    def _():
        k3_ref[...] = val.reshape(k3_ref.shape)

    @pl.when(j == 2)
    def _():
        v3_ref[...] = val.reshape(v3_ref.shape)


def _mm_body(*refs, epilogue, n_aux):
    a_ref, w_ref = refs[:2]
    aux = refs[2:2 + n_aux]
    o_ref, = refs[2 + n_aux:]
    o_ref[...] = epilogue(_dg(a_ref[...], w_ref[...]), pl.program_id(1), *aux).astype(o_ref.dtype)


def _epi_none(acc, j):
    return acc


def _epi_rope(acc, j, cos_ref, sa_ref, sb_ref):
    c, sa, sb = cos_ref[...], sa_ref[...], sb_ref[...]
    half = ROT_DIM // 2
    outs = []
    for h in range(acc.shape[1] // ATT_HEAD_DIM):
        xh = acc[:, h * ATT_HEAD_DIM:(h + 1) * ATT_HEAD_DIM]
        outs.append(xh * c + pltpu.roll(xh, ATT_HEAD_DIM - half, 1) * sa + pltpu.roll(xh, half, 1) * sb)
    rot = jnp.concatenate(outs, axis=1)
    rot = rot * jnp.where(j == 0, ATT_HEAD_DIM ** -0.5, 1.0).astype(F32)
    return jnp.where(j < 2, rot, acc)


def _epi_sigmoid_bias(acc, j, b_ref):
    return jax.nn.sigmoid(acc + b_ref[...])


def _proj_qkv(x, g, w, tabs, tm, period):
    m, d = x.shape
    att_w = w.shape[1] // 3
    nh = att_w // ATT_HEAD_DIM
    tab = pl.BlockSpec((tm, ATT_HEAD_DIM), lambda i, j: (i % period, 0))
    heads = pl.BlockSpec((tm, nh, ATT_HEAD_DIM), lambda i, j: (i, 0, 0))
    return pl.pallas_call(
        _proj_qkv_body,
        grid=(m // tm, 3),
        in_specs=[pl.BlockSpec((tm, d), lambda i, j: (i, 0)),
                  pl.BlockSpec((1, d), lambda i, j: (0, 0)),
                  pl.BlockSpec((d, att_w), lambda i, j: (0, j)), tab, tab, tab],
        out_specs=[pl.BlockSpec((tm, att_w), lambda i, j: (i, j)), pl.BlockSpec((tm, d), lambda i, j: (i, 0)),
                   heads, heads],
        out_shape=[jax.ShapeDtypeStruct((m, 3 * att_w), F32), jax.ShapeDtypeStruct((m, d), BF16),
                   jax.ShapeDtypeStruct((m, nh, ATT_HEAD_DIM), F32), jax.ShapeDtypeStruct((m, nh, ATT_HEAD_DIM), F32)],
        compiler_params=_cparams(("parallel", "arbitrary")),
        name="proj_qkv",
    )(x, g, w, *tabs)


def _matmul(a, w, tm, tn, name, epilogue=_epi_none, aux=(), aux_specs=(), out_dtype=F32):
    m, k = a.shape
    n = w.shape[1]
    return pl.pallas_call(
        functools.partial(_mm_body, epilogue=epilogue, n_aux=len(aux)),
        grid=(m // tm, n // tn),
        in_specs=[pl.BlockSpec((tm, k), lambda i, j: (i, 0)), pl.BlockSpec((k, tn), lambda i, j: (0, j)), *aux_specs],
        out_specs=pl.BlockSpec((tm, tn), lambda i, j: (i, j)),
        out_shape=jax.ShapeDtypeStruct((m, n), out_dtype),
        compiler_params=_cparams(("parallel", "parallel")),
        name=name,
    )(a, w, *aux)


def _rope_tables(pos):
    half = ROT_DIM // 2
    inv = np.exp(np.arange(half, dtype=np.float64) * (-2.0 * math.log(ROPE_THETA) / ROT_DIM))
    ang = pos.astype(np.float64)[:, None] * inv[None, :]
    cos, sin = jnp.asarray(np.cos(ang), F32), jnp.asarray(np.sin(ang), F32)
    s = pos.shape[0]
    z = lambda w: jnp.zeros((s, w), F32)
    cos_t = jnp.concatenate([cos, cos, jnp.ones((s, ATT_HEAD_DIM - ROT_DIM), F32)], axis=1)
    sin_a = jnp.concatenate([-sin, z(ATT_HEAD_DIM - half)], axis=1)
    sin_b = jnp.concatenate([z(half), sin, z(ATT_HEAD_DIM - ROT_DIM)], axis=1)
    return cos_t, sin_a, sin_b


def _moba_prompt_body(pt_ref, q_ref, k_ref, v_ref, ck_ref, o_ref, bs_ref, km_ref, kb_ref, vt_ref, s_ref,
                      pbuf, psum, psem, osem, *, nb, nh, qpb, pg):
    qp = pl.program_id(2)
    blk = MOBA_BLOCK
    sub = 8
    g = (pl.program_id(0) * nh + pl.program_id(1)) * (nb // qpb) + qp
    slot = g % 2

    def page_copies(step, slot_):
        b = step // pg.per_seq
        p0 = (step % pg.per_seq) * pg.pps
        return [pltpu.make_async_copy(ck_ref.at[pg.layer, pt_ref[b, p0 + j]], pbuf.at[slot_, j], psem.at[slot_])
                for j in range(pg.pps)]

    @pl.when(g == 0)
    def _():
        for c in page_copies(0, 0):
            c.start()

    @pl.when(g + 1 < pg.groups)
    def _():
        for c in page_copies(g + 1, 1 - slot):
            c.start()

    @pl.when(qp == 0)
    def _():
        km_ref[...] = jnp.zeros_like(km_ref)
        for j in range(nb):
            kj = k_ref[j * blk:(j + 1) * blk, :]
            km_ref[j:j + 1, :] = jnp.mean(kj, axis=0, keepdims=True)
            kb_ref[j] = kj.astype(BF16)
            vt_ref[j] = v_ref[j * blk:(j + 1) * blk, :].T.astype(BF16)

    q = q_ref[...]
    gate = _dot3(km_ref[...], q, NT)
    rowi = lax.broadcasted_iota(jnp.int32, gate.shape, 0)
    qblk = qp * qpb + lax.broadcasted_iota(jnp.int32, gate.shape, 1) // blk
    valid = rowi < qblk
    gm = jnp.where(valid, gate, -jnp.inf)
    cnt = jnp.zeros(gate.shape, F32)
    for i in range(nb):
        gi = gm[i:i + 1, :]
        cnt = cnt + jnp.where(gi > gm, 1.0, 0.0) + jnp.where(gi == gm, (rowi > i).astype(F32), 0.0)
    bias = jnp.where(valid, jnp.where(cnt < MOBA_TOP_K, 0.0, -jnp.inf), -jnp.inf)

    qh = (q * LOG2E).astype(BF16)
    keyi = lax.broadcasted_iota(jnp.int32, (blk, blk), 0)
    qi = lax.broadcasted_iota(jnp.int32, (blk, blk), 1)
    fold = lambda t, op: op(t.reshape(blk // sub, sub, blk), axis=0)

    def sweep(qbv, u):
        cols = slice(u * blk, (u + 1) * blk)
        qu = qh[cols, :]
        pm = None
        for j in range(qbv + 1):
            s = _dg(kb_ref[j], qu, NT)
            s = jnp.where(keyi <= qi, s, -jnp.inf) if j == qbv else s + bias[j:j + 1, cols]
            s_ref[u, j] = s
            t = fold(s, jnp.max)
            pm = t if pm is None else jnp.maximum(pm, t)
        m = jnp.max(pm, axis=0, keepdims=True)
        lp = jnp.zeros((sub, blk), F32)
        acc = jnp.zeros((ATT_HEAD_DIM, blk), F32)
        for j in range(qbv + 1):
            p = jnp.exp2(s_ref[u, j] - m)
            lp = lp + fold(p, jnp.sum)
            acc = acc + _dg(vt_ref[j], p.astype(BF16))
        l = jnp.sum(lp, axis=0, keepdims=True)
        o_ref[cols, :] = (acc / l).T.astype(o_ref.dtype)

    def group(gv):
        for u in range(qpb):
            sweep(gv * qpb + u, u)

    for gv in range(nb // qpb):
        pl.when(qp == gv)(functools.partial(group, gv))

    nbs = pg.pps // pg.ppb

    def sums_out(step, slot_):
        dst = bs_ref.at[step // pg.per_seq, pl.ds((step % pg.per_seq) * nbs, nbs)]
        return pltpu.make_async_copy(psum.at[slot_], dst, osem.at[slot_])

    @pl.when(g < pg.groups)
    def _():
        for c in page_copies(g, slot):
            c.wait()

        @pl.when(g >= 2)
        def _():
            sums_out(g - 2, slot).wait()

        for j in range(nbs):
            acc = jnp.sum(pbuf[slot, j * pg.ppb], axis=0)
            for t in range(1, pg.ppb):
                acc = acc + jnp.sum(pbuf[slot, j * pg.ppb + t], axis=0)
            psum[slot, j] = acc
        sums_out(g, slot).start()

        @pl.when(g == pg.groups - 1)
        def _():
            sums_out(g, slot).wait()

            @pl.when(g >= 1)
            def _():
                sums_out(g - 1, 1 - slot).wait()


class _PageGroups(NamedTuple):
    pps: int
    ppb: int
    per_seq: int
    groups: int
    layer: int


def _moba_prompt(qkv, n, s, page_table, cache_k, layer):
    att_w = qkv.shape[1] // 3
    nh = att_w // ATT_HEAD_DIM
    nb = s // MOBA_BLOCK
    nbp = -(-nb // 8) * 8
    db, n_pages = page_table.shape
    _, _, page, ch, dh = cache_k.shape
    ppb = MOBA_BLOCK // PAGE_SIZE
    qpb = 2 if nb % 2 == 0 else 1
    nq = nb // qpb
    nsteps = n * nh * nq
    pps = ppb
    while (db * n_pages // pps > nsteps or pps * 2 <= 8) and n_pages % (pps * 2) == 0:
        pps *= 2
    pg = _PageGroups(pps, ppb, n_pages // pps, db * n_pages // pps, layer)
    assert pg.groups <= nsteps, "more page groups than attention grid steps"
    gs = pltpu.PrefetchScalarGridSpec(
        num_scalar_prefetch=1, grid=(n, nh, nq),
        in_specs=[pl.BlockSpec((qpb * MOBA_BLOCK, ATT_HEAD_DIM), lambda b, h, i, pt: (b * nq + i, h)),
                  pl.BlockSpec((s, ATT_HEAD_DIM), lambda b, h, i, pt: (b, nh + h)),
                  pl.BlockSpec((s, ATT_HEAD_DIM), lambda b, h, i, pt: (b, 2 * nh + h)),
                  pl.BlockSpec(memory_space=pl.ANY)],
        out_specs=[pl.BlockSpec((qpb * MOBA_BLOCK, ATT_HEAD_DIM), lambda b, h, i, pt: (b * nq + i, h)),
                   pl.BlockSpec(memory_space=pl.ANY)],
        scratch_shapes=[pltpu.VMEM((nbp, ATT_HEAD_DIM), F32),
                        pltpu.VMEM((nb, MOBA_BLOCK, ATT_HEAD_DIM), BF16),
                        pltpu.VMEM((nb, ATT_HEAD_DIM, MOBA_BLOCK), BF16),
                        pltpu.VMEM((qpb, nb, MOBA_BLOCK, MOBA_BLOCK), F32),
                        pltpu.VMEM((2, pps, page, ch, dh), F32),
                        pltpu.VMEM((2, pps // ppb, ch, dh), F32),
                        pltpu.SemaphoreType.DMA((2,)), pltpu.SemaphoreType.DMA((2,))])
    return pl.pallas_call(
        functools.partial(_moba_prompt_body, nb=nb, nh=nh, qpb=qpb, pg=pg),
        grid_spec=gs,
        out_shape=[jax.ShapeDtypeStruct((n * s, att_w), BF16),
                   jax.ShapeDtypeStruct((db, n_pages // ppb, ch, dh), F32)],
        compiler_params=_cparams(("arbitrary", "arbitrary", "arbitrary")),
        name="moba_prompt",
    )(page_table, qkv, qkv, qkv, cache_k)


def _ring_step(copies, g, nsteps):
    slot = g % 2

    @pl.when(g == 0)
    def _():
        for c in copies(0, 0):
            c.start()

    @pl.when(g + 1 < nsteps)
    def _():
        for c in copies(g + 1, 1 - slot):
            c.start()

    for c in copies(g, slot):
        c.wait()
    return slot


def _decode_select_body(bs_ref, q_ref, sel_ref, *, nbp, q_blk, nh):
    bs = bs_ref[...]
    q = q_ref[...]
    it = lax.broadcasted_iota(jnp.int32, (nbp, 1), 0)
    for h in range(nh):
        sl = slice(h * ATT_HEAD_DIM, (h + 1) * ATT_HEAD_DIM)
        g = jnp.sum(q[:, sl] * (bs[:, sl] / MOBA_BLOCK), axis=1, keepdims=True)
        g = jnp.where(it < q_blk, g, -jnp.inf)
        for t in range(MOBA_TOP_K):
            mx = jnp.max(g, axis=0, keepdims=True)
            idx = jnp.min(jnp.where(g == mx, it, nbp), axis=0, keepdims=True)
            sel_ref[t:t + 1, h:h + 1] = jnp.where(mx > -jnp.inf, idx, -1)
            g = jnp.where(it == idx, -jnp.inf, g)


def _decode_select(blk_sum, q, q_blk):
    db, nbp, att_w = blk_sum.shape
    nh = att_w // ATT_HEAD_DIM
    return pl.pallas_call(
        functools.partial(_decode_select_body, nbp=nbp, q_blk=q_blk, nh=nh),
        grid=(db,),
        in_specs=[pl.BlockSpec((None, nbp, att_w), lambda b: (b, 0, 0)),
                  pl.BlockSpec((None, 1, att_w), lambda b: (b, 0, 0))],
        out_specs=pl.BlockSpec((None, MOBA_TOP_K, nh), lambda b: (b, 0, 0)),
        out_shape=jax.ShapeDtypeStruct((db, MOBA_TOP_K, nh), jnp.int32),
        compiler_params=_cparams(("parallel",)),
        name="decode_select",
    )(blk_sum, q.reshape(db, 1, att_w))


def _decode_attn_body(pt_ref, sel_ref, q_ref, kn_ref, vn_ref, ck_ref, cv_ref, o_ref, kbuf, vbuf, sem,
                      *, ppb, nh, hps, nsteps, layer):
    g = pl.program_id(0)
    npg = MOBA_TOP_K * ppb
    hgroups = nh // hps

    def copies(step, slot):
        b, h0 = step // hgroups, (step % hgroups) * hps
        cps = []
        for u in range(hps):
            for t in range(npg):
                blk = jnp.maximum(sel_ref[(b * MOBA_TOP_K + t // ppb) * nh + h0 + u], 0)
                page = pt_ref[b, blk * ppb + t % ppb]
                cps.append(pltpu.make_async_copy(ck_ref.at[layer, page, :, h0 + u, :], kbuf.at[slot, u, t], sem.at[0, slot]))
                cps.append(pltpu.make_async_copy(cv_ref.at[layer, page, :, h0 + u, :], vbuf.at[slot, u, t], sem.at[1, slot]))
        return cps

    slot = _ring_step(copies, g, nsteps)
    b, h0 = g // hgroups, (g % hgroups) * hps
    for u in range(hps):
        lanes = slice(u * ATT_HEAD_DIM, (u + 1) * ATT_HEAD_DIM)
        q = q_ref[:, lanes]
        s_own = jnp.sum(q * kn_ref[:, lanes], axis=1, keepdims=True)
        m = s_own
        ss = []
        for t in range(npg):
            ok = sel_ref[(b * MOBA_TOP_K + t // ppb) * nh + h0 + u] >= 0
            st = jnp.where(ok, jnp.sum(kbuf[slot, u, t] * q, axis=1, keepdims=True), -jnp.inf)
            ss.append(st)
            m = jnp.maximum(m, jnp.max(st, axis=0, keepdims=True))
        p_own = jnp.exp(s_own - m)
        l = p_own
        acc = p_own * vn_ref[:, lanes]
        for t in range(npg):
            p = jnp.exp(ss[t] - m)
            l = l + jnp.sum(p, axis=0, keepdims=True)
            acc = acc + jnp.sum(p * vbuf[slot, u, t], axis=0, keepdims=True)
        o_ref[:, lanes] = (acc / l).astype(o_ref.dtype)


def _decode_attn(page_table, sel, q, k_new, v_new, cache_k, cache_v, layer):
    db, att_w = q.shape
    nh = att_w // ATT_HEAD_DIM
    ppb = MOBA_BLOCK // PAGE_SIZE
    npg = MOBA_TOP_K * ppb
    hps = next(c for c in (4, 2, 1) if nh % c == 0)
    hgroups = nh // hps
    nsteps = db * hgroups
    row = pl.BlockSpec((None, 1, hps * ATT_HEAD_DIM), lambda g, pt, sl: (g // hgroups, 0, g % hgroups))
    hbm = pl.BlockSpec(memory_space=pl.ANY)
    gs = pltpu.PrefetchScalarGridSpec(
        num_scalar_prefetch=2, grid=(nsteps,),
        in_specs=[row, row, row, hbm, hbm],
        out_specs=row,
        scratch_shapes=[pltpu.VMEM((2, hps, npg, PAGE_SIZE, ATT_HEAD_DIM), F32),
                        pltpu.VMEM((2, hps, npg, PAGE_SIZE, ATT_HEAD_DIM), F32),
                        pltpu.SemaphoreType.DMA((2, 2))])
    r3 = lambda a: a.reshape(db, 1, att_w)
    out = pl.pallas_call(
        functools.partial(_decode_attn_body, ppb=ppb, nh=nh, hps=hps, nsteps=nsteps, layer=layer), grid_spec=gs,
        out_shape=jax.ShapeDtypeStruct((db, 1, att_w), BF16),
        compiler_params=_cparams(("arbitrary",)),
        name="decode_attn",
    )(page_table, sel.reshape(-1), r3(q), r3(k_new), r3(v_new), cache_k, cache_v)
    return out.reshape(db, att_w)


def _seg_sum(x, e_bf16):
    outs = [_dot_exact_rhs(x[:, c:c + LANES], e_bf16) for c in range(0, x.shape[1], LANES)]
    return outs[0] if len(outs) == 1 else jnp.concatenate(outs, axis=1)


def _rwkv_features(rkv, low, prev_rkv, prev_low, par, single_pass):
    (mu_rkv, mu_low, w0, a0, k_k, k_a, r_k, wd_h, wd_l, wi_h, wi_l, wg_h, wg_l, e128) = par
    w = w0.shape[1]
    mix = rkv + mu_rkv * (prev_rkv - rkv)
    mlow = low + mu_low * (prev_low - low)
    r, k, v = mix[:, :w], mix[:, w:2 * w], mix[:, 2 * w:]
    lane = lax.broadcasted_iota(jnp.int32, mlow.shape, 1)
    feat = jnp.where(lane < DECAY_RANK, jnp.tanh(mlow),
                     jnp.where(lane < DECAY_RANK + ICLR_RANK, mlow,
                               jnp.where(lane < LOW_W, jax.nn.sigmoid(mlow), 0.0)))
    z = w0 + _dot3_pre(feat, wd_h, wd_l)
    nz = -z
    w_log = -(jnp.maximum(nz, 0.0) + jnp.log1p(jnp.exp(-jnp.abs(nz)))) - 0.5
    logw = -jnp.exp(w_log)
    if single_pass:
        fb = feat.astype(BF16)
        up_i, up_g = _dg(fb, wi_h), _dg(fb, wg_h)
    else:
        up_i, up_g = _dot3_pre(feat, wi_h, wi_l), _dot3_pre(feat, wg_h, wg_l)
    a = jax.nn.sigmoid(a0 + up_i)
    g = up_g
    kk = k * k_k
    kk = kk / jnp.maximum(jnp.sqrt(_seg_sum(kk * kk, e128)), L2_EPS)
    k2 = k * (1.0 + (a - 1.0) * k_a)
    bonus = _seg_sum(r * k2 * r_k, e128) * v
    return r, k2, v, kk, a, logw, g, bonus


_N_PAR = 14


def _rwkv_prep_body(*refs, tm, chunk, tiles_per_seq):
    rkv_ref, low_ref, prkv_ref, plow_ref, frkv_ref, flow_ref, ltri_ref = refs[:7]
    par = tuple(r[...] for r in refs[7:7 + _N_PAR])
    at_ref, bt_ref, kt_ref, rt_ref, v_ref, g_ref, bon_ref, pt_ref = refs[7 + _N_PAR:]
    i = pl.program_id(0)
    rkv, low = rkv_ref[...], low_ref[...]
    seq_start = i % tiles_per_seq == 0

    def prev_of(x, tail_ref, first_ref):
        first = jnp.where(seq_start, first_ref[...], tail_ref[7:8, :])
        rowi = lax.broadcasted_iota(jnp.int32, x.shape, 0)
        return jnp.where(rowi == 0, first, pltpu.roll(x, 1, 0))

    prev_rkv = prev_of(rkv, prkv_ref, frkv_ref)
    prev_low = prev_of(low, plow_ref, flow_ref)
    r, k2, v, kk, a, logw, g, bonus = _rwkv_features(rkv, low, prev_rkv, prev_low, par, True)
    cum = _dot_exact_lhs(ltri_ref[...], logw)
    e_in = jnp.exp(cum)
    e_out = jnp.exp(-cum)
    at_ref[...] = (-kk * jnp.exp(cum - logw)).astype(at_ref.dtype)
    bt_ref[...] = (kk * a * e_out).astype(bt_ref.dtype)
    kt_ref[...] = (k2 * e_out).astype(kt_ref.dtype)
    rt_ref[...] = (r * e_in).astype(rt_ref.dtype)
    v_ref[...] = v.astype(v_ref.dtype)
    g_ref[...] = g
    bon_ref[...] = bonus
    for c in range(tm // chunk):
        pt_ref[c] = e_in[(c + 1) * chunk - 1:(c + 1) * chunk, :]


def _wkv_chunk_body(at_ref, bt_ref, kt_ref, rt_ref, v_ref, pt_ref, g_ref, bon_ref, lnw_ref, lnb_ref, e_ref,
                    rw_ref, st_ref, s_scr, *, nc, chunk, npairs):
    T = chunk
    hp = LANES // RWKV_HEAD_DIM
    W = hp * T

    @pl.when(pl.program_id(1) == 0)
    def _():
        s_scr[...] = jnp.zeros_like(s_scr)

    lane_head = lax.broadcasted_iota(jnp.int32, (1, 1, LANES), 2) // RWKV_HEAD_DIM
    ri = lax.broadcasted_iota(jnp.int32, (1, W, W), 1)
    ci = lax.broadcasted_iota(jnp.int32, (1, W, W), 2)
    eye = jnp.where(ri == ci, 1.0, 0.0).astype(F32)
    zero = jnp.zeros((), BF16)
    e1 = functools.partial(jnp.einsum, preferred_element_type=F32)
    gram = functools.partial(e1, 'bid,bjd->bij')
    mm = functools.partial(e1, 'bij,bjd->bid')
    mm_nt = functools.partial(e1, 'bwk,bvk->bwv')
    bf = lambda t: t.astype(BF16)
    e128 = e_ref[...]
    lnw, lnb = lnw_ref[...], lnb_ref[...]

    def by_pair(x):
        return jnp.stack([x[:, p * LANES:(p + 1) * LANES] for p in range(npairs)], axis=0)

    def stacked(ref, rows):
        x = by_pair(ref[rows, :])
        return jnp.concatenate([jnp.where(lane_head == h, x, zero) for h in range(hp)], axis=1)

    def chunk_step(c, carry):
        rows = pl.ds(pl.multiple_of(c * T, T), T)
        a2, b2, k2, r2, v2 = (stacked(r, rows) for r in (at_ref, bt_ref, kt_ref, rt_ref, v_ref))
        bk = jnp.concatenate([b2, k2], axis=1)
        ga = gram(a2, bk)
        gr = gram(r2, bk)
        gab = jnp.where(ci < ri, ga[:, :, :W], 0.0)
        gak = bf(jnp.where(ci < ri, ga[:, :, W:], 0.0))
        grbk = bf(jnp.concatenate([jnp.where(ci <= ri, gr[:, :, :W], 0.0),
                                   jnp.where(ci <= ri, gr[:, :, W:], 0.0)], axis=2))
        x = eye + gab
        lp = gab
        span = 2
        while span < T:
            lp = mm(bf(lp), bf(lp))
            x = mm(bf(x), bf(eye + lp))
            span *= 2
        aw = jnp.concatenate([a2, bf(mm(gak, v2))], axis=2)
        xa = mm(bf(x), aw)
        ahat = bf(xa[:, :, :LANES])
        uhat = xa[:, :, LANES:]

        s = s_scr[...]
        sb = bf(s)
        u = mm_nt(ahat, sb) + uhat
        uv = jnp.concatenate([bf(u), v2], axis=1)
        y2 = mm_nt(r2, sb) + mm(grbk, uv)
        s_scr[...] = (s + e1('bwv,bwk->bvk', uv, bk)) * by_pair(pt_ref[c])

        y3 = y2[:, :T]
        for h in range(1, hp):
            y3 = y3 + y2[:, h * T:(h + 1) * T]
        y = jnp.concatenate([y3[p] for p in range(npairs)], axis=1)
        mu = _seg_sum(y, e128) / RWKV_HEAD_DIM
        yc = y - mu
        var = _seg_sum(yc * yc, e128) / RWKV_HEAD_DIM
        yn = yc * lax.rsqrt(var + GN_EPS) * lnw + lnb
        rw_ref[rows, :] = ((yn + bon_ref[rows, :]) * g_ref[rows, :]).astype(rw_ref.dtype)
        return carry

    lax.fori_loop(0, nc, chunk_step, 0)
    st_ref[...] = s_scr[...]


def _wkv_step_body(r_ref, k_ref, v_ref, kk_ref, a_ref, lw_ref, g_ref, bon_ref, lnw_ref, lnb_ref, s0_ref,
                   rw_ref, s1_ref, *, nh):
    d = s0_ref.shape[-1]
    eye = lax.broadcasted_iota(jnp.int32, (d, d), 0) == lax.broadcasted_iota(jnp.int32, (d, d), 1)
    for h in range(nh):
        s0 = s0_ref[h]
        r, k, v, kk, a = r_ref[h], k_ref[h], v_ref[h], kk_ref[h], a_ref[h]
        v_col = jnp.sum(jnp.where(eye, v, 0.0), axis=1, keepdims=True)
        sa = jnp.sum(s0 * (-kk), axis=1, keepdims=True)
        s1 = s0 * jnp.exp(lw_ref[h]) + sa * (kk * a) + v_col * k
        s1_ref[h] = s1
        y_col = jnp.sum(s1 * r, axis=1, keepdims=True)
        y = jnp.sum(jnp.where(eye, y_col, 0.0), axis=0, keepdims=True)
        mu = jnp.mean(y, axis=1, keepdims=True)
        var = jnp.mean(jnp.square(y - mu), axis=1, keepdims=True)
        yn = (y - mu) * lax.rsqrt(var + GN_EPS) * lnw_ref[h] + lnb_ref[h]
        rw_ref[h] = ((yn + bon_ref[h]) * g_ref[h]).astype(rw_ref.dtype)


def _rwkv_feat_body(*refs):
    rkv_ref, low_ref, prkv_ref, plow_ref = refs[:4]
    par = tuple(r[...] for r in refs[4:4 + _N_PAR])
    outs = refs[4 + _N_PAR:]
    vals = _rwkv_features(rkv_ref[...], low_ref[...], prkv_ref[...], plow_ref[...], par, False)
    for o, x in zip(outs, vals):
        o[...] = x


def _rwkv_params(p, rwkv_w):
    def pad_up(wt, r0):
        full = jnp.zeros((LOW_PAD, rwkv_w), F32).at[r0:r0 + wt.shape[0]].set(wt)
        hi = full.astype(BF16)
        return hi, (full - hi.astype(F32)).astype(BF16)

    wd_h, wd_l = pad_up(p['w_decay_up'], 0)
    wi_h, wi_l = pad_up(p['w_iclr_up'], DECAY_RANK)
    wg_h, wg_l = pad_up(p['w_gate_up'], DECAY_RANK + ICLR_RANK)
    hd = np.arange(LANES) // RWKV_HEAD_DIM
    e128 = jnp.asarray(hd[:, None] == hd[None, :], BF16)
    row = lambda t: t.reshape(1, -1).astype(F32)
    mu = p['mu_shift']
    mu_rkv = row(mu[:3 * rwkv_w])
    mu_low = row(jnp.pad(mu[3 * rwkv_w:], (0, LOW_PAD - LOW_W)))
    return (mu_rkv, mu_low, row(p['w0']), row(p['a0']), row(p['k_k']), row(p['k_a']), row(p['r_k']),
            wd_h, wd_l, wi_h, wi_l, wg_h, wg_l, e128)


def _full_spec(a):
    nd = a.ndim
    return pl.BlockSpec(a.shape, lambda *_: (0,) * nd)


def _rwkv_prompt(rkv, low, par, ln_w, ln_b, n, s):
    m = n * s
    w = rkv.shape[1] // 3
    tm = min(256, s)
    chunk = WKV_CHUNK
    cpt = tm // chunk
    hd = np.arange(tm) // chunk
    ltri = jnp.asarray((hd[:, None] == hd[None, :]) & (np.arange(tm)[:, None] >= np.arange(tm)[None, :]), BF16)
    zeros_rkv = jnp.zeros((n, 1, 3 * w), F32)
    zeros_low = jnp.zeros((n, 1, LOW_PAD), F32)
    tiles_per_seq = s // tm
    tail = lambda width: pl.BlockSpec((8, width), lambda i: (jnp.maximum(i * (tm // 8) - 1, 0), 0))
    first = lambda width: pl.BlockSpec((None, 1, width), lambda i: (i // tiles_per_seq, 0, 0))
    big = pl.BlockSpec((tm, w), lambda i: (i, 0))
    outs = pl.pallas_call(
        functools.partial(_rwkv_prep_body, tm=tm, chunk=chunk, tiles_per_seq=tiles_per_seq),
        grid=(m // tm,),
        in_specs=[pl.BlockSpec((tm, 3 * w), lambda i: (i, 0)), pl.BlockSpec((tm, LOW_PAD), lambda i: (i, 0)),
                  tail(3 * w), tail(LOW_PAD), first(3 * w), first(LOW_PAD), _full_spec(ltri),
                  *[_full_spec(t) for t in par]],
        out_specs=[big] * 7 + [pl.BlockSpec((cpt, 1, w), lambda i: (i, 0, 0))],
        out_shape=[jax.ShapeDtypeStruct((m, w), BF16)] * 5 + [jax.ShapeDtypeStruct((m, w), F32)] * 2
        + [jax.ShapeDtypeStruct((m // chunk, 1, w), F32)],
        compiler_params=_cparams(("parallel",)),
        name="rwkv_prep",
    )(rkv, low, rkv, low, zeros_rkv, zeros_low, ltri, *par)
    at, bt, kt, rt, v, g, bon, ptot = outs

    tb = _pick(s, 512)
    nc = tb // chunk
    npairs = w // LANES
    nblk = s // tb
    tok = pl.BlockSpec((tb, w), lambda b, i: (b * nblk + i, 0))
    vec = pl.BlockSpec((1, w), lambda b, i: (0, 0))
    rw, st = pl.pallas_call(
        functools.partial(_wkv_chunk_body, nc=nc, chunk=chunk, npairs=npairs),
        grid=(n, nblk),
        in_specs=[tok, tok, tok, tok, tok,
                  pl.BlockSpec((nc, 1, w), lambda b, i: (b * nblk + i, 0, 0)),
                  tok, tok, vec, vec, _full_spec(par[-1])],
        out_specs=[tok, pl.BlockSpec((None, npairs, LANES, LANES), lambda b, i: (b, 0, 0, 0))],
        out_shape=[jax.ShapeDtypeStruct((m, w), BF16), jax.ShapeDtypeStruct((n, npairs, LANES, LANES), F32)],
        scratch_shapes=[pltpu.VMEM((npairs, LANES, LANES), F32)],
        compiler_params=_cparams(("parallel", "arbitrary")),
        name="wkv_chunks",
    )(at, bt, kt, rt, v, ptot, g, bon, ln_w.reshape(1, w), ln_b.reshape(1, w), par[-1])
    hp = LANES // RWKV_HEAD_DIM
    d = RWKV_HEAD_DIM
    st = jnp.stack([st[:, :, j * d:(j + 1) * d, j * d:(j + 1) * d] for j in range(hp)], axis=2)
    return rw, st.reshape(n, npairs * hp, d, d)


def _rwkv_sample(rkv, low, prev_rkv, prev_low, state0, par, ln_w, ln_b):
    db = rkv.shape[0]
    w = rkv.shape[1] // 3
    d = RWKV_HEAD_DIM
    nh = w // d
    ins = (rkv, low, prev_rkv, prev_low, *par)
    feats = pl.pallas_call(
        _rwkv_feat_body,
        in_specs=[_full_spec(t) for t in ins],
        out_specs=[pl.BlockSpec((db, w), lambda: (0, 0))] * 8,
        out_shape=[jax.ShapeDtypeStruct((db, w), F32)] * 8,
        compiler_params=pltpu.CompilerParams(vmem_limit_bytes=VMEM_LIMIT),
        name="rwkv_feats",
    )(*ins)
    hv = lambda t: t.reshape(db, nh, 1, d)
    row = pl.BlockSpec((None, nh, 1, d), lambda b: (b, 0, 0, 0))
    prow = pl.BlockSpec((nh, 1, d), lambda b: (0, 0, 0))
    mat = pl.BlockSpec((None, nh, d, d), lambda b: (b, 0, 0, 0))
    rw, s1 = pl.pallas_call(
        functools.partial(_wkv_step_body, nh=nh),
        grid=(db,),
        in_specs=[row] * 8 + [prow, prow, mat],
        out_specs=[row, mat],
        out_shape=[jax.ShapeDtypeStruct((db, nh, 1, d), BF16), jax.ShapeDtypeStruct((db, nh, d, d), F32)],
        compiler_params=_cparams(("parallel",)),
        name="wkv_step",
    )(*[hv(t) for t in feats], ln_w.reshape(nh, 1, d), ln_b.reshape(nh, 1, d), state0)
    return rw.reshape(db, w), s1


def _merge_body(attn_ref, rw_ref, ga_ref, gb_ref, wa_ref, wr_ref, o_ref):
    ya = _dg(attn_ref[...], wa_ref[...])
    yb = _dg(rw_ref[...], wr_ref[...])
    o_ref[...] = (ga_ref[...] * ya + gb_ref[...] * yb).astype(o_ref.dtype)


def _merge(attn, rw, gates, wa, wr, tm, tn):
    m, ka = attn.shape
    kr = rw.shape[1]
    d = wa.shape[1]
    nj = d // tn
    return pl.pallas_call(
        _merge_body,
        grid=(m // tm, nj),
        in_specs=[pl.BlockSpec((tm, ka), lambda i, j: (i, 0)), pl.BlockSpec((tm, kr), lambda i, j: (i, 0)),
                  pl.BlockSpec((tm, tn), lambda i, j: (i, j)), pl.BlockSpec((tm, tn), lambda i, j: (i, nj + j)),
                  pl.BlockSpec((ka, tn), lambda i, j: (0, j)), pl.BlockSpec((kr, tn), lambda i, j: (0, j))],
        out_specs=pl.BlockSpec((tm, tn), lambda i, j: (i, j)),
        out_shape=jax.ShapeDtypeStruct((m, d), BF16),
        compiler_params=_cparams(("parallel", "parallel")),
        name="merge",
    )(attn, rw, gates, gates, wa, wr)


def _resid_mm_body(a_ref, w_ref, x_ref, o_ref):
    o_ref[...] = x_ref[...] + _dg(a_ref[...], w_ref[...])


def _resid_matmul(a, w, x, tm, tn):
    m, k = a.shape
    n = w.shape[1]
    return pl.pallas_call(
        _resid_mm_body,
        grid=(m // tm, n // tn),
        in_specs=[pl.BlockSpec((tm, k), lambda i, j: (i, 0)), pl.BlockSpec((k, tn), lambda i, j: (0, j)),
                  pl.BlockSpec((tm, tn), lambda i, j: (i, j))],
        out_specs=pl.BlockSpec((tm, tn), lambda i, j: (i, j)),
        out_shape=jax.ShapeDtypeStruct((m, n), F32),
        compiler_params=_cparams(("parallel", "parallel")),
        name="out_proj",
    )(a, w, x)


def _ffn_up_body(x_ref, g_ref, wg_ref, wu_ref, o_ref, xn_ref):
    @pl.when(pl.program_id(1) == 0)
    def _():
        xn_ref[...] = _rms(x_ref[...], g_ref[...]).astype(BF16)

    xn = xn_ref[...]
    o_ref[...] = (jax.nn.silu(_dg(xn, wg_ref[...])) * _dg(xn, wu_ref[...])).astype(o_ref.dtype)


def _ffn_up(x, g, wg, wu, tm, tn):
    m, d = x.shape
    f = wg.shape[1]
    wspec = pl.BlockSpec((d, tn), lambda i, j: (0, j))
    return pl.pallas_call(
        _ffn_up_body,
        grid=(m // tm, f // tn),
        in_specs=[pl.BlockSpec((tm, d), lambda i, j: (i, 0)), pl.BlockSpec((1, d), lambda i, j: (0, 0)), wspec, wspec],
        out_specs=pl.BlockSpec((tm, tn), lambda i, j: (i, j)),
        out_shape=jax.ShapeDtypeStruct((m, f), BF16),
        scratch_shapes=[pltpu.VMEM((tm, d), BF16)],
        compiler_params=_cparams(("parallel", "arbitrary")),
        name="ffn_up",
    )(x, g, wg, wu)


def _ffn_down_body(h_ref, w_ref, x_ref, g_ref, o_ref, *, nj, tn):
    j = pl.program_id(1)
    val = x_ref[...] + _dg(h_ref[...], w_ref[...])
    for t in range(nj):
        @pl.when(j == t)
        def _(t=t):
            o_ref[:, t * tn:(t + 1) * tn] = val

    @pl.when(j == nj - 1)
    def _():
        o_ref[...] = _rms(o_ref[...], g_ref[...])


def _ffn_down(h, w, x, g, tm, tn):
    m, f = h.shape
    d = w.shape[1]
    nj = d // tn
    return pl.pallas_call(
        functools.partial(_ffn_down_body, nj=nj, tn=tn),
        grid=(m // tm, nj),
        in_specs=[pl.BlockSpec((tm, f), lambda i, j: (i, 0)), pl.BlockSpec((f, tn), lambda i, j: (0, j)),
                  pl.BlockSpec((tm, tn), lambda i, j: (i, j)), pl.BlockSpec((1, d), lambda i, j: (0, 0))],
        out_specs=pl.BlockSpec((tm, d), lambda i, j: (i, 0)),
        out_shape=jax.ShapeDtypeStruct((m, d), F32),
        compiler_params=_cparams(("parallel", "arbitrary")),
        name="ffn_down",
    )(h, w, x, g)


def _pick(total, pref):
    t = min(pref, total)
    while total % t:
        t //= 2
    return t


def _layer(x2, n, s, pos, wts, par, p, attend, rwkv):
    m, d = x2.shape
    att_w = wts['qkv'].shape[1] // 3
    rw_w = wts['rkv'].shape[1] // 3
    tm = _pick(s, 1024) if s % 8 == 0 else _pick(m, 1024)
    assert s % tm == 0 or tm % s == 0
    g_mix = p['g_mix'].reshape(1, d)

    tq = _pick(s, 512) if s % 8 == 0 else _pick(m, 512)
    assert s % tq == 0 or tq % s == 0
    tabs = _rope_tables(pos)
    if s < tq:
        tabs = tuple(jnp.tile(t, (tq // s, 1)) for t in tabs)
    qkv, xn, k3, v3 = _proj_qkv(x2, g_mix, wts['qkv'], tabs, tq, max(s // tq, 1))
    rkv = _matmul(xn, wts['rkv'], tm, rw_w, 'proj_rkv')
    low = _matmul(xn, wts['low'], tm, LOW_PAD, 'proj_low')
    tn_g = _pick(2 * d, 1024)
    gates = _matmul(xn, wts['gate'], tm, tn_g, 'proj_gates', _epi_sigmoid_bias, (p['b_gate'].reshape(1, 2 * d),),
                    (pl.BlockSpec((1, tn_g), lambda i, j: (0, j)),), out_dtype=BF16)

    attn = attend(qkv)
    rw, wkv_new = rwkv(rkv, low)

    merged = _merge(attn, rw, gates, wts['proj_attn'], wts['proj_rwkv'], tm, _pick(d, 1024))
    hid = _resid_matmul(merged, wts['out'], x2, tm, _pick(d, 1024))
    f = wts['ffn_gate'].shape[1]
    h = _ffn_up(hid, p['g_ffn'].reshape(1, d), wts['ffn_gate'], wts['ffn_up'], tm, _pick(f, 512))
    y = _ffn_down(h, wts['ffn_down'], hid, p['g_final'].reshape(1, d), _pick(m, 512), _pick(d, 1024))

    sh_last = jnp.concatenate([rkv.reshape(n, s, -1)[:, -1], low.reshape(n, s, -1)[:, -1, :LOW_W]], axis=1)
    return y, k3, v3, wkv_new, sh_last


def kernel(x_prompt, x_sample, cache_k, cache_v, state_wkv, state_shift, page_table, g_mix, w_in, b_gate, mu_shift, w0, w_decay_up, a0, w_iclr_up, w_gate_up, k_k, k_a, r_k, ln_x_w, ln_x_b, w_proj_attn, w_proj_rwkv, w_out, g_ffn, w_ffn_gate, w_ffn_up, w_ffn_down, g_final):
    depth = w_in.shape[0]
    assert depth == 1, "single-layer trunk"
    n, s, d = x_prompt.shape
    db, ds, _ = x_sample.shape
    assert ds == 1, "one new token per decode sequence"
    _, n_pool, page, nh, dh = cache_k.shape
    assert page == PAGE_SIZE and dh == ATT_HEAD_DIM
    att_w = nh * dh
    rw_w = w0.shape[1]
    n_pages = page_table.shape[1]
    past = n_pages * PAGE_SIZE
    assert s % MOBA_BLOCK == 0 and past % MOBA_BLOCK == 0
    l = 0

    wi = w_in[l]
    o = 3 * att_w
    wts = {
        'qkv': wi[:, :o].astype(BF16),
        'rkv': wi[:, o:o + 3 * rw_w].astype(BF16),
        'low': jnp.pad(wi[:, o + 3 * rw_w:o + 3 * rw_w + LOW_W], ((0, 0), (0, LOW_PAD - LOW_W))).astype(BF16),
        'gate': wi[:, o + 3 * rw_w + LOW_W:].astype(BF16),
        'proj_attn': w_proj_attn[l].astype(BF16), 'proj_rwkv': w_proj_rwkv[l].astype(BF16),
        'out': w_out[l].astype(BF16), 'ffn_gate': w_ffn_gate[l].astype(BF16),
        'ffn_up': w_ffn_up[l].astype(BF16), 'ffn_down': w_ffn_down[l].astype(BF16),
    }
    p = {'g_mix': g_mix[l], 'b_gate': b_gate[l], 'g_ffn': g_ffn[l], 'g_final': g_final,
         'mu_shift': mu_shift[l], 'w0': w0[l], 'w_decay_up': w_decay_up[l], 'a0': a0[l],
         'w_iclr_up': w_iclr_up[l], 'w_gate_up': w_gate_up[l], 'k_k': k_k[l], 'k_a': k_a[l], 'r_k': r_k[l]}
    par = _rwkv_params(p, rw_w)
    ln_w, ln_b = ln_x_w[l], ln_x_b[l]

    blk_sums = []

    def attend_prompt(qkv):
        attn, bs = _moba_prompt(qkv, n, s, page_table, cache_k, l)
        blk_sums.append(bs)
        return attn

    yp, kp, vp, wp, sp = _layer(
        x_prompt.reshape(n * s, d), n, s, np.arange(s), wts, par, p, attend_prompt,
        lambda rkv, low: _rwkv_prompt(rkv, low, par, ln_w, ln_b, n, s))

    q_blk = past // MOBA_BLOCK

    def attend_sample(qkv):
        q, k_new, v_new = qkv[:, :att_w], qkv[:, att_w:2 * att_w], qkv[:, 2 * att_w:]
        blk_sum = blk_sums[0].reshape(db, q_blk, att_w)
        sel = _decode_select(blk_sum, q, q_blk)
        return _decode_attn(page_table, sel, q, k_new, v_new, cache_k, cache_v, l)

    sh_prev = state_shift[l]
    prev_rkv = sh_prev[:, :3 * rw_w]
    prev_low = jnp.pad(sh_prev[:, 3 * rw_w:], ((0, 0), (0, LOW_PAD - LOW_W)))
    ys, ks_, vs_, ws_, ss_ = _layer(
        x_sample.reshape(db * ds, d), db, ds, past + np.arange(ds), wts, par, p,
        attend_sample,
        lambda rkv, low: _rwkv_sample(rkv, low, prev_rkv, prev_low, state_wkv[l], par, ln_w, ln_b))

    return (yp.reshape(n, s, d), ys.reshape(db, ds, d),
            kp.reshape(1, n, s, nh, dh), vp.reshape(1, n, s, nh, dh),
            wp.astype(state_wkv.dtype)[None], sp.astype(state_shift.dtype)[None],
            ks_.reshape(1, db, ds, nh, dh), vs_.reshape(1, db, ds, nh, dh),
            ws_.astype(state_wkv.dtype)[None], ss_.astype(state_shift.dtype)[None])
```
